```python
import jax, jax.numpy as jnp
from jax import lax
import numpy as np

D_MODEL = 1024
BATCH = 8
SEQ = 8192
DEPTH = 1
DEC_BATCH = 4
DEC_SEQ = 8192
PAST_LEN = 128

GRID_W = 64
MIX_WIDTH = D_MODEL
MLSTM_WIDTH = MIX_WIDTH // 2
MLSTM_HEADS = 4
MLSTM_HEAD_DIM = MLSTM_WIDTH // MLSTM_HEADS
MLSTM_CHUNK = 128
NA_WIDTH = MIX_WIDTH - MLSTM_WIDTH
NA_HEAD_DIM = 64
NA_HEADS = NA_WIDTH // NA_HEAD_DIM
NA_MAX_KH = 8
NA_KW = 16
N_GATE_COLS = 2 * 2 * MLSTM_HEADS
IN_SPLITS = (MLSTM_WIDTH, MLSTM_WIDTH, MLSTM_WIDTH, MLSTM_WIDTH, N_GATE_COLS, NA_WIDTH, NA_WIDTH, NA_WIDTH)
IN_COLS = sum(IN_SPLITS)
IN_SPLIT_POINTS = tuple(int(s) for s in np.cumsum(IN_SPLITS)[:-1])
N_EXPERTS = 16
EC_CAPACITY_FACTOR = 2
D_EXPERT = D_MODEL
N_MOD = 6
RMS_EPS = 1e-6

kernel_name = "hymba_mlstm_natten_ec_encoder"


def rmsnorm(x, g):
    xf = x.astype(jnp.float32)
    y = xf * lax.rsqrt(jnp.mean(xf * xf, axis=-1, keepdims=True) + RMS_EPS)
    return (y * g.astype(jnp.float32)).astype(x.dtype)


def mlstm_chunkwise(q, k, v, log_i, log_f):
    B, H, T, d = q.shape
    L = MLSTM_CHUNK
    nc = T // L
    q = q.astype(jnp.float32).reshape(B, H, nc, L, d)
    k = (k.astype(jnp.float32) * (d ** -0.5)).reshape(B, H, nc, L, d)
    v = v.astype(jnp.float32).reshape(B, H, nc, L, d)
    li = log_i.astype(jnp.float32).reshape(B, H, nc, L)
    lf = log_f.astype(jnp.float32).reshape(B, H, nc, L)
    b = jnp.cumsum(lf, axis=-1)
    g = b[..., -1]
    a = g[..., None] - b + li
    m_loc = jnp.max(a, axis=-1)
    w = jnp.exp(a - m_loc[..., None])
    C_loc = jnp.einsum('bhcs,bhcsv,bhcsk->bhcvk', w, v, k)
    n_loc = jnp.einsum('bhcs,bhcsk->bhck', w, k)

    def step(carry, inp):
        C, n, m = carry
        g_c, m_l, C_l, n_l = inp
        m_new = jnp.maximum(g_c + m, m_l)
        s_old = jnp.exp(g_c + m - m_new)
        s_loc = jnp.exp(m_l - m_new)
        C_new = s_old[..., None, None] * C + s_loc[..., None, None] * C_l
        n_new = s_old[..., None] * n + s_loc[..., None] * n_l
        return (C_new, n_new, m_new), (C, n, m)

    init = (jnp.zeros((B, H, d, d), jnp.float32), jnp.zeros((B, H, d), jnp.float32), jnp.zeros((B, H), jnp.float32))
    xs = (jnp.moveaxis(g, 2, 0), jnp.moveaxis(m_loc, 2, 0), jnp.moveaxis(C_loc, 2, 0), jnp.moveaxis(n_loc, 2, 0))
    _, (C_s, n_s, m_s) = lax.scan(step, init, xs)
    C_s = jnp.moveaxis(C_s, 0, 2)
    n_s = jnp.moveaxis(n_s, 0, 2)
    m_s = jnp.moveaxis(m_s, 0, 2)

    tril = np.tril(np.ones((L, L), dtype=bool))
    Dlog = jnp.where(tril, b[..., :, None] - b[..., None, :] + li[..., None, :], -jnp.inf)
    inter = b + m_s[..., None]
    m_out = jnp.maximum(jnp.max(Dlog, axis=-1), inter)
    S = jnp.einsum('bhcjd,bhcsd->bhcjs', q, k) * jnp.exp(Dlog - m_out[..., None])
    s_int = jnp.exp(inter - m_out)
    num = jnp.einsum('bhcjs,bhcsv->bhcjv', S, v) + s_int[..., None] * jnp.einsum('bhcvk,bhcjk->bhcjv', C_s, q)
    den = jnp.sum(S, axis=-1) + s_int * jnp.einsum('bhck,bhcjk->bhcj', n_s, q)
    h = num / jnp.maximum(jnp.abs(den), jnp.exp(-m_out))[..., None]
    return h.reshape(B, H, T, d)


def neighbourhood_attention(q, k, v, rel_bias):
    B, T, H, d = q.shape
    rows = T // GRID_W
    kh = min(NA_MAX_KH, rows)
    qg = q.reshape(B, rows, GRID_W, H, d)
    kg = k.reshape(B, rows, GRID_W, H, d)
    vg = v.reshape(B, rows, GRID_W, H, d)
    cols = np.arange(GRID_W)
    col_start = np.clip(cols - NA_KW // 2, 0, GRID_W - NA_KW)
    col_idx = col_start[:, None] + np.arange(NA_KW)[None, :]
    dc_idx = col_idx - cols[:, None] + (NA_KW - 1)
    bias_c = rel_bias[:, :, dc_idx]
    scale = d ** -0.5

    def row_fn(r):
        r0 = jnp.clip(r - kh // 2, 0, rows - kh)
        k_rows = lax.dynamic_slice_in_dim(kg, r0, kh, axis=1)
        v_rows = lax.dynamic_slice_in_dim(vg, r0, kh, axis=1)
        k_win = k_rows[:, :, col_idx]
        v_win = v_rows[:, :, col_idx]
        q_r = lax.dynamic_index_in_dim(qg, r, axis=1, keepdims=False)
        s = jnp.einsum('bchd,bicjhd->bhcij', q_r, k_win).astype(jnp.float32) * scale
        dr_idx = r0 + jnp.arange(kh) - r + (NA_MAX_KH - 1)
        bias = jnp.take(bias_c, dr_idx, axis=1).transpose(0, 2, 1, 3)
        s = s + bias[None].astype(jnp.float32)
        p = jax.nn.softmax(s.reshape(B, H, GRID_W, kh * NA_KW), axis=-1).reshape(B, H, GRID_W, kh, NA_KW)
        return jnp.einsum('bhcij,bicjhd->bchd', p.astype(v.dtype), v_win)

    out = lax.map(row_fn, jnp.arange(rows))
    return out.transpose(1, 0, 2, 3, 4).reshape(B, T, H * d)


def token_mixer(h, w_in, b_gates, g_mlstm, na_rel_bias, w_out):
    B, T, _ = h.shape
    proj = h @ w_in
    q_m, k_m, v_m, o_m, gates, q_n, k_n, v_n = jnp.split(proj, IN_SPLIT_POINTS, axis=-1)
    to_heads = lambda t: t.reshape(B, T, MLSTM_HEADS, MLSTM_HEAD_DIM).transpose(0, 2, 1, 3)
    q_m, k_m, v_m = to_heads(q_m), to_heads(k_m), to_heads(v_m)
    gt = gates.astype(jnp.float32).reshape(B, T, 2, 2, MLSTM_HEADS) + b_gates.astype(jnp.float32)
    gt = jnp.transpose(gt, (2, 3, 0, 4, 1))
    h_fwd = mlstm_chunkwise(q_m, k_m, v_m, gt[0, 0], jax.nn.log_sigmoid(gt[0, 1]))
    flip = lambda t: jnp.flip(t, axis=2)
    h_bwd = flip(mlstm_chunkwise(flip(q_m), flip(k_m), flip(v_m), jnp.flip(gt[1, 0], axis=-1),
                                 jnp.flip(jax.nn.log_sigmoid(gt[1, 1]), axis=-1)))
    hm = (h_fwd + h_bwd).transpose(0, 2, 1, 3)
    hm = hm * lax.rsqrt(jnp.mean(hm * hm, axis=-1, keepdims=True) + RMS_EPS) * g_mlstm.astype(jnp.float32)
    hm = (hm.reshape(B, T, MLSTM_WIDTH) * jax.nn.sigmoid(o_m.astype(jnp.float32))).astype(h.dtype)
    to_na = lambda t: t.reshape(B, T, NA_HEADS, NA_HEAD_DIM)
    hn = neighbourhood_attention(to_na(q_n), to_na(k_n), to_na(v_n), na_rel_bias)
    return jnp.concatenate([hm, hn.astype(h.dtype)], axis=-1) @ w_out


def expert_choice_moe(h, w_router, w_gate, w_up, w_down):
    B, T, D = h.shape
    n = B * T
    xf = h.reshape(n, D)
    aff = jax.nn.softmax((xf @ w_router).astype(jnp.float32), axis=-1)
    cap = EC_CAPACITY_FACTOR * n // N_EXPERTS
    gate, idx = lax.top_k(aff.T, cap)
    xe = xf[idx]
    hid = jax.nn.silu(jnp.einsum('ecd,edf->ecf', xe, w_gate)) * jnp.einsum('ecd,edf->ecf', xe, w_up)
    ye = jnp.einsum('ecf,efd->ecd', hid, w_down) * gate[..., None].astype(h.dtype)
    y = jnp.zeros_like(xf).at[idx.reshape(-1)].add(ye.reshape(-1, D))
    return y.reshape(B, T, D)


def encoder_layer(x, c, w_ada, b_ada, g_pre_mix, g_post_mix, w_in, b_gates, g_mlstm, na_rel_bias, w_out,
                  g_pre_ffn, g_post_ffn, w_router, w_expert_gate, w_expert_up, w_expert_down):
    B = x.shape[0]
    mod = (jax.nn.silu(c) @ w_ada + b_ada).reshape(B, N_MOD, D_MODEL)[:, :, None, :]
    shift1, scale1, gate1, shift2, scale2, gate2 = [mod[:, i] for i in range(N_MOD)]
    h = rmsnorm(x, g_pre_mix) * (1 + scale1) + shift1
    x = x + gate1 * rmsnorm(token_mixer(h, w_in, b_gates, g_mlstm, na_rel_bias, w_out), g_post_mix)
    h = rmsnorm(x, g_pre_ffn) * (1 + scale2) + shift2
    x = x + gate2 * rmsnorm(expert_choice_moe(h, w_router, w_expert_gate, w_expert_up, w_expert_down), g_post_ffn)
    return x


def trunk(x, c, w_ada, b_ada, g_pre_mix, g_post_mix, w_in, b_gates, g_mlstm, na_rel_bias, w_out,
          g_pre_ffn, g_post_ffn, w_router, w_expert_gate, w_expert_up, w_expert_down):
    for l in range(DEPTH):
        x = encoder_layer(x, c, w_ada[l], b_ada[l], g_pre_mix[l], g_post_mix[l], w_in[l], b_gates[l], g_mlstm[l],
                          na_rel_bias[l], w_out[l], g_pre_ffn[l], g_post_ffn[l], w_router[l],
                          w_expert_gate[l], w_expert_up[l], w_expert_down[l])
    return x


def setup_inputs(seed: int = 0) -> dict:
    key = jax.random.key(seed)
    ks = jax.random.split(key, 24)
    nrm = lambda k, shape, s: jax.random.normal(k, shape, jnp.float32) * s
    gain = lambda k, shape: 1.0 + 0.05 * jax.random.normal(k, shape, jnp.float32)
    b_i = nrm(ks[10], (DEPTH, 2, MLSTM_HEADS), 0.1)
    b_f = jnp.linspace(3.0, 6.0, MLSTM_HEADS) + nrm(ks[11], (DEPTH, 2, MLSTM_HEADS), 0.1)
    return {
        'x_prompt': nrm(ks[0], (BATCH, SEQ, D_MODEL), 1.0),
        'x_sample': nrm(ks[1], (DEC_BATCH, DEC_SEQ, D_MODEL), 1.0),
        'c_prompt': nrm(ks[2], (BATCH, D_MODEL), 1.0),
        'c_sample': nrm(ks[3], (DEC_BATCH, D_MODEL), 1.0),
        'w_ada': nrm(ks[4], (DEPTH, D_MODEL, N_MOD * D_MODEL), 0.5 * D_MODEL ** -0.5),
        'b_ada': nrm(ks[5], (DEPTH, N_MOD * D_MODEL), 0.01),
        'g_pre_mix': gain(ks[6], (DEPTH, D_MODEL)),
        'g_post_mix': gain(ks[7], (DEPTH, D_MODEL)),
        'w_in': nrm(ks[8], (DEPTH, D_MODEL, IN_COLS), D_MODEL ** -0.5),
        'b_gates': jnp.stack([b_i, b_f], axis=2),
        'g_mlstm': gain(ks[12], (DEPTH, MLSTM_HEADS, MLSTM_HEAD_DIM)),
        'na_rel_bias': nrm(ks[13], (DEPTH, NA_HEADS, 2 * NA_MAX_KH - 1, 2 * NA_KW - 1), 0.1),
        'w_out': nrm(ks[14], (DEPTH, MIX_WIDTH, D_MODEL), MIX_WIDTH ** -0.5),
        'g_pre_ffn': gain(ks[15], (DEPTH, D_MODEL)),
        'g_post_ffn': gain(ks[16], (DEPTH, D_MODEL)),
        'w_router': nrm(ks[17], (DEPTH, D_MODEL, N_EXPERTS), D_MODEL ** -0.5),
        'w_expert_gate': nrm(ks[18], (DEPTH, N_EXPERTS, D_MODEL, D_EXPERT), D_MODEL ** -0.5),
        'w_expert_up': nrm(ks[19], (DEPTH, N_EXPERTS, D_MODEL, D_EXPERT), D_MODEL ** -0.5),
        'w_expert_down': nrm(ks[20], (DEPTH, N_EXPERTS, D_EXPERT, D_MODEL), D_EXPERT ** -0.5),
    }


def reference(x_prompt, x_sample, c_prompt, c_sample, w_ada, b_ada, g_pre_mix, g_post_mix, w_in, b_gates,
              g_mlstm, na_rel_bias, w_out, g_pre_ffn, g_post_ffn, w_router, w_expert_gate, w_expert_up,
              w_expert_down):
    y_prompt = trunk(x_prompt, c_prompt, w_ada, b_ada, g_pre_mix, g_post_mix, w_in, b_gates, g_mlstm, na_rel_bias,
                     w_out, g_pre_ffn, g_post_ffn, w_router, w_expert_gate, w_expert_up, w_expert_down)
    y_sample = trunk(x_sample, c_sample, w_ada, b_ada, g_pre_mix, g_post_mix, w_in, b_gates, g_mlstm, na_rel_bias,
                     w_out, g_pre_ffn, g_post_ffn, w_router, w_expert_gate, w_expert_up, w_expert_down)
    return (y_prompt, y_sample)
```

```python
import functools
import math

import jax
import jax.numpy as jnp
from jax import lax
from jax.experimental import pallas as pl
from jax.experimental.pallas import tpu as pltpu

F32 = jnp.float32
BF16 = jnp.bfloat16

LANES = 128
SUBLANES = 8
BF16_ROWS = 16
VMEM_LIMIT = 56 * 1024 * 1024

RMS_EPS = 1e-6
N_MOD = 6
MLSTM_HEADS = 4
MLSTM_HEAD_DIM = 128
MLSTM_CHUNK = 128
MLSTM_WIDTH = MLSTM_HEADS * MLSTM_HEAD_DIM
NA_HEAD_DIM = 64
NA_HEADS = 8
NA_WIDTH = NA_HEADS * NA_HEAD_DIM
NA_MAX_KH = 8
NA_KW = 16
GRID_W = 64
N_EXPERTS = 16
EC_CAPACITY_FACTOR = 2
NEG_BIG = -1e30


def _cparams(*sem):
    return pltpu.CompilerParams(dimension_semantics=sem, vmem_limit_bytes=VMEM_LIMIT)


def _rms(x):
    return x * lax.rsqrt(jnp.mean(x * x, axis=-1, keepdims=True) + RMS_EPS)


def _mod_kernel(c_ref, w_ref, b_ref, o_ref):
    c = c_ref[...]
    a = c * jax.nn.sigmoid(c)
    o_ref[...] = jnp.dot(a, w_ref[...], preferred_element_type=F32,
                         precision=lax.Precision.HIGHEST) + b_ref[...]


def _mod(c, w_ada, b_ada):
    bsz, d = c.shape
    n = w_ada.shape[1]
    bn = 1024
    out = pl.pallas_call(
        _mod_kernel,
        grid=(n // bn,),
        in_specs=[pl.BlockSpec((bsz, d), lambda j: (0, 0)),
                  pl.BlockSpec((d, bn), lambda j: (0, j)),
                  pl.BlockSpec((1, bn), lambda j: (0, j))],
        out_specs=pl.BlockSpec((bsz, bn), lambda j: (0, j)),
        out_shape=jax.ShapeDtypeStruct((bsz, n), F32),
        compiler_params=_cparams("arbitrary"),
        name="mod",
    )(c, w_ada, b_ada.reshape(1, n))
    return out.reshape(bsz, N_MOD, d)


def _inproj_kernel(x_ref, mod_ref, g_ref, wm_ref, wo_ref, wg_ref, wn_ref,
                   qkv_ref, o_ref, gates_ref, na_ref):
    x = x_ref[...]
    shift = mod_ref[0, 0:1, :]
    scale = mod_ref[0, 1:2, :]
    h = (_rms(x) * g_ref[...]) * (1.0 + scale) + shift
    hb = h.astype(BF16)
    qkv_ref[...] = jnp.dot(hb, wm_ref[...], preferred_element_type=F32).astype(BF16)
    o_ref[...] = jnp.dot(hb, wo_ref[...], preferred_element_type=F32).astype(BF16)
    gates_ref[...] = jnp.dot(hb, wg_ref[...], preferred_element_type=F32)
    na_ref[...] = jnp.dot(hb, wn_ref[...], preferred_element_type=F32).astype(BF16)


def _inproj(x2d, mod, g_pre, w_m, w_o, w_g, w_n, seq, tm=512):
    n, d = x2d.shape
    tiles_per_b = seq // tm
    full = lambda a: pl.BlockSpec(a.shape, lambda i: (0, 0))
    return pl.pallas_call(
        _inproj_kernel,
        grid=(n // tm,),
        in_specs=[pl.BlockSpec((tm, d), lambda i: (i, 0)),
                  pl.BlockSpec((1, N_MOD, d), lambda i: (i // tiles_per_b, 0, 0)),
                  full(g_pre), full(w_m), full(w_o), full(w_g), full(w_n)],
        out_specs=[pl.BlockSpec((tm, w_m.shape[1]), lambda i: (i, 0)),
                   pl.BlockSpec((tm, w_o.shape[1]), lambda i: (i, 0)),
                   pl.BlockSpec((tm, w_g.shape[1]), lambda i: (i, 0)),
                   pl.BlockSpec((tm, w_n.shape[1]), lambda i: (i, 0))],
        out_shape=[jax.ShapeDtypeStruct((n, w_m.shape[1]), BF16),
                   jax.ShapeDtypeStruct((n, w_o.shape[1]), BF16),
                   jax.ShapeDtypeStruct((n, w_g.shape[1]), F32),
                   jax.ShapeDtypeStruct((n, w_n.shape[1]), BF16)],
        compiler_params=_cparams("arbitrary"),
        name="inproj",
    )(x2d, mod, g_pre, w_m, w_o, w_g, w_n)


def _mlstm_chain(q, k, v, b_col, li_col, r_row, g, causal_mask, c_ref, n_ref, m_ref, ci):
    scale = MLSTM_HEAD_DIM ** -0.5
    c_st = c_ref[ci]
    n_st = n_ref[ci]
    m_st = m_ref[ci][:, 0:1]
    dlog = jnp.where(causal_mask, b_col + r_row, NEG_BIG)
    inter = b_col + m_st
    m_out = jnp.maximum(jnp.max(dlog, axis=-1, keepdims=True), inter)
    p = jnp.exp(dlog - m_out)
    s = lax.dot_general(q, k, (((1,), (1,)), ((), ())), preferred_element_type=F32) * scale * p
    s_int = jnp.exp(inter - m_out)
    qc = lax.dot_general(q, c_st.astype(BF16), (((1,), (1,)), ((), ())), preferred_element_type=F32)
    num = jnp.dot(s.astype(BF16), v, preferred_element_type=F32) + s_int * qc
    qn = jnp.sum(q.astype(F32) * n_st, axis=-1, keepdims=True)
    den = jnp.sum(s, axis=-1, keepdims=True) + s_int * qn
    h = num / jnp.maximum(jnp.abs(den), jnp.exp(-m_out))

    a_col = g - b_col + li_col
    m_loc = jnp.max(a_col, axis=0, keepdims=True)
    w_col = jnp.exp(a_col - m_loc)
    wv = (w_col * v.astype(F32)).astype(BF16)
    c_loc = lax.dot_general(wv, k, (((0,), (0,)), ((), ())), preferred_element_type=F32) * scale
    n_loc = jnp.sum(w_col * k.astype(F32), axis=0, keepdims=True) * scale
    m_new = jnp.maximum(g + m_st, m_loc)
    s_old = jnp.exp(g + m_st - m_new)
    s_loc = jnp.exp(m_loc - m_new)
    c_ref[ci] = s_old * c_st + s_loc * c_loc
    n_ref[ci] = s_old * n_st + s_loc * n_loc
    m_ref[ci] = jnp.broadcast_to(m_new, (1, LANES))
    return h


def _mlstm_kernel(qkvf_ref, qkvb_ref, gf_ref, gb_ref, bias_ref, hf_ref, hb_ref, c_ref, n_ref, m_ref):
    L = MLSTM_CHUNK

    @pl.when(pl.program_id(1) == 0)
    def _():
        c_ref[...] = jnp.zeros_like(c_ref)
        n_ref[...] = jnp.zeros_like(n_ref)
        m_ref[...] = jnp.zeros_like(m_ref)

    row = lax.broadcasted_iota(jnp.int32, (L, L), 0)
    col = lax.broadcasted_iota(jnp.int32, (L, L), 1)
    tri_lower = (col <= row)
    tri_upper = (col >= row)
    hi = lax.Precision.HIGHEST
    nhd = MLSTM_HEADS

    for d, (qkv_ref, g_ref, o_ref, tri) in enumerate(
            ((qkvf_ref, gf_ref, hf_ref, tri_lower), (qkvb_ref, gb_ref, hb_ref, tri_upper))):
        gt = g_ref[0] + bias_ref[...]
        lf = jax.nn.log_sigmoid(gt)
        bc = jnp.dot(tri.astype(F32), lf, preferred_element_type=F32, precision=hi)
        gt_t = gt.T
        bc_t = bc.T
        tot = bc[L - 1:L, :] if d == 0 else bc[0:1, :]
        for hh in range(nhd):
            ci = d * nhd + hh
            fcol = 2 * nhd + ci
            q = qkv_ref[0, :, hh * MLSTM_HEAD_DIM:(hh + 1) * MLSTM_HEAD_DIM]
            k = qkv_ref[0, :, MLSTM_WIDTH + hh * MLSTM_HEAD_DIM:MLSTM_WIDTH + (hh + 1) * MLSTM_HEAD_DIM]
            v = qkv_ref[0, :, 2 * MLSTM_WIDTH + hh * MLSTM_HEAD_DIM:2 * MLSTM_WIDTH + (hh + 1) * MLSTM_HEAD_DIM]
            b_col = bc[:, fcol:fcol + 1]
            li_col = gt[:, ci:ci + 1]
            r_row = gt_t[ci:ci + 1, :] - bc_t[fcol:fcol + 1, :]
            g = tot[:, fcol:fcol + 1]
            h = _mlstm_chain(q, k, v, b_col, li_col, r_row, g, tri, c_ref, n_ref, m_ref, ci)
            o_ref[0, :, hh * MLSTM_HEAD_DIM:(hh + 1) * MLSTM_HEAD_DIM] = h.astype(o_ref.dtype)


def _mlstm(qkv, gates, gate_bias):
    bsz, seq, _ = qkv.shape
    L = MLSTM_CHUNK
    nc = seq // L
    nchain = 2 * MLSTM_HEADS
    fwd = lambda b, c: (b, c, 0)
    bwd = lambda b, c: (b, nc - 1 - c, 0)
    return pl.pallas_call(
        _mlstm_kernel,
        grid=(bsz, nc),
        in_specs=[pl.BlockSpec((1, L, 3 * MLSTM_WIDTH), fwd),
                  pl.BlockSpec((1, L, 3 * MLSTM_WIDTH), bwd),
                  pl.BlockSpec((1, L, LANES), fwd),
                  pl.BlockSpec((1, L, LANES), bwd),
                  pl.BlockSpec((1, LANES), lambda b, c: (0, 0))],
        out_specs=[pl.BlockSpec((1, L, MLSTM_WIDTH), fwd),
                   pl.BlockSpec((1, L, MLSTM_WIDTH), bwd)],
        out_shape=[jax.ShapeDtypeStruct((bsz, seq, MLSTM_WIDTH), BF16),
                   jax.ShapeDtypeStruct((bsz, seq, MLSTM_WIDTH), BF16)],
        scratch_shapes=[pltpu.VMEM((nchain, MLSTM_HEAD_DIM, MLSTM_HEAD_DIM), F32),
                        pltpu.VMEM((nchain, 1, MLSTM_HEAD_DIM), F32),
                        pltpu.VMEM((nchain, 1, LANES), F32)],
        compiler_params=_cparams("arbitrary", "arbitrary"),
        name="mlstm",
    )(qkv, qkv, gates, gates, gate_bias)


NA_ROWS_PER_STEP = 8
NA_PAIR = 2 * GRID_W


def _natten_kernel(q_ref, k_ref, v_ref, bias_ref, o_ref, *, rows):
    i = pl.program_id(1)
    scale = NA_HEAD_DIM ** -0.5
    lane = lax.broadcasted_iota(jnp.int32, (GRID_W, LANES), 1)
    npair = NA_MAX_KH // 2

    def row_body(rr, carry):
        r = i * NA_ROWS_PER_STEP + rr
        r0 = jnp.clip(r - NA_MAX_KH // 2, 0, rows - NA_MAX_KH)
        base = pl.multiple_of(r0 * GRID_W, GRID_W)
        dr0 = r0 - r + (NA_MAX_KH - 1)
        qoff = pl.multiple_of(rr * GRID_W, GRID_W)
        for hp in range(NA_HEADS // 2):
            lsl = slice(hp * LANES, (hp + 1) * LANES)
            qp = q_ref[0, pl.ds(qoff, GRID_W), lsl]
            kw = [k_ref[0, pl.ds(base + p * NA_PAIR, NA_PAIR), lsl] for p in range(npair)]
            vw = [v_ref[0, pl.ds(base + p * NA_PAIR, NA_PAIR), lsl] for p in range(npair)]
            outs = []
            for sub in range(2):
                h = 2 * hp + sub
                own = (lane >= sub * NA_HEAD_DIM) & (lane < (sub + 1) * NA_HEAD_DIM)
                qm = jnp.where(own, qp, jnp.zeros_like(qp))
                s = [lax.dot_general(qm, kw[p], (((1,), (1,)), ((), ())), preferred_element_type=F32) * scale
                     + bias_ref[h, dr0 + 2 * p] for p in range(npair)]
                m = jnp.maximum(jnp.maximum(s[0], s[1]), jnp.maximum(s[2], s[3]))
                m = jnp.max(m, axis=-1, keepdims=True)
                e = [jnp.exp(sp - m) for sp in s]
                l = jnp.sum(e[0] + e[1] + e[2] + e[3], axis=-1, keepdims=True)
                acc = jnp.dot(e[0].astype(BF16), vw[0], preferred_element_type=F32)
                for p in range(1, npair):
                    acc = acc + jnp.dot(e[p].astype(BF16), vw[p], preferred_element_type=F32)
                outs.append(acc / l)
            o_ref[0, pl.ds(qoff, GRID_W), lsl] = jnp.where(lane < NA_HEAD_DIM, outs[0], outs[1]).astype(o_ref.dtype)
        return carry

    lax.fori_loop(0, NA_ROWS_PER_STEP, row_body, 0)


def _natten_bias_table(rel_bias):
    cols = jnp.arange(GRID_W)
    col_start = jnp.clip(cols - NA_KW // 2, 0, GRID_W - NA_KW)
    kcol = jnp.arange(GRID_W)
    dc = kcol[None, :] - cols[:, None] + (NA_KW - 1)
    inwin = (kcol[None, :] >= col_start[:, None]) & (kcol[None, :] < col_start[:, None] + NA_KW)
    tab = rel_bias.astype(F32)[:, :, jnp.clip(dc, 0, 2 * NA_KW - 2)]
    tab = jnp.where(inwin[None, None], tab, NEG_BIG)
    return jnp.concatenate([tab[:, :-1], tab[:, 1:]], axis=-1)


def _natten(na, bias_table):
    bsz, seq, _ = na.shape
    rows = seq // GRID_W
    tq = NA_ROWS_PER_STEP * GRID_W
    return pl.pallas_call(
        functools.partial(_natten_kernel, rows=rows),
        grid=(bsz, rows // NA_ROWS_PER_STEP),
        in_specs=[pl.BlockSpec((1, tq, NA_WIDTH), lambda b, i: (b, i, 0)),
                  pl.BlockSpec((1, seq, NA_WIDTH), lambda b, i: (b, 0, 1)),
                  pl.BlockSpec((1, seq, NA_WIDTH), lambda b, i: (b, 0, 2)),
                  pl.BlockSpec(bias_table.shape, lambda b, i: (0, 0, 0, 0))],
        out_specs=pl.BlockSpec((1, tq, NA_WIDTH), lambda b, i: (b, i, 0)),
        out_shape=jax.ShapeDtypeStruct((bsz, seq, NA_WIDTH), BF16),
        compiler_params=_cparams("arbitrary", "arbitrary"),
        name="natten",
    )(na, na, na, bias_table)


AUG = LANES
GATE_LO_SHIFT = N_EXPERTS


def _postmix_kernel(hf_ref, hb_ref, o_ref, hn_ref, x_ref, mod_ref, gm_ref, wm_ref, wn_ref,
                    gpost_ref, gffn_ref, wr_ref, x1_ref, x2_ref, aff_ref):
    d = x_ref.shape[-1]
    s = hf_ref[...].astype(F32) + hb_ref[...].astype(F32)
    og = jax.nn.sigmoid(o_ref[...].astype(F32))
    heads = []
    for hh in range(MLSTM_HEADS):
        sl = slice(hh * MLSTM_HEAD_DIM, (hh + 1) * MLSTM_HEAD_DIM)
        heads.append(_rms(s[:, sl]) * gm_ref[:, sl] * og[:, sl])
    hm = jnp.concatenate(heads, axis=-1).astype(BF16)
    mix = (jnp.dot(hm, wm_ref[...], preferred_element_type=F32)
           + jnp.dot(hn_ref[...], wn_ref[...], preferred_element_type=F32))
    gate1 = mod_ref[0, 2:3, :]
    shift2 = mod_ref[0, 3:4, :]
    scale2 = mod_ref[0, 4:5, :]
    x1 = x_ref[...] + gate1 * (_rms(mix) * gpost_ref[...])
    x1_ref[...] = x1
    h2 = (_rms(x1) * gffn_ref[...]) * (1.0 + scale2) + shift2
    h2b = h2.astype(BF16)
    x2_ref[:, 0:d] = h2b
    logits = jnp.dot(h2b, wr_ref[...], preferred_element_type=F32)
    lane = lax.broadcasted_iota(jnp.int32, logits.shape, 1)
    logits = jnp.where(lane < N_EXPERTS, logits, NEG_BIG)
    ex = jnp.exp(logits - jnp.max(logits, axis=-1, keepdims=True))
    aff = ex / jnp.sum(ex, axis=-1, keepdims=True)
    aff_ref[...] = aff
    hi = aff.astype(BF16).astype(F32)
    lo = (aff - hi).astype(BF16).astype(F32)
    x2_ref[:, d:d + AUG] = (hi + pltpu.roll(lo, GATE_LO_SHIFT, 1)).astype(BF16)


def _postmix(hf, hb, o, hn, x2d, mod, g_mlstm, w_om, w_on, g_post, g_ffn, w_r, seq, tm=256):
    n, d = x2d.shape
    tiles_per_b = seq // tm
    full = lambda a: pl.BlockSpec(a.shape, lambda i: (0, 0))
    tok = lambda w: pl.BlockSpec((tm, w), lambda i: (i, 0))
    return pl.pallas_call(
        _postmix_kernel,
        grid=(n // tm,),
        in_specs=[tok(MLSTM_WIDTH), tok(MLSTM_WIDTH), tok(MLSTM_WIDTH), tok(NA_WIDTH), tok(d),
                  pl.BlockSpec((1, N_MOD, d), lambda i: (i // tiles_per_b, 0, 0)),
                  full(g_mlstm), full(w_om), full(w_on), full(g_post), full(g_ffn), full(w_r)],
        out_specs=[tok(d), tok(d + AUG), tok(LANES)],
        out_shape=[jax.ShapeDtypeStruct((n, d), F32),
                   jax.ShapeDtypeStruct((n, d + AUG), BF16),
                   jax.ShapeDtypeStruct((n, LANES), F32)],
        compiler_params=_cparams("arbitrary"),
        name="postmix",
    )(hf, hb, o, hn, x2d, mod, g_mlstm, w_om, w_on, g_post, g_ffn, w_r)


TOK_TILE = 128
SEL_STEP = 1024
UNSELECTED = -1e6


def _select_kernel(afft_ref, aff_ref, pos_ref, post_ref, tb_ref, thr_ref, need_ref, base_ref, *, cap):
    step = pl.program_id(0)
    ne = afft_ref.shape[0]

    @pl.when(step == 0)
    def _():
        bits = pltpu.bitcast(afft_ref[...], jnp.int32)

        def body(it, lo):
            cand = lo | lax.shift_left(jnp.int32(1), 30 - it)
            cnt = jnp.sum(jnp.where(bits >= cand, 1.0, 0.0), axis=1, keepdims=True)
            return jnp.where(cnt >= cap, cand, lo)

        thr_bits = lax.fori_loop(0, 31, body, jnp.zeros((ne, 1), jnp.int32))
        cnt_gt = jnp.sum(jnp.where(bits > thr_bits, 1.0, 0.0), axis=1, keepdims=True)
        thr_col = pltpu.bitcast(thr_bits, F32)
        need_col = cap - cnt_gt
        sub = lax.broadcasted_iota(jnp.int32, (ne, LANES), 0)
        ln = lax.broadcasted_iota(jnp.int32, (ne, LANES), 1)
        diag = sub == ln
        thr_ref[...] = jnp.sum(jnp.where(diag, thr_col, 0.0), axis=0, keepdims=True)
        need_ref[...] = jnp.sum(jnp.where(diag, need_col, 0.0), axis=0, keepdims=True)
        base_ref[...] = jnp.zeros_like(base_ref)

    thr = thr_ref[...]
    need = need_ref[...]
    lane = lax.broadcasted_iota(jnp.int32, (TOK_TILE, LANES), 1)
    valid = lane < ne
    row = lax.broadcasted_iota(jnp.int32, (TOK_TILE, TOK_TILE), 0)
    col = lax.broadcasted_iota(jnp.int32, (TOK_TILE, TOK_TILE), 1)
    strict_lower = (col < row).astype(BF16)
    base_gt = base_ref[0:1, :]
    base_eq = base_ref[1:2, :]
    tb_rows = []
    for t in range(SEL_STEP // TOK_TILE):
        a = aff_ref[t * TOK_TILE:(t + 1) * TOK_TILE, :]
        gt = (a > thr) & valid
        eq = (a == thr) & valid
        gtf = jnp.where(gt, 1.0, 0.0)
        eqf = jnp.where(eq, 1.0, 0.0)
        cg = jnp.dot(strict_lower, gtf.astype(BF16), preferred_element_type=F32)
        ce = jnp.dot(strict_lower, eqf.astype(BF16), preferred_element_type=F32)
        eq_rank = base_eq + ce
        sel = gt | (eq & (eq_rank < need))
        pos = base_gt + cg + jnp.minimum(eq_rank, need)
        posm = jnp.where(sel, pos, UNSELECTED)
        pos_ref[t * TOK_TILE:(t + 1) * TOK_TILE, :] = posm
        post_ref[:, t * TOK_TILE:(t + 1) * TOK_TILE] = posm.T[0:ne, :]
        tb_rows.append(base_gt + jnp.minimum(base_eq, need))
        base_gt = base_gt + jnp.sum(gtf, axis=0, keepdims=True)
        base_eq = base_eq + jnp.sum(eqf, axis=0, keepdims=True)
    tb_ref[...] = jnp.concatenate(tb_rows, axis=0).astype(jnp.int32)
    base_ref[0:1, :] = base_gt
    base_ref[1:2, :] = base_eq


def _select(aff, cap):
    n = aff.shape[0]
    afft = aff[:, :N_EXPERTS].T
    nsteps = n // SEL_STEP
    tiles_per_step = SEL_STEP // TOK_TILE
    return pl.pallas_call(
        functools.partial(_select_kernel, cap=cap),
        grid=(nsteps,),
        in_specs=[pl.BlockSpec((N_EXPERTS, n), lambda i: (0, 0)),
                  pl.BlockSpec((SEL_STEP, LANES), lambda i: (i, 0))],
        out_specs=[pl.BlockSpec((SEL_STEP, LANES), lambda i: (i, 0)),
                   pl.BlockSpec((N_EXPERTS, SEL_STEP), lambda i: (0, i)),
                   pl.BlockSpec((tiles_per_step, LANES), lambda i: (i, 0))],
        out_shape=[jax.ShapeDtypeStruct((n, LANES), F32),
                   jax.ShapeDtypeStruct((N_EXPERTS, n), F32),
                   jax.ShapeDtypeStruct((n // TOK_TILE, LANES), jnp.int32)],
        scratch_shapes=[pltpu.VMEM((1, LANES), F32), pltpu.VMEM((1, LANES), F32),
                        pltpu.VMEM((SUBLANES, LANES), F32)],
        compiler_params=_cparams("arbitrary"),
        name="select",
    )(afft, aff)


DISP_FAST = 48
DISP_SLOW = SUBLANES + TOK_TILE + SUBLANES
CARRY = SUBLANES
XE_PAD = 2 * TOK_TILE


def _dispatch_kernel(tb_ref, x_ref, post_ref, xe_ref, stage_ref, carry_ref, sem, flag_ref, *, cap, ntiles):
    i = pl.program_id(0)
    slot = lax.rem(i, 2)
    ne = N_EXPERTS

    @pl.when(i == 0)
    def _():
        carry_ref[...] = jnp.zeros_like(carry_ref)
        stage_ref[1, 0:XE_PAD, :] = jnp.zeros((XE_PAD, stage_ref.shape[-1]), F32)
        pads = [pltpu.make_async_copy(stage_ref.at[1, pl.ds(0, XE_PAD)], xe_ref.at[e, pl.ds(cap, XE_PAD)],
                                      sem.at[1]) for e in range(ne)]
        for cp in pads:
            cp.start()
        for cp in pads:
            cp.wait()

    fill = [tb_ref[i * ne + e] for e in range(ne)]
    nxt = [jnp.where(i + 1 < ntiles, tb_ref[jnp.minimum(i + 1, ntiles - 1) * ne + e], cap) for e in range(ne)]
    al = [(f // CARRY) * CARRY for f in fill]
    need_max = functools.reduce(jnp.maximum, [nx - a for nx, a in zip(nxt, al)])
    slow = need_max > DISP_FAST

    def copies(s_rows, sl, als):
        return [pltpu.make_async_copy(stage_ref.at[sl, pl.ds(e * s_rows, s_rows)],
                                      xe_ref.at[e, pl.ds(pl.multiple_of(als[e], CARRY), s_rows)],
                                      sem.at[sl]) for e in range(ne)]

    def wait_all(sl):
        zeros = [0] * ne

        def w(s_rows):
            for cp in copies(s_rows, sl, zeros):
                cp.wait()
        lax.cond(flag_ref[sl] == 1, lambda: w(DISP_SLOW), lambda: w(DISP_FAST))

    def body(s_rows):
        xt = x_ref[...]
        cb = carry_ref[...].astype(BF16)
        sub = lax.broadcasted_iota(jnp.int32, (s_rows, TOK_TILE), 0)
        ln = lax.broadcasted_iota(jnp.int32, (s_rows, TOK_TILE), 1)
        a_tok, a_car = [], []
        for e in range(ne):
            rel = post_ref[e:e + 1, :] - al[e].astype(F32)
            a_tok.append(jnp.where(sub.astype(F32) == rel, 1.0, 0.0).astype(BF16))
            own = (ln >= e * CARRY) & (ln < e * CARRY + (fill[e] - al[e])) & (ln - e * CARRY == sub)
            a_car.append(jnp.where(own, 1.0, 0.0).astype(BF16))
        staged = (jnp.dot(jnp.concatenate(a_tok, axis=0), xt, preferred_element_type=F32)
                  + jnp.dot(jnp.concatenate(a_car, axis=0), cb, preferred_element_type=F32))
        stage_ref[slot, 0:ne * s_rows, :] = staged
        for e in range(ne):
            off = jnp.minimum((nxt[e] // CARRY) * CARRY - al[e], s_rows - CARRY)
            carry_ref[e * CARRY:(e + 1) * CARRY, :] = stage_ref[slot, pl.ds(pl.multiple_of(e * s_rows + off, CARRY), CARRY), :]

        @pl.when(i > 0)
        def _():
            wait_all(1 - slot)
        for cp in copies(s_rows, slot, al):
            cp.start()

    @pl.when(slow)
    def _():
        body(DISP_SLOW)
        flag_ref[slot] = 1

    @pl.when(jnp.logical_not(slow))
    def _():
        body(DISP_FAST)
        flag_ref[slot] = 0

    @pl.when(i == ntiles - 1)
    def _():
        wait_all(slot)


def _dispatch(x2aug, post, tb, cap):
    n, wd = x2aug.shape
    ntiles = n // TOK_TILE
    tb_flat = tb[:, :N_EXPERTS].reshape(-1)
    return pl.pallas_call(
        functools.partial(_dispatch_kernel, cap=cap, ntiles=ntiles),
        grid_spec=pltpu.PrefetchScalarGridSpec(
            num_scalar_prefetch=1,
            grid=(ntiles,),
            in_specs=[pl.BlockSpec((TOK_TILE, wd), lambda i, tb: (i, 0)),
                      pl.BlockSpec((N_EXPERTS, TOK_TILE), lambda i, tb: (0, i))],
            out_specs=pl.BlockSpec(memory_space=pl.ANY),
            scratch_shapes=[pltpu.VMEM((2, N_EXPERTS * DISP_SLOW, wd), F32),
                            pltpu.VMEM((N_EXPERTS * CARRY, wd), F32),
                            pltpu.SemaphoreType.DMA((2,)),
                            pltpu.SMEM((2,), jnp.int32)]),
        out_shape=jax.ShapeDtypeStruct((N_EXPERTS, cap + XE_PAD, wd), F32),
        compiler_params=_cparams("arbitrary"),
        name="dispatch",
    )(tb_flat, x2aug, post)


EXP_ROWS = 512


def _experts_kernel(xe_ref, wg_ref, wu_ref, wd_ref, ye_ref, wgb_ref, wub_ref, wdb_ref):
    e = pl.program_id(0)
    d = wg_ref.shape[1]

    @pl.when(pl.program_id(1) == 0)
    def _():
        wgb_ref[...] = wg_ref[0].astype(BF16)
        wub_ref[...] = wu_ref[0].astype(BF16)
        wdb_ref[...] = wd_ref[0].astype(BF16)

    xa = xe_ref[0]
    x = xa[:, 0:d].astype(BF16)
    aug = xa[:, d:d + AUG]
    lane = lax.broadcasted_iota(jnp.int32, aug.shape, 1)
    mine = (lane == e) | (lane == e + GATE_LO_SHIFT)
    gate = jnp.sum(jnp.where(mine, aug, 0.0), axis=-1, keepdims=True)
    g = jnp.dot(x, wgb_ref[...], preferred_element_type=F32)
    u = jnp.dot(x, wub_ref[...], preferred_element_type=F32)
    hid = (g * jax.nn.sigmoid(g)) * u
    y = jnp.dot(hid.astype(BF16), wdb_ref[...], preferred_element_type=F32)
    ye_ref[0] = (y * gate).astype(ye_ref.dtype)


def _experts(xe, w_gate, w_up, w_down, cap):
    ne, _, wd = xe.shape
    d = w_gate.shape[1]
    f = w_gate.shape[2]
    wspec = lambda s: pl.BlockSpec((1,) + s, lambda e, j: (e, 0, 0))
    rows = min(EXP_ROWS, cap)
    return pl.pallas_call(
        _experts_kernel,
        grid=(ne, cap // rows),
        in_specs=[pl.BlockSpec((1, rows, wd), lambda e, j: (e, j, 0)),
                  wspec((d, f)), wspec((d, f)), wspec((f, d))],
        out_specs=pl.BlockSpec((1, rows, d), lambda e, j: (e, j, 0)),
        out_shape=jax.ShapeDtypeStruct((ne, cap, d), BF16),
        scratch_shapes=[pltpu.VMEM((d, f), BF16), pltpu.VMEM((d, f), BF16), pltpu.VMEM((f, d), BF16)],
        compiler_params=_cparams("arbitrary", "arbitrary"),
        name="experts",
    )(xe, w_gate, w_up, w_down)


COMB_FAST = 64
COMB_SLOW = 256
COMB_ALIGN = BF16_ROWS


def _combine_kernel(tb_ref, pos_ref, x1_ref, mod_ref, g_ref, ye_ref, y_ref, slab_ref, sem, *, cap, ntiles):
    i = pl.program_id(0)
    slot = lax.rem(i, 2)
    ne = N_EXPERTS

    def info(t):
        t = jnp.minimum(t, ntiles - 1)
        fill = [tb_ref[t * ne + e] for e in range(ne)]
        nxt = [jnp.where(t + 1 < ntiles, tb_ref[jnp.minimum(t + 1, ntiles - 1) * ne + e], cap) for e in range(ne)]
        st_f = [jnp.minimum((f // COMB_ALIGN) * COMB_ALIGN, cap - COMB_FAST) for f in fill]
        st_s = [jnp.minimum((f // COMB_ALIGN) * COMB_ALIGN, cap - COMB_SLOW) for f in fill]
        slow = functools.reduce(jnp.maximum, [nx - s for nx, s in zip(nxt, st_f)]) > COMB_FAST
        return st_f, st_s, slow

    def copies(s_rows, sl, starts):
        return [pltpu.make_async_copy(ye_ref.at[e, pl.ds(pl.multiple_of(starts[e], COMB_ALIGN), s_rows)],
                                      slab_ref.at[sl, pl.ds(e * s_rows, s_rows)],
                                      sem.at[sl]) for e in range(ne)]

    def fetch(t, sl):
        st_f, st_s, slow = info(t)

        @pl.when(slow)
        def _():
            for cp in copies(COMB_SLOW, sl, st_s):
                cp.start()

        @pl.when(jnp.logical_not(slow))
        def _():
            for cp in copies(COMB_FAST, sl, st_f):
                cp.start()

    @pl.when(i == 0)
    def _():
        fetch(i, slot)

    @pl.when(i + 1 < ntiles)
    def _():
        fetch(i + 1, 1 - slot)

    st_f, st_s, slow = info(i)
    lane = lax.broadcasted_iota(jnp.int32, (TOK_TILE, LANES), 1).astype(F32)
    pos = pos_ref[...]

    def finish(moe):
        gate2 = mod_ref[0, 5:6, :]
        y_ref[...] = x1_ref[...] + gate2 * (_rms(moe) * g_ref[...])

    @pl.when(jnp.logical_not(slow))
    def _():
        for cp in copies(COMB_FAST, slot, st_f):
            cp.wait()
        per = LANES // COMB_FAST
        groups = []
        for gi in range(ne // per):
            val = None
            for k in range(per):
                e = gi * per + k
                rel = pos[:, e:e + 1] - st_f[e].astype(F32) + float(k * COMB_FAST)
                inband = (lane >= k * COMB_FAST) & (lane < (k + 1) * COMB_FAST)
                hit = inband & (lane == rel)
                val = hit if val is None else (val | hit)
            groups.append(jnp.where(val, 1.0, 0.0).astype(BF16))
        onehot = jnp.concatenate(groups, axis=1)
        finish(jnp.dot(onehot, slab_ref[slot, 0:ne * COMB_FAST, :], preferred_element_type=F32))

    @pl.when(slow)
    def _():
        for cp in copies(COMB_SLOW, slot, st_s):
            cp.wait()
        per = COMB_SLOW // LANES
        groups = []
        for e in range(ne):
            rel = pos[:, e:e + 1] - st_s[e].astype(F32)
            for k in range(per):
                groups.append(jnp.where(lane + float(k * LANES) == rel, 1.0, 0.0).astype(BF16))
        onehot = jnp.concatenate(groups, axis=1)
        finish(jnp.dot(onehot, slab_ref[slot], preferred_element_type=F32))


def _combine(tb, pos, x1, mod, g_post_ffn, ye, cap, seq):
    n, d = x1.shape
    ntiles = n // TOK_TILE
    tiles_per_b = seq // TOK_TILE
    tb_flat = tb[:, :N_EXPERTS].reshape(-1)
    return pl.pallas_call(
        functools.partial(_combine_kernel, cap=cap, ntiles=ntiles),
        grid_spec=pltpu.PrefetchScalarGridSpec(
            num_scalar_prefetch=1,
            grid=(ntiles,),
            in_specs=[pl.BlockSpec((TOK_TILE, LANES), lambda i, tb: (i, 0)),
                      pl.BlockSpec((TOK_TILE, d), lambda i, tb: (i, 0)),
                      pl.BlockSpec((1, N_MOD, d), lambda i, tb: (i // tiles_per_b, 0, 0)),
                      pl.BlockSpec((1, d), lambda i, tb: (0, 0)),
                      pl.BlockSpec(memory_space=pl.ANY)],
            out_specs=pl.BlockSpec((TOK_TILE, d), lambda i, tb: (i, 0)),
            scratch_shapes=[pltpu.VMEM((2, N_EXPERTS * COMB_SLOW, d), BF16),
                            pltpu.SemaphoreType.DMA((2,))]),
        out_shape=jax.ShapeDtypeStruct((n, d), F32),
        compiler_params=_cparams("arbitrary"),
        name="combine",
    )(tb_flat, pos, x1, mod, g_post_ffn, ye)


def _prep_weights(w_in, b_gates, g_mlstm, na_rel_bias, w_out, w_router):
    w = MLSTM_WIDTH
    ng = 4 * MLSTM_HEADS
    w_m = w_in[:, 0:3 * w].astype(BF16)
    w_o = w_in[:, 3 * w:4 * w].astype(BF16)
    perm = [d * 2 * MLSTM_HEADS + g * MLSTM_HEADS + h
            for g in range(2) for d in range(2) for h in range(MLSTM_HEADS)]
    perm = jnp.asarray(perm)
    w_g = jnp.pad(w_in[:, 4 * w:4 * w + ng][:, perm], ((0, 0), (0, LANES - ng))).astype(BF16)
    b_g = jnp.pad(b_gates.reshape(-1)[perm].astype(F32), (0, LANES - ng)).reshape(1, LANES)
    w_n = w_in[:, 4 * w + ng:].astype(BF16)
    w_om = w_out[0:w].astype(BF16)
    w_on = w_out[w:].astype(BF16)
    w_r = jnp.pad(w_router, ((0, 0), (0, LANES - N_EXPERTS))).astype(BF16)
    return dict(w_m=w_m, w_o=w_o, w_g=w_g, b_g=b_g, w_n=w_n, w_om=w_om, w_on=w_on, w_r=w_r,
                g_mlstm=g_mlstm.reshape(1, w).astype(F32), bias_table=_natten_bias_table(na_rel_bias))


def _layer(x, mod, pw, g_pre_mix, g_post_mix, g_pre_ffn, g_post_ffn, w_eg, w_eu, w_ed):
    bsz, seq, d = x.shape
    n = bsz * seq
    x2d = x.reshape(n, d)
    row = lambda g: g.reshape(1, d).astype(F32)
    qkv, o, gates, na = _inproj(x2d, mod, row(g_pre_mix), pw["w_m"], pw["w_o"], pw["w_g"], pw["w_n"], seq)
    hf, hb = _mlstm(qkv.reshape(bsz, seq, -1), gates.reshape(bsz, seq, -1), pw["b_g"])
    hn = _natten(na.reshape(bsz, seq, -1), pw["bias_table"])
    x1, x2aug, aff = _postmix(hf.reshape(n, -1), hb.reshape(n, -1), o, hn.reshape(n, -1), x2d, mod,
                              pw["g_mlstm"], pw["w_om"], pw["w_on"], row(g_post_mix), row(g_pre_ffn),
                              pw["w_r"], seq)
    cap = EC_CAPACITY_FACTOR * n // N_EXPERTS
    pos, post, tb = _select(aff, cap)
    xe = _dispatch(x2aug, post, tb, cap)
    ye = _experts(xe, w_eg, w_eu, w_ed, cap)
    y = _combine(tb, pos, x1, mod, row(g_post_ffn), ye, cap, seq)
    return y.reshape(bsz, seq, d)


def kernel(x_prompt, x_sample, c_prompt, c_sample, w_ada, b_ada, g_pre_mix, g_post_mix, w_in, b_gates, g_mlstm, na_rel_bias, w_out, g_pre_ffn, g_post_ffn, w_router, w_expert_gate, w_expert_up, w_expert_down):
    depth = w_ada.shape[0]
    nb = x_prompt.shape[0]
    xs = [x_prompt, x_sample]
    cs = jnp.concatenate([c_prompt, c_sample], axis=0)
    for l in range(depth):
        mod = _mod(cs, w_ada[l], b_ada[l])
        pw = _prep_weights(w_in[l], b_gates[l], g_mlstm[l], na_rel_bias[l], w_out[l], w_router[l])
        mods = [mod[:nb], mod[nb:]]
        xs = [_layer(x, m, pw, g_pre_mix[l], g_post_mix[l], g_pre_ffn[l], g_post_ffn[l],
                     w_expert_gate[l], w_expert_up[l], w_expert_down[l]) for x, m in zip(xs, mods)]
    return (xs[0], xs[1])
```

```python
import functools
import math

import jax
import jax.numpy as jnp
from jax import lax
from jax.experimental import pallas as pl
from jax.experimental.pallas import tpu as pltpu

F32 = jnp.float32
BF16 = jnp.bfloat16

LANES = 128
SUBLANES = 8
BF16_ROWS = 16
VMEM_LIMIT = 56 * 1024 * 1024

RMS_EPS = 1e-6
N_MOD = 6
MLSTM_HEADS = 4
MLSTM_HEAD_DIM = 128
MLSTM_CHUNK = 128
MLSTM_WIDTH = MLSTM_HEADS * MLSTM_HEAD_DIM
NA_HEAD_DIM = 64
NA_HEADS = 8
NA_WIDTH = NA_HEADS * NA_HEAD_DIM
NA_MAX_KH = 8
NA_KW = 16
GRID_W = 64
N_EXPERTS = 16
EC_CAPACITY_FACTOR = 2
NEG_BIG = -1e30


def _cparams(*sem):
    return pltpu.CompilerParams(dimension_semantics=sem, vmem_limit_bytes=VMEM_LIMIT)


def _rms(x):
    return x * lax.rsqrt(jnp.mean(x * x, axis=-1, keepdims=True) + RMS_EPS)


def _mod_kernel(c_ref, w_ref, b_ref, o_ref):
    c = c_ref[...]
    a = c * jax.nn.sigmoid(c)
    o_ref[...] = jnp.dot(a, w_ref[...], preferred_element_type=F32,
                         precision=lax.Precision.HIGHEST) + b_ref[...]


def _mod(c, w_ada, b_ada):
    bsz, d = c.shape
    n = w_ada.shape[1]
    bn = 1024
    out = pl.pallas_call(
        _mod_kernel,
        grid=(n // bn,),
        in_specs=[pl.BlockSpec((bsz, d), lambda j: (0, 0)),
                  pl.BlockSpec((d, bn), lambda j: (0, j)),
                  pl.BlockSpec((1, bn), lambda j: (0, j))],
        out_specs=pl.BlockSpec((bsz, bn), lambda j: (0, j)),
        out_shape=jax.ShapeDtypeStruct((bsz, n), F32),
        compiler_params=_cparams("arbitrary"),
        name="mod",
    )(c, w_ada, b_ada.reshape(1, n))
    return out.reshape(bsz, N_MOD, d)


def _inproj_kernel(x_ref, mod_ref, g_ref, wqt_ref, wk_ref, wvt_ref, wo_ref, wg_ref, wn_ref,
                   qt_ref, k_ref, vt_ref, o_ref, gates_ref, na_ref):
    x = x_ref[...]
    shift = mod_ref[0, 0:1, :]
    scale = mod_ref[0, 1:2, :]
    h = (_rms(x) * g_ref[...]) * (1.0 + scale) + shift
    hb = h.astype(BF16)
    nt = (((1,), (1,)), ((), ()))
    qt_ref[...] = lax.dot_general(wqt_ref[...], hb, nt, preferred_element_type=F32).astype(BF16)
    vt_ref[...] = lax.dot_general(wvt_ref[...], hb, nt, preferred_element_type=F32).astype(BF16)
    k_ref[...] = jnp.dot(hb, wk_ref[...], preferred_element_type=F32).astype(BF16)
    o_ref[...] = jnp.dot(hb, wo_ref[...], preferred_element_type=F32).astype(BF16)
    gates_ref[...] = jnp.dot(hb, wg_ref[...], preferred_element_type=F32)
    na_ref[...] = jnp.dot(hb, wn_ref[...], preferred_element_type=F32).astype(BF16)


def _inproj(x2d, mod, g_pre, w_qt, w_k, w_vt, w_o, w_g, w_n, seq, tm=512):
    n, d = x2d.shape
    tiles_per_b = seq // tm
    full = lambda a: pl.BlockSpec(a.shape, lambda i: (0, 0))
    tok = lambda w: pl.BlockSpec((tm, w.shape[1]), lambda i: (i, 0))
    feat = lambda w: pl.BlockSpec((w.shape[0], tm), lambda i: (0, i))
    return pl.pallas_call(
        _inproj_kernel,
        grid=(n // tm,),
        in_specs=[pl.BlockSpec((tm, d), lambda i: (i, 0)),
                  pl.BlockSpec((1, N_MOD, d), lambda i: (i // tiles_per_b, 0, 0)),
                  full(g_pre), full(w_qt), full(w_k), full(w_vt), full(w_o), full(w_g), full(w_n)],
        out_specs=[feat(w_qt), tok(w_k), feat(w_vt), tok(w_o), tok(w_g), tok(w_n)],
        out_shape=[jax.ShapeDtypeStruct((w_qt.shape[0], n), BF16),
                   jax.ShapeDtypeStruct((n, w_k.shape[1]), BF16),
                   jax.ShapeDtypeStruct((w_vt.shape[0], n), BF16),
                   jax.ShapeDtypeStruct((n, w_o.shape[1]), BF16),
                   jax.ShapeDtypeStruct((n, w_g.shape[1]), F32),
                   jax.ShapeDtypeStruct((n, w_n.shape[1]), BF16)],
        compiler_params=_cparams("arbitrary"),
        name="inproj",
    )(x2d, mod, g_pre, w_qt, w_k, w_vt, w_o, w_g, w_n)


N_CHAINS = 2 * MLSTM_HEADS
GROW_R, GROW_CM, GROW_B, GROW_G, GROW_RMAX = (i * N_CHAINS for i in range(5))
GROWS = 5 * N_CHAINS
GATEPREP_CHUNKS = 8


def _lane_scan_max(x, lane, reverse):
    sh = 1
    while sh < LANES:
        if reverse:
            x = jnp.maximum(x, jnp.where(lane < LANES - sh, pltpu.roll(x, LANES - sh, 1), NEG_BIG))
        else:
            x = jnp.maximum(x, jnp.where(lane >= sh, pltpu.roll(x, sh, 1), NEG_BIG))
        sh *= 2
    return x


def _gateprep_kernel(g_ref, bias_ref, rows_ref, cols_ref):
    L = MLSTM_CHUNK
    row = lax.broadcasted_iota(jnp.int32, (L, L), 0)
    col = lax.broadcasted_iota(jnp.int32, (L, L), 1)
    tri = jnp.concatenate([jnp.where(row <= col, 1.0, 0.0), jnp.where(row >= col, 1.0, 0.0)], axis=1)
    crow = lax.broadcasted_iota(jnp.int32, (N_CHAINS, L), 0)
    lane = lax.broadcasted_iota(jnp.int32, (N_CHAINS, L), 1)
    is_fwd = crow < MLSTM_HEADS
    for t in range(GATEPREP_CHUNKS):
        gt = (g_ref[t * L:(t + 1) * L, :] + bias_ref[...]).T
        bb = jnp.dot(jax.nn.log_sigmoid(gt[N_CHAINS:2 * N_CHAINS]), tri, preferred_element_type=F32,
                     precision=lax.Precision.HIGHEST)
        b_f, b_b = bb[:, 0:L], bb[:, L:2 * L]
        b = jnp.where(is_fwd, b_f, b_b)
        r = gt[0:N_CHAINS] - b
        g = jnp.where(is_fwd[:, 0:1], b_f[:, L - 1:L], b_b[:, 0:1])
        cm = jnp.where(is_fwd, _lane_scan_max(r, lane, False), _lane_scan_max(r, lane, True))
        rmax = jnp.max(r, axis=1, keepdims=True)
        rows_ref[t] = jnp.concatenate([r, cm, b, jnp.broadcast_to(g, (N_CHAINS, L)),
                                       jnp.broadcast_to(rmax, (N_CHAINS, L))], axis=0)
        cols_ref[t * L:(t + 1) * L, :] = jnp.concatenate([r, jnp.zeros((L - N_CHAINS, L), F32)], axis=0).T


def _gateprep(gates, gate_bias):
    n = gates.shape[0]
    L = MLSTM_CHUNK
    tn = GATEPREP_CHUNKS * L
    return pl.pallas_call(
        _gateprep_kernel,
        grid=(n // tn,),
        in_specs=[pl.BlockSpec((tn, LANES), lambda i: (i, 0)),
                  pl.BlockSpec((1, LANES), lambda i: (0, 0))],
        out_specs=[pl.BlockSpec((GATEPREP_CHUNKS, GROWS, LANES), lambda i: (i, 0, 0)),
                   pl.BlockSpec((tn, LANES), lambda i: (i, 0))],
        out_shape=[jax.ShapeDtypeStruct((n // L, GROWS, LANES), F32),
                   jax.ShapeDtypeStruct((n, LANES), F32)],
        compiler_params=_cparams("arbitrary"),
        name="gateprep",
    )(gates, gate_bias)


def _mlstm_chain(k, qt, vt, rows, cols, mask_sj, c_ref, n_ref, m_ref, ci):
    L = MLSTM_CHUNK
    scale = MLSTM_HEAD_DIM ** -0.5
    r_row = rows[GROW_R + ci:GROW_R + ci + 1]
    cm_row = rows[GROW_CM + ci:GROW_CM + ci + 1]
    b_row = rows[GROW_B + ci:GROW_B + ci + 1]
    g = rows[GROW_G + ci:GROW_G + ci + 1, 0:1]
    rmax = rows[GROW_RMAX + ci:GROW_RMAX + ci + 1, 0:1]
    r_colb = jnp.broadcast_to(cols[:, ci:ci + 1], (L, L))
    c_st = c_ref[ci]
    n_st = n_ref[ci]
    m_st = m_ref[ci][:, 0:1]
    mm = jnp.maximum(cm_row, m_st)
    pt = jnp.where(mask_sj, jnp.exp(r_colb - mm), 0.0)
    st = jnp.dot(k, qt, preferred_element_type=F32) * scale * pt
    s_int = jnp.exp(m_st - mm)
    n_hi = n_st.astype(BF16).astype(F32)
    n_lo = (n_st - n_hi).astype(BF16).astype(F32)
    nn = jnp.concatenate([n_hi, n_lo, jnp.zeros((BF16_ROWS - 2, MLSTM_HEAD_DIM), F32)], axis=0).astype(BF16)
    qn2 = jnp.dot(nn, qt, preferred_element_type=F32)
    den = jnp.sum(st, axis=0, keepdims=True) + s_int * (qn2[0:1] + qn2[1:2])
    inv = 1.0 / jnp.maximum(jnp.abs(den), jnp.exp(-(b_row + mm)))
    lhs = jnp.concatenate([vt, c_st.astype(BF16)], axis=1)
    rhs = jnp.concatenate([st.astype(BF16), (qt.astype(F32) * s_int).astype(BF16)], axis=0)
    h = (jnp.dot(lhs, rhs, preferred_element_type=F32) * inv).astype(BF16).T

    w_row = jnp.exp(r_row - rmax)
    wl = jnp.concatenate([(vt.astype(F32) * w_row).astype(BF16),
                          jnp.broadcast_to(w_row, (BF16_ROWS, L)).astype(BF16)], axis=0)
    loc = jnp.dot(wl, k, preferred_element_type=F32) * scale
    m_loc = g + rmax
    m_new = jnp.maximum(g + m_st, m_loc)
    s_old = jnp.exp(g + m_st - m_new)
    s_loc = jnp.exp(m_loc - m_new)
    c_ref[ci] = s_old * c_st + s_loc * loc[0:MLSTM_HEAD_DIM]
    n_ref[ci] = s_old * n_st + s_loc * loc[MLSTM_HEAD_DIM:MLSTM_HEAD_DIM + 1]
    m_ref[ci] = jnp.broadcast_to(m_new, (1, LANES))
    return h


def _mlstm_kernel(kf_ref, kb_ref, qtf_ref, qtb_ref, vtf_ref, vtb_ref, rf_ref, rb_ref, cf_ref, cb_ref,
                  hf_ref, hb_ref, c_ref, n_ref, m_ref):
    L = MLSTM_CHUNK

    @pl.when(pl.program_id(1) == 0)
    def _():
        c_ref[...] = jnp.zeros_like(c_ref)
        n_ref[...] = jnp.zeros_like(n_ref)
        m_ref[...] = jnp.zeros_like(m_ref)

    row = lax.broadcasted_iota(jnp.int32, (L, L), 0)
    col = lax.broadcasted_iota(jnp.int32, (L, L), 1)
    dirs = ((kf_ref, qtf_ref, vtf_ref, rf_ref, cf_ref, hf_ref, row <= col),
            (kb_ref, qtb_ref, vtb_ref, rb_ref, cb_ref, hb_ref, row >= col))
    for d, (k_ref, qt_ref, vt_ref, r_ref, cl_ref, o_ref, mask) in enumerate(dirs):
        rows = r_ref[0]
        cols = cl_ref[0]
        for hh in range(MLSTM_HEADS):
            sl = slice(hh * MLSTM_HEAD_DIM, (hh + 1) * MLSTM_HEAD_DIM)
            h = _mlstm_chain(k_ref[0, :, sl], qt_ref[sl, :], vt_ref[sl, :], rows, cols, mask,
                             c_ref, n_ref, m_ref, d * MLSTM_HEADS + hh)
            o_ref[0, :, sl] = h


def _mlstm(k, qt, vt, grows, gcols):
    bsz, seq, _ = k.shape
    L = MLSTM_CHUNK
    nc = seq // L
    fwd = lambda b, c: (b, c, 0)
    bwd = lambda b, c: (b, nc - 1 - c, 0)
    fwd_t = lambda b, c: (0, b * nc + c)
    bwd_t = lambda b, c: (0, b * nc + nc - 1 - c)
    fwd_r = lambda b, c: (b * nc + c, 0, 0)
    bwd_r = lambda b, c: (b * nc + nc - 1 - c, 0, 0)
    return pl.pallas_call(
        _mlstm_kernel,
        grid=(bsz, nc),
        in_specs=[pl.BlockSpec((1, L, MLSTM_WIDTH), fwd), pl.BlockSpec((1, L, MLSTM_WIDTH), bwd),
                  pl.BlockSpec((MLSTM_WIDTH, L), fwd_t), pl.BlockSpec((MLSTM_WIDTH, L), bwd_t),
                  pl.BlockSpec((MLSTM_WIDTH, L), fwd_t), pl.BlockSpec((MLSTM_WIDTH, L), bwd_t),
                  pl.BlockSpec((1, GROWS, LANES), fwd_r), pl.BlockSpec((1, GROWS, LANES), bwd_r),
                  pl.BlockSpec((1, L, LANES), fwd), pl.BlockSpec((1, L, LANES), bwd)],
        out_specs=[pl.BlockSpec((1, L, MLSTM_WIDTH), fwd),
                   pl.BlockSpec((1, L, MLSTM_WIDTH), bwd)],
        out_shape=[jax.ShapeDtypeStruct((bsz, seq, MLSTM_WIDTH), BF16),
                   jax.ShapeDtypeStruct((bsz, seq, MLSTM_WIDTH), BF16)],
        scratch_shapes=[pltpu.VMEM((N_CHAINS, MLSTM_HEAD_DIM, MLSTM_HEAD_DIM), F32),
                        pltpu.VMEM((N_CHAINS, 1, MLSTM_HEAD_DIM), F32),
                        pltpu.VMEM((N_CHAINS, 1, LANES), F32)],
        compiler_params=_cparams("arbitrary", "arbitrary"),
        name="mlstm",
    )(k, k, qt, qt, vt, vt, grows, grows, gcols, gcols)


NA_QROWS = 8
NA_KROWS = 2 * NA_QROWS
NA_KBLK = 4
NA_NKBLK = NA_KROWS // NA_KBLK
NA_CODE_SECOND = 2 * NA_MAX_KH - 2
NA_CODE_FIRST = NA_CODE_SECOND + NA_MAX_KH
NA_CODE_NONE = NA_CODE_FIRST + NA_MAX_KH
NA_NCODES = NA_CODE_NONE + 1


def _natten_kernel(q_ref, k0_ref, k1_ref, k2_ref, k3_ref, v0_ref, v1_ref, v2_ref, v3_ref, bias_ref, o_ref, *, rows):
    i = pl.program_id(1)
    half = NA_MAX_KH // 2
    rq0 = i * NA_QROWS
    w0 = jnp.clip(rq0 - half, 0, rows - NA_KROWS)
    nqp = NA_QROWS // 2
    nkp = NA_KROWS // 2
    tq = NA_QROWS * GRID_W

    def code(rq, kp):
        r = rq0 + rq
        r0 = jnp.clip(r - half, 0, rows - NA_MAX_KH)
        kr = w0 + 2 * kp
        d0 = kr - r + (NA_MAX_KH - 1)
        va = (kr >= r0) & (kr < r0 + NA_MAX_KH)
        vb = (kr + 1 >= r0) & (kr + 1 < r0 + NA_MAX_KH)
        return jnp.where(va & vb, d0,
                         jnp.where(vb, NA_CODE_SECOND + d0 + 1,
                                   jnp.where(va, NA_CODE_FIRST + d0 - (NA_MAX_KH - 1), NA_CODE_NONE)))

    codes = [[code(rq, kp) for kp in range(nkp)] for rq in range(NA_QROWS)]
    lane_t = lax.broadcasted_iota(jnp.int32, (LANES, LANES), 1)
    lane_q = lax.broadcasted_iota(jnp.int32, (tq, LANES), 1)
    sub_o = lax.broadcasted_iota(jnp.int32, (LANES, tq), 0)
    scale = jnp.asarray(NA_HEAD_DIM ** -0.5, BF16)
    k_refs = (k0_ref, k1_ref, k2_ref, k3_ref)
    v_refs = (v0_ref, v1_ref, v2_ref, v3_ref)

    for hp in range(NA_HEADS // 2):
        lsl = slice(hp * LANES, (hp + 1) * LANES)
        qp = q_ref[0, :, lsl] * scale
        k_all = jnp.concatenate([r[0, :, lsl] for r in k_refs], axis=0)
        vt = jnp.concatenate([r[0, :, lsl] for r in v_refs], axis=0).T
        outs = []
        for sb in range(2):
            h = 2 * hp + sb
            own = (lane_q >= sb * NA_HEAD_DIM) & (lane_q < (sb + 1) * NA_HEAD_DIM)
            qm = jnp.where(own, qp, jnp.zeros_like(qp))
            st = lax.dot_general(k_all, qm, (((1,), (1,)), ((), ())), preferred_element_type=F32)
            pts, invs = [], []
            for qi in range(nqp):
                tiles = []
                for kp in range(nkp):
                    bias = jnp.where(lane_t < GRID_W, bias_ref[h, codes[2 * qi][kp]], bias_ref[h, codes[2 * qi + 1][kp]])
                    tiles.append(st[kp * LANES:(kp + 1) * LANES, qi * LANES:(qi + 1) * LANES] + bias)
                m = functools.reduce(jnp.maximum, tiles)
                m = jnp.max(m, axis=0, keepdims=True)
                es = [jnp.exp(t - m) for t in tiles]
                l = jnp.sum(functools.reduce(jnp.add, es), axis=0, keepdims=True)
                invs.append(1.0 / l)
                pts.append(jnp.concatenate([e.astype(BF16) for e in es], axis=0))
            pt = jnp.concatenate(pts, axis=1)
            outs.append(jnp.dot(vt, pt, preferred_element_type=F32) * jnp.concatenate(invs, axis=1))
        ot = jnp.where(sub_o < NA_HEAD_DIM, outs[0], outs[1])
        o_ref[0, :, lsl] = ot.T.astype(o_ref.dtype)


def _natten_bias_table(rel_bias):
    cols = jnp.arange(GRID_W)
    col_start = jnp.clip(cols - NA_KW // 2, 0, GRID_W - NA_KW)
    kcol = jnp.arange(GRID_W)
    dc = kcol[:, None] - cols[None, :] + (NA_KW - 1)
    inwin = (kcol[:, None] >= col_start[None, :]) & (kcol[:, None] < col_start[None, :] + NA_KW)
    tab = rel_bias.astype(F32)[:, :, jnp.clip(dc, 0, 2 * NA_KW - 2)]
    tab = jnp.where(inwin[None, None], tab, NEG_BIG)
    neg = jnp.full_like(tab[:, :NA_MAX_KH], NEG_BIG)
    both = jnp.concatenate([tab[:, :-1], tab[:, 1:]], axis=2)
    second = jnp.concatenate([neg, tab[:, :NA_MAX_KH]], axis=2)
    first = jnp.concatenate([tab[:, NA_MAX_KH - 1:], neg], axis=2)
    none = jnp.concatenate([neg[:, :1], neg[:, :1]], axis=2)
    t = jnp.concatenate([both, second, first, none], axis=1)
    return jnp.concatenate([t, t], axis=-1)


def _natten(na, bias_table):
    bsz, seq, _ = na.shape
    rows = seq // GRID_W
    tq = NA_QROWS * GRID_W
    tk = NA_KBLK * GRID_W
    nkb = seq // tk

    def kv_spec(j, col):
        return pl.BlockSpec((1, tk, NA_WIDTH),
                            lambda b, i: (b, jnp.clip(2 * i - 1, 0, nkb - NA_NKBLK) + j, col))

    return pl.pallas_call(
        functools.partial(_natten_kernel, rows=rows),
        grid=(bsz, rows // NA_QROWS),
        in_specs=[pl.BlockSpec((1, tq, NA_WIDTH), lambda b, i: (b, i, 0))]
                 + [kv_spec(j, 1) for j in range(NA_NKBLK)]
                 + [kv_spec(j, 2) for j in range(NA_NKBLK)]
                 + [pl.BlockSpec(bias_table.shape, lambda b, i: (0, 0, 0, 0), pipeline_mode=pl.Buffered(1))],
        out_specs=pl.BlockSpec((1, tq, NA_WIDTH), lambda b, i: (b, i, 0)),
        out_shape=jax.ShapeDtypeStruct((bsz, seq, NA_WIDTH), BF16),
        compiler_params=_cparams("arbitrary", "arbitrary"),
        name="natten",
    )(na, *([na] * (2 * NA_NKBLK)), bias_table)


AUG = LANES
GATE_LO_SHIFT = N_EXPERTS


def _postmix_kernel(hf_ref, hb_ref, o_ref, hn_ref, x_ref, mod_ref, gm_ref, wm_ref, wn_ref,
                    gpost_ref, gffn_ref, wr_ref, x1_ref, x2_ref, aff_ref):
    d = x_ref.shape[-1]
    s = hf_ref[...].astype(F32) + hb_ref[...].astype(F32)
    og = jax.nn.sigmoid(o_ref[...].astype(F32))
    heads = []
    for hh in range(MLSTM_HEADS):
        sl = slice(hh * MLSTM_HEAD_DIM, (hh + 1) * MLSTM_HEAD_DIM)
        heads.append(_rms(s[:, sl]) * gm_ref[:, sl] * og[:, sl])
    hm = jnp.concatenate(heads, axis=-1).astype(BF16)
    mix = (jnp.dot(hm, wm_ref[...], preferred_element_type=F32)
           + jnp.dot(hn_ref[...], wn_ref[...], preferred_element_type=F32))
    gate1 = mod_ref[0, 2:3, :]
    shift2 = mod_ref[0, 3:4, :]
    scale2 = mod_ref[0, 4:5, :]
    x1 = x_ref[...] + gate1 * (_rms(mix) * gpost_ref[...])
    x1_ref[...] = x1
    h2 = (_rms(x1) * gffn_ref[...]) * (1.0 + scale2) + shift2
    h2b = h2.astype(BF16)
    x2_ref[:, 0:d] = h2b
    logits = jnp.dot(h2b, wr_ref[...], preferred_element_type=F32)
    lane = lax.broadcasted_iota(jnp.int32, logits.shape, 1)
    logits = jnp.where(lane < N_EXPERTS, logits, NEG_BIG)
    ex = jnp.exp(logits - jnp.max(logits, axis=-1, keepdims=True))
    aff = ex / jnp.sum(ex, axis=-1, keepdims=True)
    aff_ref[...] = aff
    hi = aff.astype(BF16).astype(F32)
    lo = (aff - hi).astype(BF16).astype(F32)
    x2_ref[:, d:d + AUG] = (hi + pltpu.roll(lo, GATE_LO_SHIFT, 1)).astype(BF16)


def _postmix(hf, hb, o, hn, x2d, mod, g_mlstm, w_om, w_on, g_post, g_ffn, w_r, seq, tm=256):
    n, d = x2d.shape
    tiles_per_b = seq // tm
    full = lambda a: pl.BlockSpec(a.shape, lambda i: (0, 0))
    tok = lambda w: pl.BlockSpec((tm, w), lambda i: (i, 0))
    return pl.pallas_call(
        _postmix_kernel,
        grid=(n // tm,),
        in_specs=[tok(MLSTM_WIDTH), tok(MLSTM_WIDTH), tok(MLSTM_WIDTH), tok(NA_WIDTH), tok(d),
                  pl.BlockSpec((1, N_MOD, d), lambda i: (i // tiles_per_b, 0, 0)),
                  full(g_mlstm), full(w_om), full(w_on), full(g_post), full(g_ffn), full(w_r)],
        out_specs=[tok(d), tok(d + AUG), tok(LANES)],
        out_shape=[jax.ShapeDtypeStruct((n, d), F32),
                   jax.ShapeDtypeStruct((n, d + AUG), BF16),
                   jax.ShapeDtypeStruct((n, LANES), F32)],
        compiler_params=_cparams("arbitrary"),
        name="postmix",
    )(hf, hb, o, hn, x2d, mod, g_mlstm, w_om, w_on, g_post, g_ffn, w_r)


TOK_TILE = 128
SEL_STEP = 1024
UNSELECTED = -1e6


def _select_kernel(afft_ref, aff_ref, pos_ref, post_ref, tb_ref, thr_ref, need_ref, base_ref, *, cap):
    step = pl.program_id(0)
    ne = afft_ref.shape[0]

    @pl.when(step == 0)
    def _():
        bits = pltpu.bitcast(afft_ref[...], jnp.int32)

        def body(it, lo):
            cand = lo | lax.shift_left(jnp.int32(1), 30 - it)
            cnt = jnp.sum(jnp.where(bits >= cand, 1.0, 0.0), axis=1, keepdims=True)
            return jnp.where(cnt >= cap, cand, lo)

        thr_bits = lax.fori_loop(0, 31, body, jnp.zeros((ne, 1), jnp.int32))
        cnt_gt = jnp.sum(jnp.where(bits > thr_bits, 1.0, 0.0), axis=1, keepdims=True)
        thr_col = pltpu.bitcast(thr_bits, F32)
        need_col = cap - cnt_gt
        sub = lax.broadcasted_iota(jnp.int32, (ne, LANES), 0)
        ln = lax.broadcasted_iota(jnp.int32, (ne, LANES), 1)
        diag = sub == ln
        thr_ref[...] = jnp.sum(jnp.where(diag, thr_col, 0.0), axis=0, keepdims=True)
        need_ref[...] = jnp.sum(jnp.where(diag, need_col, 0.0), axis=0, keepdims=True)
        base_ref[...] = jnp.zeros_like(base_ref)

    thr = thr_ref[...]
    need = need_ref[...]
    lane = lax.broadcasted_iota(jnp.int32, (TOK_TILE, LANES), 1)
    valid = lane < ne
    row = lax.broadcasted_iota(jnp.int32, (TOK_TILE, TOK_TILE), 0)
    col = lax.broadcasted_iota(jnp.int32, (TOK_TILE, TOK_TILE), 1)
    strict_lower = (col < row).astype(BF16)
    base_gt = base_ref[0:1, :]
    base_eq = base_ref[1:2, :]
    tb_rows = []
    for t in range(SEL_STEP // TOK_TILE):
        a = aff_ref[t * TOK_TILE:(t + 1) * TOK_TILE, :]
        gt = (a > thr) & valid
        eq = (a == thr) & valid
        gtf = jnp.where(gt, 1.0, 0.0)
        eqf = jnp.where(eq, 1.0, 0.0)
        cg = jnp.dot(strict_lower, gtf.astype(BF16), preferred_element_type=F32)
        ce = jnp.dot(strict_lower, eqf.astype(BF16), preferred_element_type=F32)
        eq_rank = base_eq + ce
        sel = gt | (eq & (eq_rank < need))
        pos = base_gt + cg + jnp.minimum(eq_rank, need)
        posm = jnp.where(sel, pos, UNSELECTED)
        pos_ref[t * TOK_TILE:(t + 1) * TOK_TILE, :] = posm
        post_ref[:, t * TOK_TILE:(t + 1) * TOK_TILE] = posm.T[0:ne, :]
        tb_rows.append(base_gt + jnp.minimum(base_eq, need))
        base_gt = base_gt + jnp.sum(gtf, axis=0, keepdims=True)
        base_eq = base_eq + jnp.sum(eqf, axis=0, keepdims=True)
    tb_ref[...] = jnp.concatenate(tb_rows, axis=0).astype(jnp.int32)
    base_ref[0:1, :] = base_gt
    base_ref[1:2, :] = base_eq


def _select(aff, cap):
    n = aff.shape[0]
    afft = aff[:, :N_EXPERTS].T
    nsteps = n // SEL_STEP
    tiles_per_step = SEL_STEP // TOK_TILE
    return pl.pallas_call(
        functools.partial(_select_kernel, cap=cap),
        grid=(nsteps,),
        in_specs=[pl.BlockSpec((N_EXPERTS, n), lambda i: (0, 0)),
                  pl.BlockSpec((SEL_STEP, LANES), lambda i: (i, 0))],
        out_specs=[pl.BlockSpec((SEL_STEP, LANES), lambda i: (i, 0)),
                   pl.BlockSpec((N_EXPERTS, SEL_STEP), lambda i: (0, i)),
                   pl.BlockSpec((tiles_per_step, LANES), lambda i: (i, 0))],
        out_shape=[jax.ShapeDtypeStruct((n, LANES), F32),
                   jax.ShapeDtypeStruct((N_EXPERTS, n), F32),
                   jax.ShapeDtypeStruct((n // TOK_TILE, LANES), jnp.int32)],
        scratch_shapes=[pltpu.VMEM((1, LANES), F32), pltpu.VMEM((1, LANES), F32),
                        pltpu.VMEM((SUBLANES, LANES), F32)],
        compiler_params=_cparams("arbitrary"),
        name="select",
    )(afft, aff)


DISP_FAST = 48
DISP_SLOW = SUBLANES + TOK_TILE + SUBLANES
CARRY = SUBLANES
XE_PAD = 2 * TOK_TILE


def _dispatch_kernel(tb_ref, x_ref, post_ref, xe_ref, stage_ref, carry_ref, sem, flag_ref, *, cap, ntiles):
    i = pl.program_id(0)
    slot = lax.rem(i, 2)
    ne = N_EXPERTS

    @pl.when(i == 0)
    def _():
        carry_ref[...] = jnp.zeros_like(carry_ref)
        stage_ref[1, 0:XE_PAD, :] = jnp.zeros((XE_PAD, stage_ref.shape[-1]), F32)
        pads = [pltpu.make_async_copy(stage_ref.at[1, pl.ds(0, XE_PAD)], xe_ref.at[e, pl.ds(cap, XE_PAD)],
                                      sem.at[1]) for e in range(ne)]
        for cp in pads:
            cp.start()
        for cp in pads:
            cp.wait()

    fill = [tb_ref[i * ne + e] for e in range(ne)]
    nxt = [jnp.where(i + 1 < ntiles, tb_ref[jnp.minimum(i + 1, ntiles - 1) * ne + e], cap) for e in range(ne)]
    al = [(f // CARRY) * CARRY for f in fill]
    need_max = functools.reduce(jnp.maximum, [nx - a for nx, a in zip(nxt, al)])
    slow = need_max > DISP_FAST

    def copies(s_rows, sl, als):
        return [pltpu.make_async_copy(stage_ref.at[sl, pl.ds(e * s_rows, s_rows)],
                                      xe_ref.at[e, pl.ds(pl.multiple_of(als[e], CARRY), s_rows)],
                                      sem.at[sl]) for e in range(ne)]

    def wait_all(sl):
        zeros = [0] * ne

        def w(s_rows):
            for cp in copies(s_rows, sl, zeros):
                cp.wait()
        lax.cond(flag_ref[sl] == 1, lambda: w(DISP_SLOW), lambda: w(DISP_FAST))

    def body(s_rows):
        xt = x_ref[...]
        cb = carry_ref[...].astype(BF16)
        sub = lax.broadcasted_iota(jnp.int32, (s_rows, TOK_TILE), 0)
        ln = lax.broadcasted_iota(jnp.int32, (s_rows, TOK_TILE), 1)
        a_tok, a_car = [], []
        for e in range(ne):
            rel = post_ref[e:e + 1, :] - al[e].astype(F32)
            a_tok.append(jnp.where(sub.astype(F32) == rel, 1.0, 0.0).astype(BF16))
            own = (ln >= e * CARRY) & (ln < e * CARRY + (fill[e] - al[e])) & (ln - e * CARRY == sub)
            a_car.append(jnp.where(own, 1.0, 0.0).astype(BF16))
        staged = (jnp.dot(jnp.concatenate(a_tok, axis=0), xt, preferred_element_type=F32)
                  + jnp.dot(jnp.concatenate(a_car, axis=0), cb, preferred_element_type=F32))
        stage_ref[slot, 0:ne * s_rows, :] = staged
        for e in range(ne):
            off = jnp.minimum((nxt[e] // CARRY) * CARRY - al[e], s_rows - CARRY)
            carry_ref[e * CARRY:(e + 1) * CARRY, :] = stage_ref[slot, pl.ds(pl.multiple_of(e * s_rows + off, CARRY), CARRY), :]

        @pl.when(i > 0)
        def _():
            wait_all(1 - slot)
        for cp in copies(s_rows, slot, al):
            cp.start()

    @pl.when(slow)
    def _():
        body(DISP_SLOW)
        flag_ref[slot] = 1

    @pl.when(jnp.logical_not(slow))
    def _():
        body(DISP_FAST)
        flag_ref[slot] = 0

    @pl.when(i == ntiles - 1)
    def _():
        wait_all(slot)


def _dispatch(x2aug, post, tb, cap):
    n, wd = x2aug.shape
    ntiles = n // TOK_TILE
    tb_flat = tb[:, :N_EXPERTS].reshape(-1)
    return pl.pallas_call(
        functools.partial(_dispatch_kernel, cap=cap, ntiles=ntiles),
        grid_spec=pltpu.PrefetchScalarGridSpec(
            num_scalar_prefetch=1,
            grid=(ntiles,),
            in_specs=[pl.BlockSpec((TOK_TILE, wd), lambda i, tb: (i, 0)),
                      pl.BlockSpec((N_EXPERTS, TOK_TILE), lambda i, tb: (0, i))],
            out_specs=pl.BlockSpec(memory_space=pl.ANY),
            scratch_shapes=[pltpu.VMEM((2, N_EXPERTS * DISP_SLOW, wd), F32),
                            pltpu.VMEM((N_EXPERTS * CARRY, wd), F32),
                            pltpu.SemaphoreType.DMA((2,)),
                            pltpu.SMEM((2,), jnp.int32)]),
        out_shape=jax.ShapeDtypeStruct((N_EXPERTS, cap + XE_PAD, wd), F32),
        compiler_params=_cparams("arbitrary"),
        name="dispatch",
    )(tb_flat, x2aug, post)


EXP_ROWS = 512


def _experts_kernel(xe_ref, wg_ref, wu_ref, wd_ref, ye_ref, wgb_ref, wub_ref, wdb_ref):
    e = pl.program_id(0)
    d = wg_ref.shape[1]

    @pl.when(pl.program_id(1) == 0)
    def _():
        wgb_ref[...] = wg_ref[0].astype(BF16)
        wub_ref[...] = wu_ref[0].astype(BF16)
        wdb_ref[...] = wd_ref[0].astype(BF16)

    xa = xe_ref[0]
    x = xa[:, 0:d].astype(BF16)
    aug = xa[:, d:d + AUG]
    lane = lax.broadcasted_iota(jnp.int32, aug.shape, 1)
    mine = (lane == e) | (lane == e + GATE_LO_SHIFT)
    gate = jnp.sum(jnp.where(mine, aug, 0.0), axis=-1, keepdims=True)
    g = jnp.dot(x, wgb_ref[...], preferred_element_type=F32)
    u = jnp.dot(x, wub_ref[...], preferred_element_type=F32)
    hid = (g * jax.nn.sigmoid(g)) * u
    y = jnp.dot(hid.astype(BF16), wdb_ref[...], preferred_element_type=F32)
    ye_ref[0] = (y * gate).astype(ye_ref.dtype)


def _experts(xe, w_gate, w_up, w_down, cap):
    ne, _, wd = xe.shape
    d = w_gate.shape[1]
    f = w_gate.shape[2]
    wspec = lambda s: pl.BlockSpec((1,) + s, lambda e, j: (e, 0, 0))
    rows = min(EXP_ROWS, cap)
    return pl.pallas_call(
        _experts_kernel,
        grid=(ne, cap // rows),
        in_specs=[pl.BlockSpec((1, rows, wd), lambda e, j: (e, j, 0)),
                  wspec((d, f)), wspec((d, f)), wspec((f, d))],
        out_specs=pl.BlockSpec((1, rows, d), lambda e, j: (e, j, 0)),
        out_shape=jax.ShapeDtypeStruct((ne, cap, d), BF16),
        scratch_shapes=[pltpu.VMEM((d, f), BF16), pltpu.VMEM((d, f), BF16), pltpu.VMEM((f, d), BF16)],
        compiler_params=_cparams("arbitrary", "arbitrary"),
        name="experts",
    )(xe, w_gate, w_up, w_down)


COMB_FAST = 64
COMB_SLOW = 256
COMB_ALIGN = BF16_ROWS


def _combine_kernel(tb_ref, pos_ref, x1_ref, mod_ref, g_ref, ye_ref, y_ref, slab_ref, sem, *, cap, ntiles):
    i = pl.program_id(0)
    slot = lax.rem(i, 2)
    ne = N_EXPERTS

    def info(t):
        t = jnp.minimum(t, ntiles - 1)
        fill = [tb_ref[t * ne + e] for e in range(ne)]
        nxt = [jnp.where(t + 1 < ntiles, tb_ref[jnp.minimum(t + 1, ntiles - 1) * ne + e], cap) for e in range(ne)]
        st_f = [jnp.minimum((f // COMB_ALIGN) * COMB_ALIGN, cap - COMB_FAST) for f in fill]
        st_s = [jnp.minimum((f // COMB_ALIGN) * COMB_ALIGN, cap - COMB_SLOW) for f in fill]
        slow = functools.reduce(jnp.maximum, [nx - s for nx, s in zip(nxt, st_f)]) > COMB_FAST
        return st_f, st_s, slow

    def copies(s_rows, sl, starts):
        return [pltpu.make_async_copy(ye_ref.at[e, pl.ds(pl.multiple_of(starts[e], COMB_ALIGN), s_rows)],
                                      slab_ref.at[sl, pl.ds(e * s_rows, s_rows)],
                                      sem.at[sl]) for e in range(ne)]

    def fetch(t, sl):
        st_f, st_s, slow = info(t)

        @pl.when(slow)
        def _():
            for cp in copies(COMB_SLOW, sl, st_s):
                cp.start()

        @pl.when(jnp.logical_not(slow))
        def _():
            for cp in copies(COMB_FAST, sl, st_f):
                cp.start()

    @pl.when(i == 0)
    def _():
        fetch(i, slot)

    @pl.when(i + 1 < ntiles)
    def _():
        fetch(i + 1, 1 - slot)

    st_f, st_s, slow = info(i)
    lane = lax.broadcasted_iota(jnp.int32, (TOK_TILE, LANES), 1).astype(F32)
    pos = pos_ref[...]

    def finish(moe):
        gate2 = mod_ref[0, 5:6, :]
        y_ref[...] = x1_ref[...] + gate2 * (_rms(moe) * g_ref[...])

    @pl.when(jnp.logical_not(slow))
    def _():
        for cp in copies(COMB_FAST, slot, st_f):
            cp.wait()
        per = LANES // COMB_FAST
        groups = []
        for gi in range(ne // per):
            val = None
            for k in range(per):
                e = gi * per + k
                rel = pos[:, e:e + 1] - st_f[e].astype(F32) + float(k * COMB_FAST)
                inband = (lane >= k * COMB_FAST) & (lane < (k + 1) * COMB_FAST)
                hit = inband & (lane == rel)
                val = hit if val is None else (val | hit)
            groups.append(jnp.where(val, 1.0, 0.0).astype(BF16))
        onehot = jnp.concatenate(groups, axis=1)
        finish(jnp.dot(onehot, slab_ref[slot, 0:ne * COMB_FAST, :], preferred_element_type=F32))

    @pl.when(slow)
    def _():
        for cp in copies(COMB_SLOW, slot, st_s):
            cp.wait()
        per = COMB_SLOW // LANES
        groups = []
        for e in range(ne):
            rel = pos[:, e:e + 1] - st_s[e].astype(F32)
            for k in range(per):
                groups.append(jnp.where(lane + float(k * LANES) == rel, 1.0, 0.0).astype(BF16))
        onehot = jnp.concatenate(groups, axis=1)
        finish(jnp.dot(onehot, slab_ref[slot], preferred_element_type=F32))


def _combine(tb, pos, x1, mod, g_post_ffn, ye, cap, seq):
    n, d = x1.shape
    ntiles = n // TOK_TILE
    tiles_per_b = seq // TOK_TILE
    tb_flat = tb[:, :N_EXPERTS].reshape(-1)
    return pl.pallas_call(
        functools.partial(_combine_kernel, cap=cap, ntiles=ntiles),
        grid_spec=pltpu.PrefetchScalarGridSpec(
            num_scalar_prefetch=1,
            grid=(ntiles,),
            in_specs=[pl.BlockSpec((TOK_TILE, LANES), lambda i, tb: (i, 0)),
                      pl.BlockSpec((TOK_TILE, d), lambda i, tb: (i, 0)),
                      pl.BlockSpec((1, N_MOD, d), lambda i, tb: (i // tiles_per_b, 0, 0)),
                      pl.BlockSpec((1, d), lambda i, tb: (0, 0)),
                      pl.BlockSpec(memory_space=pl.ANY)],
            out_specs=pl.BlockSpec((TOK_TILE, d), lambda i, tb: (i, 0)),
            scratch_shapes=[pltpu.VMEM((2, N_EXPERTS * COMB_SLOW, d), BF16),
                            pltpu.SemaphoreType.DMA((2,))]),
        out_shape=jax.ShapeDtypeStruct((n, d), F32),
        compiler_params=_cparams("arbitrary"),
        name="combine",
    )(tb_flat, pos, x1, mod, g_post_ffn, ye)


def _prep_weights(w_in, b_gates, g_mlstm, na_rel_bias, w_out, w_router):
    w = MLSTM_WIDTH
    ng = 4 * MLSTM_HEADS
    w_qt = w_in[:, 0:w].T.astype(BF16)
    w_k = w_in[:, w:2 * w].astype(BF16)
    w_vt = w_in[:, 2 * w:3 * w].T.astype(BF16)
    w_o = w_in[:, 3 * w:4 * w].astype(BF16)
    perm = [d * 2 * MLSTM_HEADS + g * MLSTM_HEADS + h
            for g in range(2) for d in range(2) for h in range(MLSTM_HEADS)]
    perm = jnp.asarray(perm)
    w_g = jnp.pad(w_in[:, 4 * w:4 * w + ng][:, perm], ((0, 0), (0, LANES - ng))).astype(BF16)
    b_g = jnp.pad(b_gates.reshape(-1)[perm].astype(F32), (0, LANES - ng)).reshape(1, LANES)
    w_n = w_in[:, 4 * w + ng:].astype(BF16)
    w_om = w_out[0:w].astype(BF16)
    w_on = w_out[w:].astype(BF16)
    w_r = jnp.pad(w_router, ((0, 0), (0, LANES - N_EXPERTS))).astype(BF16)
    return dict(w_qt=w_qt, w_k=w_k, w_vt=w_vt, w_o=w_o, w_g=w_g, b_g=b_g, w_n=w_n, w_om=w_om, w_on=w_on, w_r=w_r,
                g_mlstm=g_mlstm.reshape(1, w).astype(F32), bias_table=_natten_bias_table(na_rel_bias))


def _layer(x, mod, pw, g_pre_mix, g_post_mix, g_pre_ffn, g_post_ffn, w_eg, w_eu, w_ed):
    bsz, seq, d = x.shape
    n = bsz * seq
    x2d = x.reshape(n, d)
    row = lambda g: g.reshape(1, d).astype(F32)
    qt, k, vt, o, gates, na = _inproj(x2d, mod, row(g_pre_mix), pw["w_qt"], pw["w_k"], pw["w_vt"], pw["w_o"],
                                      pw["w_g"], pw["w_n"], seq)
    grows, gcols = _gateprep(gates, pw["b_g"])
    hf, hb = _mlstm(k.reshape(bsz, seq, -1), qt, vt, grows, gcols.reshape(bsz, seq, -1))
    hn = _natten(na.reshape(bsz, seq, -1), pw["bias_table"])
    x1, x2aug, aff = _postmix(hf.reshape(n, -1), hb.reshape(n, -1), o, hn.reshape(n, -1), x2d, mod,
                              pw["g_mlstm"], pw["w_om"], pw["w_on"], row(g_post_mix), row(g_pre_ffn),
                              pw["w_r"], seq)
    cap = EC_CAPACITY_FACTOR * n // N_EXPERTS
    pos, post, tb = _select(aff, cap)
    xe = _dispatch(x2aug, post, tb, cap)
    ye = _experts(xe, w_eg, w_eu, w_ed, cap)
    y = _combine(tb, pos, x1, mod, row(g_post_ffn), ye, cap, seq)
    return y.reshape(bsz, seq, d)


def kernel(x_prompt, x_sample, c_prompt, c_sample, w_ada, b_ada, g_pre_mix, g_post_mix, w_in, b_gates, g_mlstm, na_rel_bias, w_out, g_pre_ffn, g_post_ffn, w_router, w_expert_gate, w_expert_up, w_expert_down):
    depth = w_ada.shape[0]
    nb = x_prompt.shape[0]
    xs = [x_prompt, x_sample]
    cs = jnp.concatenate([c_prompt, c_sample], axis=0)
    for l in range(depth):
        mod = _mod(cs, w_ada[l], b_ada[l])
        pw = _prep_weights(w_in[l], b_gates[l], g_mlstm[l], na_rel_bias[l], w_out[l], w_router[l])
        mods = [mod[:nb], mod[nb:]]
        xs = [_layer(x, m, pw, g_pre_mix[l], g_post_mix[l], g_pre_ffn[l], g_post_ffn[l],
                     w_expert_gate[l], w_expert_up[l], w_expert_down[l]) for x, m in zip(xs, mods)]
    return (xs[0], xs[1])
```

```python
import functools
import math

import jax
import jax.numpy as jnp
from jax import lax
from jax.experimental import pallas as pl
from jax.experimental.pallas import tpu as pltpu

F32 = jnp.float32
BF16 = jnp.bfloat16

LANES = 128
SUBLANES = 8
BF16_ROWS = 16
VMEM_LIMIT = 56 * 1024 * 1024

RMS_EPS = 1e-6
N_MOD = 6
MLSTM_HEADS = 4
MLSTM_HEAD_DIM = 128
MLSTM_CHUNK = 128
MLSTM_WIDTH = MLSTM_HEADS * MLSTM_HEAD_DIM
NA_HEAD_DIM = 64
NA_HEADS = 8
NA_WIDTH = NA_HEADS * NA_HEAD_DIM
NA_MAX_KH = 8
NA_KW = 16
GRID_W = 64
N_EXPERTS = 16
EC_CAPACITY_FACTOR = 2
NEG_BIG = -1e30


def _cparams(*sem):
    return pltpu.CompilerParams(dimension_semantics=sem, vmem_limit_bytes=VMEM_LIMIT)


def _rms(x):
    return x * lax.rsqrt(jnp.mean(x * x, axis=-1, keepdims=True) + RMS_EPS)


def _mod_kernel(c_ref, w_ref, b_ref, o_ref):
    c = c_ref[...]
    a = c * jax.nn.sigmoid(c)
    o_ref[...] = jnp.dot(a, w_ref[...], preferred_element_type=F32,
                         precision=lax.Precision.HIGHEST) + b_ref[...]


def _mod(c, w_ada, b_ada):
    bsz, d = c.shape
    n = w_ada.shape[1]
    bn = 1024
    out = pl.pallas_call(
        _mod_kernel,
        grid=(n // bn,),
        in_specs=[pl.BlockSpec((bsz, d), lambda j: (0, 0)),
                  pl.BlockSpec((d, bn), lambda j: (0, j)),
                  pl.BlockSpec((1, bn), lambda j: (0, j))],
        out_specs=pl.BlockSpec((bsz, bn), lambda j: (0, j)),
        out_shape=jax.ShapeDtypeStruct((bsz, n), F32),
        compiler_params=_cparams("arbitrary"),
        name="mod",
    )(c, w_ada, b_ada.reshape(1, n))
    return out.reshape(bsz, N_MOD, d)


def _inproj_kernel(x_ref, mod_ref, g_ref, wqt_ref, wk_ref, wvt_ref, wo_ref, wg_ref, wn_ref,
                   qt_ref, k_ref, vt_ref, o_ref, gates_ref, na_ref):
    x = x_ref[...]
    shift = mod_ref[0, 0:1, :]
    scale = mod_ref[0, 1:2, :]
    h = (_rms(x) * g_ref[...]) * (1.0 + scale) + shift
    hb = h.astype(BF16)
    nt = (((1,), (1,)), ((), ()))
    qt_ref[...] = lax.dot_general(wqt_ref[...], hb, nt, preferred_element_type=F32).astype(BF16)
    vt_ref[...] = lax.dot_general(wvt_ref[...], hb, nt, preferred_element_type=F32).astype(BF16)
    k_ref[...] = jnp.dot(hb, wk_ref[...], preferred_element_type=F32).astype(BF16)
    o_ref[...] = jnp.dot(hb, wo_ref[...], preferred_element_type=F32).astype(BF16)
    gates_ref[...] = jnp.dot(hb, wg_ref[...], preferred_element_type=F32)
    na_ref[...] = jnp.dot(hb, wn_ref[...], preferred_element_type=F32).astype(BF16)


def _inproj(x2d, mod, g_pre, w_qt, w_k, w_vt, w_o, w_g, w_n, seq, tm=512):
    n, d = x2d.shape
    tiles_per_b = seq // tm
    full = lambda a: pl.BlockSpec(a.shape, lambda i: (0, 0))
    tok = lambda w: pl.BlockSpec((tm, w.shape[1]), lambda i: (i, 0))
    feat = lambda w: pl.BlockSpec((w.shape[0], tm), lambda i: (0, i))
    return pl.pallas_call(
        _inproj_kernel,
        grid=(n // tm,),
        in_specs=[pl.BlockSpec((tm, d), lambda i: (i, 0)),
                  pl.BlockSpec((1, N_MOD, d), lambda i: (i // tiles_per_b, 0, 0)),
                  full(g_pre), full(w_qt), full(w_k), full(w_vt), full(w_o), full(w_g), full(w_n)],
        out_specs=[feat(w_qt), tok(w_k), feat(w_vt), tok(w_o), tok(w_g), tok(w_n)],
        out_shape=[jax.ShapeDtypeStruct((w_qt.shape[0], n), BF16),
                   jax.ShapeDtypeStruct((n, w_k.shape[1]), BF16),
                   jax.ShapeDtypeStruct((w_vt.shape[0], n), BF16),
                   jax.ShapeDtypeStruct((n, w_o.shape[1]), BF16),
                   jax.ShapeDtypeStruct((n, w_g.shape[1]), F32),
                   jax.ShapeDtypeStruct((n, w_n.shape[1]), BF16)],
        compiler_params=_cparams("arbitrary"),
        name="inproj",
    )(x2d, mod, g_pre, w_qt, w_k, w_vt, w_o, w_g, w_n)


N_CHAINS = 2 * MLSTM_HEADS
GROW_R, GROW_CM, GROW_B, GROW_G, GROW_RMAX = (i * N_CHAINS for i in range(5))
GROWS = 5 * N_CHAINS
GATEPREP_CHUNKS = 8
MLSTM_LOOKAHEAD = 4


def _lane_scan_max(x, lane, reverse):
    sh = 1
    while sh < LANES:
        if reverse:
            x = jnp.maximum(x, jnp.where(lane < LANES - sh, pltpu.roll(x, LANES - sh, 1), NEG_BIG))
        else:
            x = jnp.maximum(x, jnp.where(lane >= sh, pltpu.roll(x, sh, 1), NEG_BIG))
        sh *= 2
    return x


def _gateprep_kernel(g_ref, bias_ref, rows_ref, cols_ref):
    L = MLSTM_CHUNK
    row = lax.broadcasted_iota(jnp.int32, (L, L), 0)
    col = lax.broadcasted_iota(jnp.int32, (L, L), 1)
    tri = jnp.concatenate([jnp.where(row <= col, 1.0, 0.0), jnp.where(row >= col, 1.0, 0.0)], axis=1)
    crow = lax.broadcasted_iota(jnp.int32, (N_CHAINS, L), 0)
    lane = lax.broadcasted_iota(jnp.int32, (N_CHAINS, L), 1)
    is_fwd = crow < MLSTM_HEADS
    for t in range(GATEPREP_CHUNKS):
        gt = (g_ref[t * L:(t + 1) * L, :] + bias_ref[...]).T
        bb = jnp.dot(jax.nn.log_sigmoid(gt[N_CHAINS:2 * N_CHAINS]), tri, preferred_element_type=F32,
                     precision=lax.Precision.HIGHEST)
        b_f, b_b = bb[:, 0:L], bb[:, L:2 * L]
        b = jnp.where(is_fwd, b_f, b_b)
        r = gt[0:N_CHAINS] - b
        g = jnp.where(is_fwd[:, 0:1], b_f[:, L - 1:L], b_b[:, 0:1])
        cm = jnp.where(is_fwd, _lane_scan_max(r, lane, False), _lane_scan_max(r, lane, True))
        rmax = jnp.max(r, axis=1, keepdims=True)
        rows_ref[t] = jnp.concatenate([r, cm, b, jnp.broadcast_to(g, (N_CHAINS, L)),
                                       jnp.broadcast_to(rmax, (N_CHAINS, L))], axis=0)
        cols_ref[t * L:(t + 1) * L, :] = jnp.concatenate([r, jnp.zeros((L - N_CHAINS, L), F32)], axis=0).T


def _gateprep(gates, gate_bias):
    n = gates.shape[0]
    L = MLSTM_CHUNK
    tn = GATEPREP_CHUNKS * L
    return pl.pallas_call(
        _gateprep_kernel,
        grid=(n // tn,),
        in_specs=[pl.BlockSpec((tn, LANES), lambda i: (i, 0)),
                  pl.BlockSpec((1, LANES), lambda i: (0, 0))],
        out_specs=[pl.BlockSpec((GATEPREP_CHUNKS, GROWS, LANES), lambda i: (i, 0, 0)),
                   pl.BlockSpec((tn, LANES), lambda i: (i, 0))],
        out_shape=[jax.ShapeDtypeStruct((n // L, GROWS, LANES), F32),
                   jax.ShapeDtypeStruct((n, LANES), F32)],
        compiler_params=_cparams("arbitrary"),
        name="gateprep",
    )(gates, gate_bias)


def _mlstm_scores(k, qt, n_ref, ci):
    n_st = n_ref[ci]
    n_hi = n_st.astype(BF16).astype(F32)
    n_lo = (n_st - n_hi).astype(BF16).astype(F32)
    nn = jnp.concatenate([n_hi, n_lo, jnp.zeros((BF16_ROWS - 2, MLSTM_HEAD_DIM), F32)], axis=0).astype(BF16)
    return jnp.dot(k, qt, preferred_element_type=F32), jnp.dot(nn, qt, preferred_element_type=F32)


def _mlstm_chain(kq, qn2, k, qt, vt, rows, cols, mask_sj, c_ref, n_ref, m_ref, ci):
    L = MLSTM_CHUNK
    scale = MLSTM_HEAD_DIM ** -0.5
    r_row = rows[GROW_R + ci:GROW_R + ci + 1]
    cm_row = rows[GROW_CM + ci:GROW_CM + ci + 1]
    b_row = rows[GROW_B + ci:GROW_B + ci + 1]
    g = rows[GROW_G + ci:GROW_G + ci + 1, 0:1]
    rmax = rows[GROW_RMAX + ci:GROW_RMAX + ci + 1, 0:1]
    r_colb = jnp.broadcast_to(cols[:, ci:ci + 1], (L, L))
    c_st = c_ref[ci]
    n_st = n_ref[ci]
    m_st = m_ref[ci][:, 0:1]
    mm = jnp.maximum(cm_row, m_st)
    pt = jnp.where(mask_sj, jnp.exp(r_colb - mm), 0.0)
    st = kq * scale * pt
    s_int = jnp.exp(m_st - mm)
    den = jnp.sum(st, axis=0, keepdims=True) + s_int * (qn2[0:1] + qn2[1:2])
    inv = 1.0 / jnp.maximum(jnp.abs(den), jnp.exp(-(b_row + mm)))
    lhs = jnp.concatenate([vt, c_st.astype(BF16)], axis=1)
    rhs = jnp.concatenate([st.astype(BF16), (qt.astype(F32) * s_int).astype(BF16)], axis=0)
    h = (jnp.dot(lhs, rhs, preferred_element_type=F32) * inv).astype(BF16).T

    w_row = jnp.exp(r_row - rmax)
    wl = jnp.concatenate([(vt.astype(F32) * w_row).astype(BF16),
                          jnp.broadcast_to(w_row, (BF16_ROWS, L)).astype(BF16)], axis=0)
    loc = jnp.dot(wl, k, preferred_element_type=F32) * scale
    m_loc = g + rmax
    m_new = jnp.maximum(g + m_st, m_loc)
    s_old = jnp.exp(g + m_st - m_new)
    s_loc = jnp.exp(m_loc - m_new)
    c_ref[ci] = s_old * c_st + s_loc * loc[0:MLSTM_HEAD_DIM]
    n_ref[ci] = s_old * n_st + s_loc * loc[MLSTM_HEAD_DIM:MLSTM_HEAD_DIM + 1]
    m_ref[ci] = jnp.broadcast_to(m_new, (1, LANES))
    return h


def _mlstm_kernel(kf_ref, kb_ref, qtf_ref, qtb_ref, vtf_ref, vtb_ref, rf_ref, rb_ref, cf_ref, cb_ref,
                  hf_ref, hb_ref, c_ref, n_ref, m_ref):
    L = MLSTM_CHUNK

    @pl.when(pl.program_id(1) == 0)
    def _():
        c_ref[...] = jnp.zeros_like(c_ref)
        n_ref[...] = jnp.zeros_like(n_ref)
        m_ref[...] = jnp.zeros_like(m_ref)

    row = lax.broadcasted_iota(jnp.int32, (L, L), 0)
    col = lax.broadcasted_iota(jnp.int32, (L, L), 1)
    dirs = ((kf_ref, qtf_ref, vtf_ref, rf_ref, cf_ref, hf_ref, row <= col),
            (kb_ref, qtb_ref, vtb_ref, rb_ref, cb_ref, hb_ref, row >= col))

    def operands(ci):
        d, hh = divmod(ci, MLSTM_HEADS)
        sl = slice(hh * MLSTM_HEAD_DIM, (hh + 1) * MLSTM_HEAD_DIM)
        return dirs[d], sl

    def scores(ci):
        (k_ref, qt_ref, _, _, _, _, _), sl = operands(ci)
        return _mlstm_scores(k_ref[0, :, sl], qt_ref[sl, :], n_ref, ci)

    pre = [scores(ci) for ci in range(MLSTM_LOOKAHEAD)]
    for ci in range(N_CHAINS):
        if ci + MLSTM_LOOKAHEAD < N_CHAINS:
            pre.append(scores(ci + MLSTM_LOOKAHEAD))
        kq, qn2 = pre.pop(0)
        (k_ref, qt_ref, vt_ref, r_ref, cl_ref, o_ref, mask), sl = operands(ci)
        o_ref[0, :, sl] = _mlstm_chain(kq, qn2, k_ref[0, :, sl], qt_ref[sl, :], vt_ref[sl, :],
                                       r_ref[0], cl_ref[0], mask, c_ref, n_ref, m_ref, ci)


def _mlstm(k, qt, vt, grows, gcols):
    bsz, seq, _ = k.shape
    L = MLSTM_CHUNK
    nc = seq // L
    fwd = lambda b, c: (b, c, 0)
    bwd = lambda b, c: (b, nc - 1 - c, 0)
    fwd_t = lambda b, c: (0, b * nc + c)
    bwd_t = lambda b, c: (0, b * nc + nc - 1 - c)
    fwd_r = lambda b, c: (b * nc + c, 0, 0)
    bwd_r = lambda b, c: (b * nc + nc - 1 - c, 0, 0)
    return pl.pallas_call(
        _mlstm_kernel,
        grid=(bsz, nc),
        in_specs=[pl.BlockSpec((1, L, MLSTM_WIDTH), fwd), pl.BlockSpec((1, L, MLSTM_WIDTH), bwd),
                  pl.BlockSpec((MLSTM_WIDTH, L), fwd_t), pl.BlockSpec((MLSTM_WIDTH, L), bwd_t),
                  pl.BlockSpec((MLSTM_WIDTH, L), fwd_t), pl.BlockSpec((MLSTM_WIDTH, L), bwd_t),
                  pl.BlockSpec((1, GROWS, LANES), fwd_r), pl.BlockSpec((1, GROWS, LANES), bwd_r),
                  pl.BlockSpec((1, L, LANES), fwd), pl.BlockSpec((1, L, LANES), bwd)],
        out_specs=[pl.BlockSpec((1, L, MLSTM_WIDTH), fwd),
                   pl.BlockSpec((1, L, MLSTM_WIDTH), bwd)],
        out_shape=[jax.ShapeDtypeStruct((bsz, seq, MLSTM_WIDTH), BF16),
                   jax.ShapeDtypeStruct((bsz, seq, MLSTM_WIDTH), BF16)],
        scratch_shapes=[pltpu.VMEM((N_CHAINS, MLSTM_HEAD_DIM, MLSTM_HEAD_DIM), F32),
                        pltpu.VMEM((N_CHAINS, 1, MLSTM_HEAD_DIM), F32),
                        pltpu.VMEM((N_CHAINS, 1, LANES), F32)],
        compiler_params=_cparams("arbitrary", "arbitrary"),
        name="mlstm",
    )(k, k, qt, qt, vt, vt, grows, grows, gcols, gcols)


NA_QROWS = 8
NA_KROWS = 2 * NA_QROWS
NA_KBLK = 4
NA_NKBLK = NA_KROWS // NA_KBLK
NA_CODE_SECOND = 2 * NA_MAX_KH - 2
NA_CODE_FIRST = NA_CODE_SECOND + NA_MAX_KH
NA_CODE_NONE = NA_CODE_FIRST + NA_MAX_KH
NA_NCODES = NA_CODE_NONE + 1


NA_LIVE_PAIRS = NA_MAX_KH // 2 + 1


def _natten_kernel(q_ref, k0_ref, k1_ref, k2_ref, k3_ref, v0_ref, v1_ref, v2_ref, v3_ref, bias_ref, o_ref,
                   st0_ref, st1_ref, pt0_ref, pt1_ref, *, rows):
    i = pl.program_id(1)
    half = NA_MAX_KH // 2
    rq0 = i * NA_QROWS
    w0 = jnp.clip(rq0 - half, 0, rows - NA_KROWS)
    nqp = NA_QROWS // 2
    nkp = NA_KROWS // 2
    tq = NA_QROWS * GRID_W

    def window_start(r):
        return jnp.clip(r - half, 0, rows - NA_MAX_KH)

    def code(rq, kp):
        r = rq0 + rq
        r0 = window_start(r)
        kr = w0 + 2 * kp
        d0 = kr - r + (NA_MAX_KH - 1)
        va = (kr >= r0) & (kr < r0 + NA_MAX_KH)
        vb = (kr + 1 >= r0) & (kr + 1 < r0 + NA_MAX_KH)
        return jnp.where(va & vb, d0,
                         jnp.where(vb, NA_CODE_SECOND + d0 + 1,
                                   jnp.where(va, NA_CODE_FIRST + d0 - (NA_MAX_KH - 1), NA_CODE_NONE)))

    first = [jnp.minimum((window_start(rq0 + 2 * qi) - w0) // 2, nkp - NA_LIVE_PAIRS) for qi in range(nqp)]
    codes = [[(code(2 * qi, first[qi] + t), code(2 * qi + 1, first[qi] + t)) for t in range(NA_LIVE_PAIRS)]
             for qi in range(nqp)]
    lane_t = lax.broadcasted_iota(jnp.int32, (LANES, LANES), 1)
    lane_q = lax.broadcasted_iota(jnp.int32, (tq, LANES), 1)
    sub_o = lax.broadcasted_iota(jnp.int32, (LANES, tq), 0)
    scale = jnp.asarray(NA_HEAD_DIM ** -0.5, BF16)
    k_refs = (k0_ref, k1_ref, k2_ref, k3_ref)
    v_refs = (v0_ref, v1_ref, v2_ref, v3_ref)
    st_refs = (st0_ref, st1_ref)
    pt_refs = (pt0_ref, pt1_ref)
    pt0_ref[...] = jnp.zeros_like(pt0_ref)
    pt1_ref[...] = jnp.zeros_like(pt1_ref)

    def lanes_of(hp):
        return slice(hp * LANES, (hp + 1) * LANES)

    def scores(h):
        hp, sb = divmod(h, 2)
        lsl = lanes_of(hp)
        qp = q_ref[0, :, lsl] * scale
        own = (lane_q >= sb * NA_HEAD_DIM) & (lane_q < (sb + 1) * NA_HEAD_DIM)
        qm = jnp.where(own, qp, jnp.zeros_like(qp))
        k_all = jnp.concatenate([r[0, :, lsl] for r in k_refs], axis=0)
        st_refs[sb][...] = lax.dot_general(k_all, qm, (((1,), (1,)), ((), ())), preferred_element_type=F32)

    def attend(h):
        hp, sb = divmod(h, 2)
        st_ref, pt_ref = st_refs[sb], pt_refs[sb]
        invs = []
        for qi in range(nqp):
            qsl = slice(qi * LANES, (qi + 1) * LANES)
            offs = [pl.multiple_of((first[qi] + t) * LANES, LANES) for t in range(NA_LIVE_PAIRS)]
            tiles = []
            for t in range(NA_LIVE_PAIRS):
                ca, cb = codes[qi][t]
                bias = jnp.where(lane_t < GRID_W, bias_ref[h, ca], bias_ref[h, cb])
                tiles.append(st_ref[pl.ds(offs[t], LANES), qsl] + bias)
            m = functools.reduce(jnp.maximum, tiles)
            m = jnp.max(m, axis=0, keepdims=True)
            es = [jnp.exp(t - m) for t in tiles]
            l = jnp.sum(functools.reduce(jnp.add, es), axis=0, keepdims=True)
            invs.append(1.0 / l)
            for t in range(NA_LIVE_PAIRS):
                pt_ref[pl.ds(offs[t], LANES), qsl] = es[t].astype(BF16)
        vt = jnp.concatenate([r[0, :, lanes_of(hp)] for r in v_refs], axis=0).T
        return jnp.dot(vt, pt_ref[...], preferred_element_type=F32) * jnp.concatenate(invs, axis=1)

    scores(0)
    outs = []
    for h in range(NA_HEADS):
        if h + 1 < NA_HEADS:
            scores(h + 1)
        outs.append(attend(h))
        if h % 2 == 1:
            ot = jnp.where(sub_o < NA_HEAD_DIM, outs[0], outs[1])
            o_ref[0, :, lanes_of(h // 2)] = ot.T.astype(o_ref.dtype)
            outs = []


def _natten_bias_table(rel_bias):
    cols = jnp.arange(GRID_W)
    col_start = jnp.clip(cols - NA_KW // 2, 0, GRID_W - NA_KW)
    kcol = jnp.arange(GRID_W)
    dc = kcol[:, None] - cols[None, :] + (NA_KW - 1)
    inwin = (kcol[:, None] >= col_start[None, :]) & (kcol[:, None] < col_start[None, :] + NA_KW)
    tab = rel_bias.astype(F32)[:, :, jnp.clip(dc, 0, 2 * NA_KW - 2)]
    tab = jnp.where(inwin[None, None], tab, NEG_BIG)
    neg = jnp.full_like(tab[:, :NA_MAX_KH], NEG_BIG)
    both = jnp.concatenate([tab[:, :-1], tab[:, 1:]], axis=2)
    second = jnp.concatenate([neg, tab[:, :NA_MAX_KH]], axis=2)
    first = jnp.concatenate([tab[:, NA_MAX_KH - 1:], neg], axis=2)
    none = jnp.concatenate([neg[:, :1], neg[:, :1]], axis=2)
    t = jnp.concatenate([both, second, first, none], axis=1)
    return jnp.concatenate([t, t], axis=-1)


def _natten(na, bias_table):
    bsz, seq, _ = na.shape
    rows = seq // GRID_W
    tq = NA_QROWS * GRID_W
    tk = NA_KBLK * GRID_W
    nkb = seq // tk

    def kv_spec(j, col):
        return pl.BlockSpec((1, tk, NA_WIDTH),
                            lambda b, i: (b, jnp.clip(2 * i - 1, 0, nkb - NA_NKBLK) + j, col))

    return pl.pallas_call(
        functools.partial(_natten_kernel, rows=rows),
        grid=(bsz, rows // NA_QROWS),
        in_specs=[pl.BlockSpec((1, tq, NA_WIDTH), lambda b, i: (b, i, 0))]
                 + [kv_spec(j, 1) for j in range(NA_NKBLK)]
                 + [kv_spec(j, 2) for j in range(NA_NKBLK)]
                 + [pl.BlockSpec(bias_table.shape, lambda b, i: (0, 0, 0, 0), pipeline_mode=pl.Buffered(1))],
        out_specs=pl.BlockSpec((1, tq, NA_WIDTH), lambda b, i: (b, i, 0)),
        out_shape=jax.ShapeDtypeStruct((bsz, seq, NA_WIDTH), BF16),
        scratch_shapes=[pltpu.VMEM((NA_KROWS * GRID_W, tq), F32), pltpu.VMEM((NA_KROWS * GRID_W, tq), F32),
                        pltpu.VMEM((NA_KROWS * GRID_W, tq), BF16), pltpu.VMEM((NA_KROWS * GRID_W, tq), BF16)],
        compiler_params=_cparams("arbitrary", "arbitrary"),
        name="natten",
    )(na, *([na] * (2 * NA_NKBLK)), bias_table)


AUG = 2 * LANES
GATE_LO_SHIFT = N_EXPERTS


def _postmix_kernel(hf_ref, hb_ref, o_ref, hn_ref, x_ref, mod_ref, gm_ref, wm_ref, wn_ref,
                    gpost_ref, gffn_ref, wr_ref, x1_ref, x2_ref, aff_ref):
    d = x_ref.shape[-1]
    s = hf_ref[...].astype(F32) + hb_ref[...].astype(F32)
    og = jax.nn.sigmoid(o_ref[...].astype(F32))
    heads = []
    for hh in range(MLSTM_HEADS):
        sl = slice(hh * MLSTM_HEAD_DIM, (hh + 1) * MLSTM_HEAD_DIM)
        heads.append(_rms(s[:, sl]) * gm_ref[:, sl] * og[:, sl])
    hm = jnp.concatenate(heads, axis=-1).astype(BF16)
    mix = (jnp.dot(hm, wm_ref[...], preferred_element_type=F32)
           + jnp.dot(hn_ref[...], wn_ref[...], preferred_element_type=F32))
    gate1 = mod_ref[0, 2:3, :]
    shift2 = mod_ref[0, 3:4, :]
    scale2 = mod_ref[0, 4:5, :]
    x1 = x_ref[...] + gate1 * (_rms(mix) * gpost_ref[...])
    x1_ref[...] = x1
    h2 = (_rms(x1) * gffn_ref[...]) * (1.0 + scale2) + shift2
    h2b = h2.astype(BF16)
    x2_ref[:, 0:d] = h2b
    logits = jnp.dot(h2b, wr_ref[...], preferred_element_type=F32)
    lane = lax.broadcasted_iota(jnp.int32, logits.shape, 1)
    logits = jnp.where(lane < N_EXPERTS, logits, NEG_BIG)
    ex = jnp.exp(logits - jnp.max(logits, axis=-1, keepdims=True))
    aff = ex / jnp.sum(ex, axis=-1, keepdims=True)
    aff_ref[...] = aff
    hi = aff.astype(BF16).astype(F32)
    lo = (aff - hi).astype(BF16).astype(F32)
    x2_ref[:, d:d + LANES] = (hi + pltpu.roll(lo, GATE_LO_SHIFT, 1)).astype(BF16)
    x2_ref[:, d + LANES:d + AUG] = jnp.zeros((x2_ref.shape[0], AUG - LANES), BF16)


def _postmix(hf, hb, o, hn, x2d, mod, g_mlstm, w_om, w_on, g_post, g_ffn, w_r, seq, tm=512):
    n, d = x2d.shape
    tiles_per_b = seq // tm
    full = lambda a: pl.BlockSpec(a.shape, lambda i: (0, 0))
    tok = lambda w: pl.BlockSpec((tm, w), lambda i: (i, 0))
    return pl.pallas_call(
        _postmix_kernel,
        grid=(n // tm,),
        in_specs=[tok(MLSTM_WIDTH), tok(MLSTM_WIDTH), tok(MLSTM_WIDTH), tok(NA_WIDTH), tok(d),
                  pl.BlockSpec((1, N_MOD, d), lambda i: (i // tiles_per_b, 0, 0)),
                  full(g_mlstm), full(w_om), full(w_on), full(g_post), full(g_ffn), full(w_r)],
        out_specs=[tok(d), tok(d + AUG), tok(LANES)],
        out_shape=[jax.ShapeDtypeStruct((n, d), F32),
                   jax.ShapeDtypeStruct((n, d + AUG), BF16),
                   jax.ShapeDtypeStruct((n, LANES), F32)],
        compiler_params=_cparams("arbitrary"),
        name="postmix",
    )(hf, hb, o, hn, x2d, mod, g_mlstm, w_om, w_on, g_post, g_ffn, w_r)


TOK_TILE = 128
SEL_STEP = 1024
UNSELECTED = -1e6


def _select_kernel(afft_ref, aff_ref, pos_ref, post_ref, tb_ref, thr_ref, need_ref, base_ref, *, cap):
    step = pl.program_id(0)
    ne = afft_ref.shape[0]

    @pl.when(step == 0)
    def _():
        bits = pltpu.bitcast(afft_ref[...], jnp.int32)

        def body(it, lo):
            cand = lo | lax.shift_left(jnp.int32(1), 30 - it)
            cnt = jnp.sum(jnp.where(bits >= cand, 1.0, 0.0), axis=1, keepdims=True)
            return jnp.where(cnt >= cap, cand, lo)

        thr_bits = lax.fori_loop(0, 31, body, jnp.zeros((ne, 1), jnp.int32))
        cnt_gt = jnp.sum(jnp.where(bits > thr_bits, 1.0, 0.0), axis=1, keepdims=True)
        thr_col = pltpu.bitcast(thr_bits, F32)
        need_col = cap - cnt_gt
        sub = lax.broadcasted_iota(jnp.int32, (ne, LANES), 0)
        ln = lax.broadcasted_iota(jnp.int32, (ne, LANES), 1)
        diag = sub == ln
        thr_ref[...] = jnp.sum(jnp.where(diag, thr_col, 0.0), axis=0, keepdims=True)
        need_ref[...] = jnp.sum(jnp.where(diag, need_col, 0.0), axis=0, keepdims=True)
        base_ref[...] = jnp.zeros_like(base_ref)

    thr = thr_ref[...]
    need = need_ref[...]
    lane = lax.broadcasted_iota(jnp.int32, (TOK_TILE, LANES), 1)
    valid = lane < ne
    row = lax.broadcasted_iota(jnp.int32, (TOK_TILE, TOK_TILE), 0)
    col = lax.broadcasted_iota(jnp.int32, (TOK_TILE, TOK_TILE), 1)
    strict_lower = (col < row).astype(BF16)
    base_gt = base_ref[0:1, :]
    base_eq = base_ref[1:2, :]
    tb_rows = []
    for t in range(SEL_STEP // TOK_TILE):
        a = aff_ref[t * TOK_TILE:(t + 1) * TOK_TILE, :]
        gt = (a > thr) & valid
        eq = (a == thr) & valid
        gtf = jnp.where(gt, 1.0, 0.0)
        eqf = jnp.where(eq, 1.0, 0.0)
        cg = jnp.dot(strict_lower, gtf.astype(BF16), preferred_element_type=F32)
        ce = jnp.dot(strict_lower, eqf.astype(BF16), preferred_element_type=F32)
        eq_rank = base_eq + ce
        sel = gt | (eq & (eq_rank < need))
        pos = base_gt + cg + jnp.minimum(eq_rank, need)
        posm = jnp.where(sel, pos, UNSELECTED)
        pos_ref[t * TOK_TILE:(t + 1) * TOK_TILE, :] = posm
        post_ref[:, t * TOK_TILE:(t + 1) * TOK_TILE] = posm.T[0:ne, :]
        tb_rows.append(base_gt + jnp.minimum(base_eq, need))
        base_gt = base_gt + jnp.sum(gtf, axis=0, keepdims=True)
        base_eq = base_eq + jnp.sum(eqf, axis=0, keepdims=True)
    tb_ref[...] = jnp.concatenate(tb_rows, axis=0).astype(jnp.int32)
    base_ref[0:1, :] = base_gt
    base_ref[1:2, :] = base_eq


def _select(aff, cap):
    n = aff.shape[0]
    afft = aff[:, :N_EXPERTS].T
    nsteps = n // SEL_STEP
    tiles_per_step = SEL_STEP // TOK_TILE
    return pl.pallas_call(
        functools.partial(_select_kernel, cap=cap),
        grid=(nsteps,),
        in_specs=[pl.BlockSpec((N_EXPERTS, n), lambda i: (0, 0)),
                  pl.BlockSpec((SEL_STEP, LANES), lambda i: (i, 0))],
        out_specs=[pl.BlockSpec((SEL_STEP, LANES), lambda i: (i, 0)),
                   pl.BlockSpec((N_EXPERTS, SEL_STEP), lambda i: (0, i)),
                   pl.BlockSpec((tiles_per_step, LANES), lambda i: (i, 0))],
        out_shape=[jax.ShapeDtypeStruct((n, LANES), F32),
                   jax.ShapeDtypeStruct((N_EXPERTS, n), F32),
                   jax.ShapeDtypeStruct((n // TOK_TILE, LANES), jnp.int32)],
        scratch_shapes=[pltpu.VMEM((1, LANES), F32), pltpu.VMEM((1, LANES), F32),
                        pltpu.VMEM((SUBLANES, LANES), F32)],
        compiler_params=_cparams("arbitrary"),
        name="select",
    )(afft, aff)


DISP_FAST = 40
DISP_SLOW = SUBLANES + TOK_TILE + SUBLANES
CARRY = SUBLANES
XE_PAD = 2 * TOK_TILE
U32 = jnp.uint32
HI16 = 0xFFFF0000


def _pack_halves(x):
    w = x.shape[1] // 2
    lo = lax.bitcast_convert_type(x[:, 0:w], U32)
    hi = lax.bitcast_convert_type(x[:, w:], U32)
    return lax.shift_right_logical(lo, U32(16)) | (hi & U32(HI16))


def _unpack_halves(p):
    lo = lax.bitcast_convert_type(lax.shift_left(p, U32(16)), F32)
    hi = lax.bitcast_convert_type(p & U32(HI16), F32)
    return lo, hi


def _dispatch_kernel(tb_ref, x_ref, post_ref, xe_ref, full_ref, stage_ref, carry_ref, sem, flag_ref, *, cap, ntiles):
    i = pl.program_id(0)
    slot = lax.rem(i, 2)
    ne = N_EXPERTS

    @pl.when(i == 0)
    def _():
        carry_ref[...] = jnp.zeros_like(carry_ref)
        stage_ref[1, 0:XE_PAD, :] = jnp.zeros((XE_PAD, stage_ref.shape[-1]), U32)
        pads = [pltpu.make_async_copy(stage_ref.at[1, pl.ds(0, XE_PAD)], xe_ref.at[e, pl.ds(cap, XE_PAD)],
                                      sem.at[1]) for e in range(ne)]
        for cp in pads:
            cp.start()
        for cp in pads:
            cp.wait()

    fill = [tb_ref[i * ne + e] for e in range(ne)]
    nxt = [jnp.where(i + 1 < ntiles, tb_ref[jnp.minimum(i + 1, ntiles - 1) * ne + e], cap) for e in range(ne)]
    al = [(f // CARRY) * CARRY for f in fill]
    need_max = functools.reduce(jnp.maximum, [nx - a for nx, a in zip(nxt, al)])
    slow = need_max > DISP_FAST

    def copies(s_rows, sl, als):
        return [pltpu.make_async_copy(stage_ref.at[sl, pl.ds(e * s_rows, s_rows)],
                                      xe_ref.at[e, pl.ds(pl.multiple_of(als[e], CARRY), s_rows)],
                                      sem.at[sl]) for e in range(ne)]

    def wait_all(sl):
        zeros = [0] * ne

        def w(s_rows):
            for cp in copies(s_rows, sl, zeros):
                cp.wait()
        lax.cond(flag_ref[sl] == 1, lambda: w(DISP_SLOW), lambda: w(DISP_FAST))

    def body(s_rows):
        xt = x_ref[...]
        cb = carry_ref[...].astype(BF16)
        sub = lax.broadcasted_iota(jnp.int32, (s_rows, TOK_TILE), 0)
        ln = lax.broadcasted_iota(jnp.int32, (s_rows, TOK_TILE), 1)
        a_tok, a_car = [], []
        for e in range(ne):
            rel = post_ref[e:e + 1, :] - al[e].astype(F32)
            a_tok.append(jnp.where(sub.astype(F32) == rel, 1.0, 0.0))
            own = (ln >= e * CARRY) & (ln < e * CARRY + (fill[e] - al[e])) & (ln - e * CARRY == sub)
            a_car.append(jnp.where(own, 1.0, 0.0))
        staged = (jnp.dot(jnp.concatenate(a_tok, axis=0).astype(BF16), xt, preferred_element_type=F32)
                  + jnp.dot(jnp.concatenate(a_car, axis=0).astype(BF16), cb, preferred_element_type=F32))
        full_ref[0:ne * s_rows, :] = staged
        stage_ref[slot, 0:ne * s_rows, :] = _pack_halves(staged)
        for e in range(ne):
            off = jnp.minimum((nxt[e] // CARRY) * CARRY - al[e], s_rows - CARRY)
            carry_ref[e * CARRY:(e + 1) * CARRY, :] = full_ref[pl.ds(pl.multiple_of(e * s_rows + off, CARRY), CARRY), :]

        @pl.when(i > 0)
        def _():
            wait_all(1 - slot)
        for cp in copies(s_rows, slot, al):
            cp.start()

    @pl.when(slow)
    def _():
        body(DISP_SLOW)
        flag_ref[slot] = 1

    @pl.when(jnp.logical_not(slow))
    def _():
        body(DISP_FAST)
        flag_ref[slot] = 0

    @pl.when(i == ntiles - 1)
    def _():
        wait_all(slot)


def _dispatch(x2aug, post, tb, cap):
    n, wd = x2aug.shape
    ntiles = n // TOK_TILE
    tb_flat = tb[:, :N_EXPERTS].reshape(-1)
    return pl.pallas_call(
        functools.partial(_dispatch_kernel, cap=cap, ntiles=ntiles),
        grid_spec=pltpu.PrefetchScalarGridSpec(
            num_scalar_prefetch=1,
            grid=(ntiles,),
            in_specs=[pl.BlockSpec((TOK_TILE, wd), lambda i, tb: (i, 0)),
                      pl.BlockSpec((N_EXPERTS, TOK_TILE), lambda i, tb: (0, i))],
            out_specs=pl.BlockSpec(memory_space=pl.ANY),
            scratch_shapes=[pltpu.VMEM((N_EXPERTS * DISP_SLOW, wd), F32),
                            pltpu.VMEM((2, N_EXPERTS * DISP_SLOW, wd // 2), U32),
                            pltpu.VMEM((N_EXPERTS * CARRY, wd), F32),
                            pltpu.SemaphoreType.DMA((2,)),
                            pltpu.SMEM((2,), jnp.int32)]),
        out_shape=jax.ShapeDtypeStruct((N_EXPERTS, cap + XE_PAD, wd // 2), U32),
        compiler_params=_cparams("arbitrary"),
        name="dispatch",
    )(tb_flat, x2aug, post)


EXP_ROWS = 512


def _experts_kernel(xe_ref, wg_ref, wu_ref, wd_ref, ye_ref, wgb_ref, wub_ref, wdb_ref):
    e = pl.program_id(0)
    d = wg_ref.shape[1]

    @pl.when(pl.program_id(1) == 0)
    def _():
        wgb_ref[...] = wg_ref[0].astype(BF16)
        wub_ref[...] = wu_ref[0].astype(BF16)
        wdb_ref[...] = wd_ref[0].astype(BF16)

    lo, hi = _unpack_halves(xe_ref[0])
    w = lo.shape[1]
    x = jnp.concatenate([lo, hi[:, 0:d - w]], axis=1).astype(BF16)
    aug = hi[:, d - w:d - w + LANES]
    lane = lax.broadcasted_iota(jnp.int32, aug.shape, 1)
    mine = (lane == e) | (lane == e + GATE_LO_SHIFT)
    gate = jnp.sum(jnp.where(mine, aug, 0.0), axis=-1, keepdims=True)
    g = jnp.dot(x, wgb_ref[...], preferred_element_type=F32)
    u = jnp.dot(x, wub_ref[...], preferred_element_type=F32)
    hid = (g * jax.nn.sigmoid(g)) * u
    y = jnp.dot(hid.astype(BF16), wdb_ref[...], preferred_element_type=F32)
    ye_ref[0] = (y * gate).astype(ye_ref.dtype)


def _experts(xe, w_gate, w_up, w_down, cap):
    ne, _, wd = xe.shape
    d = w_gate.shape[1]
    f = w_gate.shape[2]
    wspec = lambda s: pl.BlockSpec((1,) + s, lambda e, j: (e, 0, 0))
    rows = min(EXP_ROWS, cap)
    return pl.pallas_call(
        _experts_kernel,
        grid=(ne, cap // rows),
        in_specs=[pl.BlockSpec((1, rows, wd), lambda e, j: (e, j, 0)),
                  wspec((d, f)), wspec((d, f)), wspec((f, d))],
        out_specs=pl.BlockSpec((1, rows, d), lambda e, j: (e, j, 0)),
        out_shape=jax.ShapeDtypeStruct((ne, cap, d), BF16),
        scratch_shapes=[pltpu.VMEM((d, f), BF16), pltpu.VMEM((d, f), BF16), pltpu.VMEM((f, d), BF16)],
        compiler_params=_cparams("arbitrary", "arbitrary"),
        name="experts",
    )(xe, w_gate, w_up, w_down)


COMB_FAST = 64
COMB_SLOW = 256
COMB_ALIGN = BF16_ROWS


def _combine_kernel(tb_ref, pos_ref, x1_ref, mod_ref, g_ref, ye_ref, y_ref, slab_ref, sem, *, cap, ntiles):
    i = pl.program_id(0)
    slot = lax.rem(i, 2)
    ne = N_EXPERTS

    def info(t):
        t = jnp.minimum(t, ntiles - 1)
        fill = [tb_ref[t * ne + e] for e in range(ne)]
        nxt = [jnp.where(t + 1 < ntiles, tb_ref[jnp.minimum(t + 1, ntiles - 1) * ne + e], cap) for e in range(ne)]
        st_f = [jnp.minimum((f // COMB_ALIGN) * COMB_ALIGN, cap - COMB_FAST) for f in fill]
        st_s = [jnp.minimum((f // COMB_ALIGN) * COMB_ALIGN, cap - COMB_SLOW) for f in fill]
        slow = functools.reduce(jnp.maximum, [nx - s for nx, s in zip(nxt, st_f)]) > COMB_FAST
        return st_f, st_s, slow

    def copies(s_rows, sl, starts):
        return [pltpu.make_async_copy(ye_ref.at[e, pl.ds(pl.multiple_of(starts[e], COMB_ALIGN), s_rows)],
                                      slab_ref.at[sl, pl.ds(e * s_rows, s_rows)],
                                      sem.at[sl]) for e in range(ne)]

    def fetch(t, sl):
        st_f, st_s, slow = info(t)

        @pl.when(slow)
        def _():
            for cp in copies(COMB_SLOW, sl, st_s):
                cp.start()

        @pl.when(jnp.logical_not(slow))
        def _():
            for cp in copies(COMB_FAST, sl, st_f):
                cp.start()

    @pl.when(i == 0)
    def _():
        fetch(i, slot)

    @pl.when(i + 1 < ntiles)
    def _():
        fetch(i + 1, 1 - slot)

    st_f, st_s, slow = info(i)
    lane = lax.broadcasted_iota(jnp.int32, (TOK_TILE, LANES), 1).astype(F32)
    pos = pos_ref[...]

    def finish(moe):
        gate2 = mod_ref[0, 5:6, :]
        y_ref[...] = x1_ref[...] + gate2 * (_rms(moe) * g_ref[...])

    @pl.when(jnp.logical_not(slow))
    def _():
        for cp in copies(COMB_FAST, slot, st_f):
            cp.wait()
        per = LANES // COMB_FAST
        groups = []
        for gi in range(ne // per):
            val = None
            for k in range(per):
                e = gi * per + k
                rel = pos[:, e:e + 1] - st_f[e].astype(F32) + float(k * COMB_FAST)
                inband = (lane >= k * COMB_FAST) & (lane < (k + 1) * COMB_FAST)
                hit = inband & (lane == rel)
                val = hit if val is None else (val | hit)
            groups.append(jnp.where(val, 1.0, 0.0).astype(BF16))
        onehot = jnp.concatenate(groups, axis=1)
        finish(jnp.dot(onehot, slab_ref[slot, 0:ne * COMB_FAST, :], preferred_element_type=F32))

    @pl.when(slow)
    def _():
        for cp in copies(COMB_SLOW, slot, st_s):
            cp.wait()
        per = COMB_SLOW // LANES
        groups = []
        for e in range(ne):
            rel = pos[:, e:e + 1] - st_s[e].astype(F32)
            for k in range(per):
                groups.append(jnp.where(lane + float(k * LANES) == rel, 1.0, 0.0).astype(BF16))
        onehot = jnp.concatenate(groups, axis=1)
        finish(jnp.dot(onehot, slab_ref[slot], preferred_element_type=F32))


def _combine(tb, pos, x1, mod, g_post_ffn, ye, cap, seq):
    n, d = x1.shape
    ntiles = n // TOK_TILE
    tiles_per_b = seq // TOK_TILE
    tb_flat = tb[:, :N_EXPERTS].reshape(-1)
    return pl.pallas_call(
        functools.partial(_combine_kernel, cap=cap, ntiles=ntiles),
        grid_spec=pltpu.PrefetchScalarGridSpec(
            num_scalar_prefetch=1,
            grid=(ntiles,),
            in_specs=[pl.BlockSpec((TOK_TILE, LANES), lambda i, tb: (i, 0)),
                      pl.BlockSpec((TOK_TILE, d), lambda i, tb: (i, 0)),
                      pl.BlockSpec((1, N_MOD, d), lambda i, tb: (i // tiles_per_b, 0, 0)),
                      pl.BlockSpec((1, d), lambda i, tb: (0, 0)),
                      pl.BlockSpec(memory_space=pl.ANY)],
            out_specs=pl.BlockSpec((TOK_TILE, d), lambda i, tb: (i, 0)),
            scratch_shapes=[pltpu.VMEM((2, N_EXPERTS * COMB_SLOW, d), BF16),
                            pltpu.SemaphoreType.DMA((2,))]),
        out_shape=jax.ShapeDtypeStruct((n, d), F32),
        compiler_params=_cparams("arbitrary"),
        name="combine",
    )(tb_flat, pos, x1, mod, g_post_ffn, ye)


def _prep_weights(w_in, b_gates, g_mlstm, na_rel_bias, w_out, w_router):
    w = MLSTM_WIDTH
    ng = 4 * MLSTM_HEADS
    w_qt = w_in[:, 0:w].T.astype(BF16)
    w_k = w_in[:, w:2 * w].astype(BF16)
    w_vt = w_in[:, 2 * w:3 * w].T.astype(BF16)
    w_o = w_in[:, 3 * w:4 * w].astype(BF16)
    perm = [d * 2 * MLSTM_HEADS + g * MLSTM_HEADS + h
            for g in range(2) for d in range(2) for h in range(MLSTM_HEADS)]
    perm = jnp.asarray(perm)
    w_g = jnp.pad(w_in[:, 4 * w:4 * w + ng][:, perm], ((0, 0), (0, LANES - ng))).astype(BF16)
    b_g = jnp.pad(b_gates.reshape(-1)[perm].astype(F32), (0, LANES - ng)).reshape(1, LANES)
    w_n = w_in[:, 4 * w + ng:].astype(BF16)
    w_om = w_out[0:w].astype(BF16)
    w_on = w_out[w:].astype(BF16)
    w_r = jnp.pad(w_router, ((0, 0), (0, LANES - N_EXPERTS))).astype(BF16)
    return dict(w_qt=w_qt, w_k=w_k, w_vt=w_vt, w_o=w_o, w_g=w_g, b_g=b_g, w_n=w_n, w_om=w_om, w_on=w_on, w_r=w_r,
                g_mlstm=g_mlstm.reshape(1, w).astype(F32), bias_table=_natten_bias_table(na_rel_bias))


def _layer(x, mod, pw, g_pre_mix, g_post_mix, g_pre_ffn, g_post_ffn, w_eg, w_eu, w_ed):
    bsz, seq, d = x.shape
    n = bsz * seq
    x2d = x.reshape(n, d)
    row = lambda g: g.reshape(1, d).astype(F32)
    qt, k, vt, o, gates, na = _inproj(x2d, mod, row(g_pre_mix), pw["w_qt"], pw["w_k"], pw["w_vt"], pw["w_o"],
                                      pw["w_g"], pw["w_n"], seq)
    grows, gcols = _gateprep(gates, pw["b_g"])
    hf, hb = _mlstm(k.reshape(bsz, seq, -1), qt, vt, grows, gcols.reshape(bsz, seq, -1))
    hn = _natten(na.reshape(bsz, seq, -1), pw["bias_table"])
    x1, x2aug, aff = _postmix(hf.reshape(n, -1), hb.reshape(n, -1), o, hn.reshape(n, -1), x2d, mod,
                              pw["g_mlstm"], pw["w_om"], pw["w_on"], row(g_post_mix), row(g_pre_ffn),
                              pw["w_r"], seq)
    cap = EC_CAPACITY_FACTOR * n // N_EXPERTS
    pos, post, tb = _select(aff, cap)
    xe = _dispatch(x2aug, post, tb, cap)
    ye = _experts(xe, w_eg, w_eu, w_ed, cap)
    y = _combine(tb, pos, x1, mod, row(g_post_ffn), ye, cap, seq)
    return y.reshape(bsz, seq, d)


def kernel(x_prompt, x_sample, c_prompt, c_sample, w_ada, b_ada, g_pre_mix, g_post_mix, w_in, b_gates, g_mlstm, na_rel_bias, w_out, g_pre_ffn, g_post_ffn, w_router, w_expert_gate, w_expert_up, w_expert_down):
    depth = w_ada.shape[0]
    nb = x_prompt.shape[0]
    xs = [x_prompt, x_sample]
    cs = jnp.concatenate([c_prompt, c_sample], axis=0)
    for l in range(depth):
        mod = _mod(cs, w_ada[l], b_ada[l])
        pw = _prep_weights(w_in[l], b_gates[l], g_mlstm[l], na_rel_bias[l], w_out[l], w_router[l])
        mods = [mod[:nb], mod[nb:]]
        xs = [_layer(x, m, pw, g_pre_mix[l], g_post_mix[l], g_pre_ffn[l], g_post_ffn[l],
                     w_expert_gate[l], w_expert_up[l], w_expert_down[l]) for x, m in zip(xs, mods)]
    return (xs[0], xs[1])
```

```python
import functools
import math

import jax
import jax.numpy as jnp
from jax import lax
from jax.experimental import pallas as pl
from jax.experimental.pallas import tpu as pltpu

F32 = jnp.float32
BF16 = jnp.bfloat16

LANES = 128
SUBLANES = 8
BF16_ROWS = 16
VMEM_LIMIT = 56 * 1024 * 1024

RMS_EPS = 1e-6
N_MOD = 6
MLSTM_HEADS = 4
MLSTM_HEAD_DIM = 128
MLSTM_CHUNK = 128
MLSTM_WIDTH = MLSTM_HEADS * MLSTM_HEAD_DIM
NA_HEAD_DIM = 64
NA_HEADS = 8
NA_WIDTH = NA_HEADS * NA_HEAD_DIM
NA_MAX_KH = 8
NA_KW = 16
GRID_W = 64
N_EXPERTS = 16
EC_CAPACITY_FACTOR = 2
NEG_BIG = -1e30


def _cparams(*sem):
    return pltpu.CompilerParams(dimension_semantics=sem, vmem_limit_bytes=VMEM_LIMIT)


def _rms(x):
    return x * lax.rsqrt(jnp.mean(x * x, axis=-1, keepdims=True) + RMS_EPS)


def _mod_kernel(c_ref, w_ref, b_ref, o_ref):
    c = c_ref[...]
    a = c * jax.nn.sigmoid(c)
    o_ref[...] = jnp.dot(a, w_ref[...], preferred_element_type=F32,
                         precision=lax.Precision.HIGHEST) + b_ref[...]


def _mod(c, w_ada, b_ada):
    bsz, d = c.shape
    n = w_ada.shape[1]
    bn = 1024
    out = pl.pallas_call(
        _mod_kernel,
        grid=(n // bn,),
        in_specs=[pl.BlockSpec((bsz, d), lambda j: (0, 0)),
                  pl.BlockSpec((d, bn), lambda j: (0, j)),
                  pl.BlockSpec((1, bn), lambda j: (0, j))],
        out_specs=pl.BlockSpec((bsz, bn), lambda j: (0, j)),
        out_shape=jax.ShapeDtypeStruct((bsz, n), F32),
        compiler_params=_cparams("arbitrary"),
        name="mod",
    )(c, w_ada, b_ada.reshape(1, n))
    return out.reshape(bsz, N_MOD, d)


def _inproj_kernel(x_ref, mod_ref, g_ref, wqt_ref, wk_ref, wvt_ref, wo_ref, wg_ref, wn_ref,
                   qt_ref, k_ref, vt_ref, o_ref, gates_ref, na_ref):
    x = x_ref[...]
    shift = mod_ref[0, 0:1, :]
    scale = mod_ref[0, 1:2, :]
    h = (_rms(x) * g_ref[...]) * (1.0 + scale) + shift
    hb = h.astype(BF16)
    nt = (((1,), (1,)), ((), ()))
    qt_ref[...] = lax.dot_general(wqt_ref[...], hb, nt, preferred_element_type=F32).astype(BF16)
    vt_ref[...] = lax.dot_general(wvt_ref[...], hb, nt, preferred_element_type=F32).astype(BF16)
    k_ref[...] = jnp.dot(hb, wk_ref[...], preferred_element_type=F32).astype(BF16)
    o_ref[...] = jnp.dot(hb, wo_ref[...], preferred_element_type=F32).astype(BF16)
    gates_ref[...] = jnp.dot(hb, wg_ref[...], preferred_element_type=F32)
    na_ref[...] = jnp.dot(hb, wn_ref[...], preferred_element_type=F32).astype(BF16)


def _inproj(x2d, mod, g_pre, w_qt, w_k, w_vt, w_o, w_g, w_n, seq, tm=512):
    n, d = x2d.shape
    tiles_per_b = seq // tm
    full = lambda a: pl.BlockSpec(a.shape, lambda i: (0, 0))
    tok = lambda w: pl.BlockSpec((tm, w.shape[1]), lambda i: (i, 0))
    feat = lambda w: pl.BlockSpec((w.shape[0], tm), lambda i: (0, i))
    return pl.pallas_call(
        _inproj_kernel,
        grid=(n // tm,),
        in_specs=[pl.BlockSpec((tm, d), lambda i: (i, 0)),
                  pl.BlockSpec((1, N_MOD, d), lambda i: (i // tiles_per_b, 0, 0)),
                  full(g_pre), full(w_qt), full(w_k), full(w_vt), full(w_o), full(w_g), full(w_n)],
        out_specs=[feat(w_qt), tok(w_k), feat(w_vt), tok(w_o), tok(w_g), tok(w_n)],
        out_shape=[jax.ShapeDtypeStruct((w_qt.shape[0], n), BF16),
                   jax.ShapeDtypeStruct((n, w_k.shape[1]), BF16),
                   jax.ShapeDtypeStruct((w_vt.shape[0], n), BF16),
                   jax.ShapeDtypeStruct((n, w_o.shape[1]), BF16),
                   jax.ShapeDtypeStruct((n, w_g.shape[1]), F32),
                   jax.ShapeDtypeStruct((n, w_n.shape[1]), BF16)],
        compiler_params=_cparams("arbitrary"),
        name="inproj",
    )(x2d, mod, g_pre, w_qt, w_k, w_vt, w_o, w_g, w_n)


N_CHAINS = 2 * MLSTM_HEADS
GROW_R, GROW_CM, GROW_B, GROW_G, GROW_RMAX = (i * N_CHAINS for i in range(5))
GROWS = 5 * N_CHAINS
GATEPREP_CHUNKS = 8
MLSTM_LOOKAHEAD = 4


def _lane_scan_max(x, lane, reverse):
    sh = 1
    while sh < LANES:
        if reverse:
            x = jnp.maximum(x, jnp.where(lane < LANES - sh, pltpu.roll(x, LANES - sh, 1), NEG_BIG))
        else:
            x = jnp.maximum(x, jnp.where(lane >= sh, pltpu.roll(x, sh, 1), NEG_BIG))
        sh *= 2
    return x


def _gateprep_kernel(g_ref, bias_ref, rows_ref, cols_ref):
    L = MLSTM_CHUNK
    row = lax.broadcasted_iota(jnp.int32, (L, L), 0)
    col = lax.broadcasted_iota(jnp.int32, (L, L), 1)
    tri = jnp.concatenate([jnp.where(row <= col, 1.0, 0.0), jnp.where(row >= col, 1.0, 0.0)], axis=1)
    crow = lax.broadcasted_iota(jnp.int32, (N_CHAINS, L), 0)
    lane = lax.broadcasted_iota(jnp.int32, (N_CHAINS, L), 1)
    is_fwd = crow < MLSTM_HEADS
    for t in range(GATEPREP_CHUNKS):
        gt = (g_ref[t * L:(t + 1) * L, :] + bias_ref[...]).T
        bb = jnp.dot(jax.nn.log_sigmoid(gt[N_CHAINS:2 * N_CHAINS]), tri, preferred_element_type=F32,
                     precision=lax.Precision.HIGHEST)
        b_f, b_b = bb[:, 0:L], bb[:, L:2 * L]
        b = jnp.where(is_fwd, b_f, b_b)
        r = gt[0:N_CHAINS] - b
        g = jnp.where(is_fwd[:, 0:1], b_f[:, L - 1:L], b_b[:, 0:1])
        cm = jnp.where(is_fwd, _lane_scan_max(r, lane, False), _lane_scan_max(r, lane, True))
        rmax = jnp.max(r, axis=1, keepdims=True)
        rows_ref[t] = jnp.concatenate([r, cm, b, jnp.broadcast_to(g, (N_CHAINS, L)),
                                       jnp.broadcast_to(rmax, (N_CHAINS, L))], axis=0)
        cols_ref[t * L:(t + 1) * L, :] = jnp.concatenate([r, jnp.zeros((L - N_CHAINS, L), F32)], axis=0).T


def _gateprep(gates, gate_bias):
    n = gates.shape[0]
    L = MLSTM_CHUNK
    tn = GATEPREP_CHUNKS * L
    return pl.pallas_call(
        _gateprep_kernel,
        grid=(n // tn,),
        in_specs=[pl.BlockSpec((tn, LANES), lambda i: (i, 0)),
                  pl.BlockSpec((1, LANES), lambda i: (0, 0))],
        out_specs=[pl.BlockSpec((GATEPREP_CHUNKS, GROWS, LANES), lambda i: (i, 0, 0)),
                   pl.BlockSpec((tn, LANES), lambda i: (i, 0))],
        out_shape=[jax.ShapeDtypeStruct((n // L, GROWS, LANES), F32),
                   jax.ShapeDtypeStruct((n, LANES), F32)],
        compiler_params=_cparams("arbitrary"),
        name="gateprep",
    )(gates, gate_bias)


def _mlstm_scores(k, qt, n_ref, ci):
    n_st = n_ref[ci]
    n_hi = n_st.astype(BF16).astype(F32)
    n_lo = (n_st - n_hi).astype(BF16).astype(F32)
    nn = jnp.concatenate([n_hi, n_lo, jnp.zeros((BF16_ROWS - 2, MLSTM_HEAD_DIM), F32)], axis=0).astype(BF16)
    return jnp.dot(k, qt, preferred_element_type=F32), jnp.dot(nn, qt, preferred_element_type=F32)


def _mlstm_chain(kq, qn2, k, qt, vt, rows, cols, mask_sj, c_ref, n_ref, m_ref, ci):
    L = MLSTM_CHUNK
    scale = MLSTM_HEAD_DIM ** -0.5
    r_row = rows[GROW_R + ci:GROW_R + ci + 1]
    cm_row = rows[GROW_CM + ci:GROW_CM + ci + 1]
    b_row = rows[GROW_B + ci:GROW_B + ci + 1]
    g = rows[GROW_G + ci:GROW_G + ci + 1, 0:1]
    rmax = rows[GROW_RMAX + ci:GROW_RMAX + ci + 1, 0:1]
    r_colb = jnp.broadcast_to(cols[:, ci:ci + 1], (L, L))
    c_st = c_ref[ci]
    n_st = n_ref[ci]
    m_st = m_ref[ci][:, 0:1]
    mm = jnp.maximum(cm_row, m_st)
    pt = jnp.where(mask_sj, jnp.exp(r_colb - mm), 0.0)
    st = kq * scale * pt
    s_int = jnp.exp(m_st - mm)
    den = jnp.sum(st, axis=0, keepdims=True) + s_int * (qn2[0:1] + qn2[1:2])
    inv = 1.0 / jnp.maximum(jnp.abs(den), jnp.exp(-(b_row + mm)))
    lhs = jnp.concatenate([vt, c_st.astype(BF16)], axis=1)
    rhs = jnp.concatenate([st.astype(BF16), (qt.astype(F32) * s_int).astype(BF16)], axis=0)
    h = (jnp.dot(lhs, rhs, preferred_element_type=F32) * inv).astype(BF16).T

    w_row = jnp.exp(r_row - rmax)
    wl = jnp.concatenate([(vt.astype(F32) * w_row).astype(BF16),
                          jnp.broadcast_to(w_row, (BF16_ROWS, L)).astype(BF16)], axis=0)
    loc = jnp.dot(wl, k, preferred_element_type=F32) * scale
    m_loc = g + rmax
    m_new = jnp.maximum(g + m_st, m_loc)
    s_old = jnp.exp(g + m_st - m_new)
    s_loc = jnp.exp(m_loc - m_new)
    c_ref[ci] = s_old * c_st + s_loc * loc[0:MLSTM_HEAD_DIM]
    n_ref[ci] = s_old * n_st + s_loc * loc[MLSTM_HEAD_DIM:MLSTM_HEAD_DIM + 1]
    m_ref[ci] = jnp.broadcast_to(m_new, (1, LANES))
    return h


def _mlstm_kernel(kf_ref, kb_ref, qtf_ref, qtb_ref, vtf_ref, vtb_ref, rf_ref, rb_ref, cf_ref, cb_ref,
                  hf_ref, hb_ref, c_ref, n_ref, m_ref):
    L = MLSTM_CHUNK

    @pl.when(pl.program_id(1) == 0)
    def _():
        c_ref[...] = jnp.zeros_like(c_ref)
        n_ref[...] = jnp.zeros_like(n_ref)
        m_ref[...] = jnp.zeros_like(m_ref)

    row = lax.broadcasted_iota(jnp.int32, (L, L), 0)
    col = lax.broadcasted_iota(jnp.int32, (L, L), 1)
    dirs = ((kf_ref, qtf_ref, vtf_ref, rf_ref, cf_ref, hf_ref, row <= col),
            (kb_ref, qtb_ref, vtb_ref, rb_ref, cb_ref, hb_ref, row >= col))

    def operands(ci):
        d, hh = divmod(ci, MLSTM_HEADS)
        sl = slice(hh * MLSTM_HEAD_DIM, (hh + 1) * MLSTM_HEAD_DIM)
        return dirs[d], sl

    def scores(ci):
        (k_ref, qt_ref, _, _, _, _, _), sl = operands(ci)
        return _mlstm_scores(k_ref[0, :, sl], qt_ref[sl, :], n_ref, ci)

    pre = [scores(ci) for ci in range(MLSTM_LOOKAHEAD)]
    for ci in range(N_CHAINS):
        if ci + MLSTM_LOOKAHEAD < N_CHAINS:
            pre.append(scores(ci + MLSTM_LOOKAHEAD))
        kq, qn2 = pre.pop(0)
        (k_ref, qt_ref, vt_ref, r_ref, cl_ref, o_ref, mask), sl = operands(ci)
        o_ref[0, :, sl] = _mlstm_chain(kq, qn2, k_ref[0, :, sl], qt_ref[sl, :], vt_ref[sl, :],
                                       r_ref[0], cl_ref[0], mask, c_ref, n_ref, m_ref, ci)


def _mlstm(k, qt, vt, grows, gcols):
    bsz, seq, _ = k.shape
    L = MLSTM_CHUNK
    nc = seq // L
    fwd = lambda b, c: (b, c, 0)
    bwd = lambda b, c: (b, nc - 1 - c, 0)
    fwd_t = lambda b, c: (0, b * nc + c)
    bwd_t = lambda b, c: (0, b * nc + nc - 1 - c)
    fwd_r = lambda b, c: (b * nc + c, 0, 0)
    bwd_r = lambda b, c: (b * nc + nc - 1 - c, 0, 0)
    return pl.pallas_call(
        _mlstm_kernel,
        grid=(bsz, nc),
        in_specs=[pl.BlockSpec((1, L, MLSTM_WIDTH), fwd), pl.BlockSpec((1, L, MLSTM_WIDTH), bwd),
                  pl.BlockSpec((MLSTM_WIDTH, L), fwd_t), pl.BlockSpec((MLSTM_WIDTH, L), bwd_t),
                  pl.BlockSpec((MLSTM_WIDTH, L), fwd_t), pl.BlockSpec((MLSTM_WIDTH, L), bwd_t),
                  pl.BlockSpec((1, GROWS, LANES), fwd_r), pl.BlockSpec((1, GROWS, LANES), bwd_r),
                  pl.BlockSpec((1, L, LANES), fwd), pl.BlockSpec((1, L, LANES), bwd)],
        out_specs=[pl.BlockSpec((1, L, MLSTM_WIDTH), fwd),
                   pl.BlockSpec((1, L, MLSTM_WIDTH), bwd)],
        out_shape=[jax.ShapeDtypeStruct((bsz, seq, MLSTM_WIDTH), BF16),
                   jax.ShapeDtypeStruct((bsz, seq, MLSTM_WIDTH), BF16)],
        scratch_shapes=[pltpu.VMEM((N_CHAINS, MLSTM_HEAD_DIM, MLSTM_HEAD_DIM), F32),
                        pltpu.VMEM((N_CHAINS, 1, MLSTM_HEAD_DIM), F32),
                        pltpu.VMEM((N_CHAINS, 1, LANES), F32)],
        compiler_params=_cparams("arbitrary", "arbitrary"),
        name="mlstm",
    )(k, k, qt, qt, vt, vt, grows, grows, gcols, gcols)


NA_QROWS = 8
NA_KROWS = 2 * NA_QROWS
NA_KBLK = 4
NA_NKBLK = NA_KROWS // NA_KBLK
NA_CODE_SECOND = 2 * NA_MAX_KH - 2
NA_CODE_FIRST = NA_CODE_SECOND + NA_MAX_KH
NA_CODE_NONE = NA_CODE_FIRST + NA_MAX_KH
NA_NCODES = NA_CODE_NONE + 1


NA_LIVE_PAIRS = NA_MAX_KH // 2 + 1


def _natten_kernel(q_ref, k0_ref, k1_ref, k2_ref, k3_ref, v0_ref, v1_ref, v2_ref, v3_ref, bias_ref, o_ref,
                   st0_ref, st1_ref, pt0_ref, pt1_ref, *, rows):
    i = pl.program_id(1)
    half = NA_MAX_KH // 2
    rq0 = i * NA_QROWS
    w0 = jnp.clip(rq0 - half, 0, rows - NA_KROWS)
    nqp = NA_QROWS // 2
    nkp = NA_KROWS // 2
    tq = NA_QROWS * GRID_W

    def window_start(r):
        return jnp.clip(r - half, 0, rows - NA_MAX_KH)

    def code(rq, kp):
        r = rq0 + rq
        r0 = window_start(r)
        kr = w0 + 2 * kp
        d0 = kr - r + (NA_MAX_KH - 1)
        va = (kr >= r0) & (kr < r0 + NA_MAX_KH)
        vb = (kr + 1 >= r0) & (kr + 1 < r0 + NA_MAX_KH)
        return jnp.where(va & vb, d0,
                         jnp.where(vb, NA_CODE_SECOND + d0 + 1,
                                   jnp.where(va, NA_CODE_FIRST + d0 - (NA_MAX_KH - 1), NA_CODE_NONE)))

    first = [jnp.minimum((window_start(rq0 + 2 * qi) - w0) // 2, nkp - NA_LIVE_PAIRS) for qi in range(nqp)]
    codes = [[(code(2 * qi, first[qi] + t), code(2 * qi + 1, first[qi] + t)) for t in range(NA_LIVE_PAIRS)]
             for qi in range(nqp)]
    lane_t = lax.broadcasted_iota(jnp.int32, (LANES, LANES), 1)
    lane_q = lax.broadcasted_iota(jnp.int32, (tq, LANES), 1)
    sub_o = lax.broadcasted_iota(jnp.int32, (LANES, tq), 0)
    scale = jnp.asarray(NA_HEAD_DIM ** -0.5, BF16)
    k_refs = (k0_ref, k1_ref, k2_ref, k3_ref)
    v_refs = (v0_ref, v1_ref, v2_ref, v3_ref)
    st_refs = (st0_ref, st1_ref)
    pt_refs = (pt0_ref, pt1_ref)
    pt0_ref[...] = jnp.zeros_like(pt0_ref)
    pt1_ref[...] = jnp.zeros_like(pt1_ref)

    def lanes_of(hp):
        return slice(hp * LANES, (hp + 1) * LANES)

    def scores(h):
        hp, sb = divmod(h, 2)
        lsl = lanes_of(hp)
        qp = q_ref[0, :, lsl] * scale
        own = (lane_q >= sb * NA_HEAD_DIM) & (lane_q < (sb + 1) * NA_HEAD_DIM)
        qm = jnp.where(own, qp, jnp.zeros_like(qp))
        k_all = jnp.concatenate([r[0, :, lsl] for r in k_refs], axis=0)
        st_refs[sb][...] = lax.dot_general(k_all, qm, (((1,), (1,)), ((), ())), preferred_element_type=F32)

    def attend(h):
        hp, sb = divmod(h, 2)
        st_ref, pt_ref = st_refs[sb], pt_refs[sb]
        invs = []
        for qi in range(nqp):
            qsl = slice(qi * LANES, (qi + 1) * LANES)
            offs = [pl.multiple_of((first[qi] + t) * LANES, LANES) for t in range(NA_LIVE_PAIRS)]
            tiles = []
            for t in range(NA_LIVE_PAIRS):
                ca, cb = codes[qi][t]
                bias = jnp.where(lane_t < GRID_W, bias_ref[h, ca], bias_ref[h, cb])
                tiles.append(st_ref[pl.ds(offs[t], LANES), qsl] + bias)
            m = functools.reduce(jnp.maximum, tiles)
            m = jnp.max(m, axis=0, keepdims=True)
            es = [jnp.exp(t - m) for t in tiles]
            l = jnp.sum(functools.reduce(jnp.add, es), axis=0, keepdims=True)
            invs.append(1.0 / l)
            for t in range(NA_LIVE_PAIRS):
                pt_ref[pl.ds(offs[t], LANES), qsl] = es[t].astype(BF16)
        vt = jnp.concatenate([r[0, :, lanes_of(hp)] for r in v_refs], axis=0).T
        return jnp.dot(vt, pt_ref[...], preferred_element_type=F32) * jnp.concatenate(invs, axis=1)

    scores(0)
    outs = []
    for h in range(NA_HEADS):
        if h + 1 < NA_HEADS:
            scores(h + 1)
        outs.append(attend(h))
        if h % 2 == 1:
            ot = jnp.where(sub_o < NA_HEAD_DIM, outs[0], outs[1])
            o_ref[0, :, lanes_of(h // 2)] = ot.T.astype(o_ref.dtype)
            outs = []


def _natten_bias_table(rel_bias):
    cols = jnp.arange(GRID_W)
    col_start = jnp.clip(cols - NA_KW // 2, 0, GRID_W - NA_KW)
    kcol = jnp.arange(GRID_W)
    dc = kcol[:, None] - cols[None, :] + (NA_KW - 1)
    inwin = (kcol[:, None] >= col_start[None, :]) & (kcol[:, None] < col_start[None, :] + NA_KW)
    tab = rel_bias.astype(F32)[:, :, jnp.clip(dc, 0, 2 * NA_KW - 2)]
    tab = jnp.where(inwin[None, None], tab, NEG_BIG)
    neg = jnp.full_like(tab[:, :NA_MAX_KH], NEG_BIG)
    both = jnp.concatenate([tab[:, :-1], tab[:, 1:]], axis=2)
    second = jnp.concatenate([neg, tab[:, :NA_MAX_KH]], axis=2)
    first = jnp.concatenate([tab[:, NA_MAX_KH - 1:], neg], axis=2)
    none = jnp.concatenate([neg[:, :1], neg[:, :1]], axis=2)
    t = jnp.concatenate([both, second, first, none], axis=1)
    return jnp.concatenate([t, t], axis=-1)


def _natten(na, bias_table):
    bsz, seq, _ = na.shape
    rows = seq // GRID_W
    tq = NA_QROWS * GRID_W
    tk = NA_KBLK * GRID_W
    nkb = seq // tk

    def kv_spec(j, col):
        return pl.BlockSpec((1, tk, NA_WIDTH),
                            lambda b, i: (b, jnp.clip(2 * i - 1, 0, nkb - NA_NKBLK) + j, col))

    return pl.pallas_call(
        functools.partial(_natten_kernel, rows=rows),
        grid=(bsz, rows // NA_QROWS),
        in_specs=[pl.BlockSpec((1, tq, NA_WIDTH), lambda b, i: (b, i, 0))]
                 + [kv_spec(j, 1) for j in range(NA_NKBLK)]
                 + [kv_spec(j, 2) for j in range(NA_NKBLK)]
                 + [pl.BlockSpec(bias_table.shape, lambda b, i: (0, 0, 0, 0), pipeline_mode=pl.Buffered(1))],
        out_specs=pl.BlockSpec((1, tq, NA_WIDTH), lambda b, i: (b, i, 0)),
        out_shape=jax.ShapeDtypeStruct((bsz, seq, NA_WIDTH), BF16),
        scratch_shapes=[pltpu.VMEM((NA_KROWS * GRID_W, tq), F32), pltpu.VMEM((NA_KROWS * GRID_W, tq), F32),
                        pltpu.VMEM((NA_KROWS * GRID_W, tq), BF16), pltpu.VMEM((NA_KROWS * GRID_W, tq), BF16)],
        compiler_params=_cparams("arbitrary", "arbitrary"),
        name="natten",
    )(na, *([na] * (2 * NA_NKBLK)), bias_table)


AUG = 2 * LANES
GATE_LO_SHIFT = N_EXPERTS


def _postmix_kernel(hf_ref, hb_ref, o_ref, hn_ref, x_ref, mod_ref, gm_ref, wm_ref, wn_ref,
                    gpost_ref, gffn_ref, wr_ref, x1_ref, x2_ref, aff_ref):
    d = x_ref.shape[-1]
    s = hf_ref[...].astype(F32) + hb_ref[...].astype(F32)
    og = jax.nn.sigmoid(o_ref[...].astype(F32))
    heads = []
    for hh in range(MLSTM_HEADS):
        sl = slice(hh * MLSTM_HEAD_DIM, (hh + 1) * MLSTM_HEAD_DIM)
        heads.append(_rms(s[:, sl]) * gm_ref[:, sl] * og[:, sl])
    hm = jnp.concatenate(heads, axis=-1).astype(BF16)
    mix = (jnp.dot(hm, wm_ref[...], preferred_element_type=F32)
           + jnp.dot(hn_ref[...], wn_ref[...], preferred_element_type=F32))
    gate1 = mod_ref[0, 2:3, :]
    shift2 = mod_ref[0, 3:4, :]
    scale2 = mod_ref[0, 4:5, :]
    x1 = x_ref[...] + gate1 * (_rms(mix) * gpost_ref[...])
    x1_ref[...] = x1
    h2 = (_rms(x1) * gffn_ref[...]) * (1.0 + scale2) + shift2
    h2b = h2.astype(BF16)
    x2_ref[:, 0:d] = h2b
    logits = jnp.dot(h2b, wr_ref[...], preferred_element_type=F32)
    lane = lax.broadcasted_iota(jnp.int32, logits.shape, 1)
    logits = jnp.where(lane < N_EXPERTS, logits, NEG_BIG)
    ex = jnp.exp(logits - jnp.max(logits, axis=-1, keepdims=True))
    aff = ex / jnp.sum(ex, axis=-1, keepdims=True)
    aff_ref[...] = aff
    hi = aff.astype(BF16).astype(F32)
    lo = (aff - hi).astype(BF16).astype(F32)
    x2_ref[:, d:d + LANES] = (hi + pltpu.roll(lo, GATE_LO_SHIFT, 1)).astype(BF16)
    x2_ref[:, d + LANES:d + AUG] = jnp.zeros((x2_ref.shape[0], AUG - LANES), BF16)


def _postmix(hf, hb, o, hn, x2d, mod, g_mlstm, w_om, w_on, g_post, g_ffn, w_r, seq, tm=512):
    n, d = x2d.shape
    tiles_per_b = seq // tm
    full = lambda a: pl.BlockSpec(a.shape, lambda i: (0, 0))
    tok = lambda w: pl.BlockSpec((tm, w), lambda i: (i, 0))
    return pl.pallas_call(
        _postmix_kernel,
        grid=(n // tm,),
        in_specs=[tok(MLSTM_WIDTH), tok(MLSTM_WIDTH), tok(MLSTM_WIDTH), tok(NA_WIDTH), tok(d),
                  pl.BlockSpec((1, N_MOD, d), lambda i: (i // tiles_per_b, 0, 0)),
                  full(g_mlstm), full(w_om), full(w_on), full(g_post), full(g_ffn), full(w_r)],
        out_specs=[tok(d), tok(d + AUG), tok(LANES)],
        out_shape=[jax.ShapeDtypeStruct((n, d), F32),
                   jax.ShapeDtypeStruct((n, d + AUG), BF16),
                   jax.ShapeDtypeStruct((n, LANES), F32)],
        compiler_params=_cparams("arbitrary"),
        name="postmix",
    )(hf, hb, o, hn, x2d, mod, g_mlstm, w_om, w_on, g_post, g_ffn, w_r)


TOK_TILE = 128
SEL_STEP = 1024
UNSELECTED = -1e6


def _select_kernel(afft_ref, aff_ref, pos_ref, post_ref, tb_ref, thr_ref, need_ref, base_ref, *, cap):
    step = pl.program_id(0)
    ne = afft_ref.shape[0]

    @pl.when(step == 0)
    def _():
        bits = pltpu.bitcast(afft_ref[...], jnp.int32)

        def body(it, lo):
            cand = lo | lax.shift_left(jnp.int32(1), 30 - it)
            cnt = jnp.sum(jnp.where(bits >= cand, 1.0, 0.0), axis=1, keepdims=True)
            return jnp.where(cnt >= cap, cand, lo)

        thr_bits = lax.fori_loop(0, 31, body, jnp.zeros((ne, 1), jnp.int32))
        cnt_gt = jnp.sum(jnp.where(bits > thr_bits, 1.0, 0.0), axis=1, keepdims=True)
        thr_col = pltpu.bitcast(thr_bits, F32)
        need_col = cap - cnt_gt
        sub = lax.broadcasted_iota(jnp.int32, (ne, LANES), 0)
        ln = lax.broadcasted_iota(jnp.int32, (ne, LANES), 1)
        diag = sub == ln
        thr_ref[...] = jnp.sum(jnp.where(diag, thr_col, 0.0), axis=0, keepdims=True)
        need_ref[...] = jnp.sum(jnp.where(diag, need_col, 0.0), axis=0, keepdims=True)
        base_ref[...] = jnp.zeros_like(base_ref)

    thr = thr_ref[...]
    need = need_ref[...]
    lane = lax.broadcasted_iota(jnp.int32, (TOK_TILE, LANES), 1)
    valid = lane < ne
    row = lax.broadcasted_iota(jnp.int32, (TOK_TILE, TOK_TILE), 0)
    col = lax.broadcasted_iota(jnp.int32, (TOK_TILE, TOK_TILE), 1)
    strict_lower = (col < row).astype(BF16)
    base_gt = base_ref[0:1, :]
    base_eq = base_ref[1:2, :]
    tb_rows = []
    for t in range(SEL_STEP // TOK_TILE):
        a = aff_ref[t * TOK_TILE:(t + 1) * TOK_TILE, :]
        gt = (a > thr) & valid
        eq = (a == thr) & valid
        gtf = jnp.where(gt, 1.0, 0.0)
        eqf = jnp.where(eq, 1.0, 0.0)
        cg = jnp.dot(strict_lower, gtf.astype(BF16), preferred_element_type=F32)
        ce = jnp.dot(strict_lower, eqf.astype(BF16), preferred_element_type=F32)
        eq_rank = base_eq + ce
        sel = gt | (eq & (eq_rank < need))
        pos = base_gt + cg + jnp.minimum(eq_rank, need)
        posm = jnp.where(sel, pos, UNSELECTED)
        pos_ref[t * TOK_TILE:(t + 1) * TOK_TILE, :] = posm
        post_ref[:, t * TOK_TILE:(t + 1) * TOK_TILE] = posm.T[0:ne, :]
        tb_rows.append(base_gt + jnp.minimum(base_eq, need))
        base_gt = base_gt + jnp.sum(gtf, axis=0, keepdims=True)
        base_eq = base_eq + jnp.sum(eqf, axis=0, keepdims=True)
    tb_ref[...] = jnp.concatenate(tb_rows, axis=0).astype(jnp.int32)
    base_ref[0:1, :] = base_gt
    base_ref[1:2, :] = base_eq


def _select(aff, cap):
    n = aff.shape[0]
    afft = aff[:, :N_EXPERTS].T
    nsteps = n // SEL_STEP
    tiles_per_step = SEL_STEP // TOK_TILE
    return pl.pallas_call(
        functools.partial(_select_kernel, cap=cap),
        grid=(nsteps,),
        in_specs=[pl.BlockSpec((N_EXPERTS, n), lambda i: (0, 0)),
                  pl.BlockSpec((SEL_STEP, LANES), lambda i: (i, 0))],
        out_specs=[pl.BlockSpec((SEL_STEP, LANES), lambda i: (i, 0)),
                   pl.BlockSpec((N_EXPERTS, SEL_STEP), lambda i: (0, i)),
                   pl.BlockSpec((tiles_per_step, LANES), lambda i: (i, 0))],
        out_shape=[jax.ShapeDtypeStruct((n, LANES), F32),
                   jax.ShapeDtypeStruct((N_EXPERTS, n), F32),
                   jax.ShapeDtypeStruct((n // TOK_TILE, LANES), jnp.int32)],
        scratch_shapes=[pltpu.VMEM((1, LANES), F32), pltpu.VMEM((1, LANES), F32),
                        pltpu.VMEM((SUBLANES, LANES), F32)],
        compiler_params=_cparams("arbitrary"),
        name="select",
    )(afft, aff)


DISP_TILE = 2 * TOK_TILE
DISP_FAST = 80
DISP_SLOW = SUBLANES + DISP_TILE + SUBLANES
CARRY = SUBLANES
XE_PAD = 2 * DISP_TILE
U32 = jnp.uint32
HI16 = 0xFFFF0000


def _pack_halves(x):
    w = x.shape[1] // 2
    lo = lax.bitcast_convert_type(x[:, 0:w], U32)
    hi = lax.bitcast_convert_type(x[:, w:], U32)
    return lax.shift_right_logical(lo, U32(16)) | (hi & U32(HI16))


def _unpack_halves(p):
    lo = lax.bitcast_convert_type(lax.shift_left(p, U32(16)), F32)
    hi = lax.bitcast_convert_type(p & U32(HI16), F32)
    return lo, hi


def _dispatch_kernel(tb_ref, x_ref, post_ref, xe_ref, stage_ref, carry_ref, sem, flag_ref, *, cap, ntiles):
    i = pl.program_id(0)
    slot = lax.rem(i, 2)
    ne = N_EXPERTS

    @pl.when(i == 0)
    def _():
        carry_ref[...] = jnp.zeros_like(carry_ref)
        stage_ref[1, 0:XE_PAD, :] = jnp.zeros((XE_PAD, stage_ref.shape[-1]), U32)
        pads = [pltpu.make_async_copy(stage_ref.at[1, pl.ds(0, XE_PAD)], xe_ref.at[e, pl.ds(cap, XE_PAD)],
                                      sem.at[1]) for e in range(ne)]
        for cp in pads:
            cp.start()
        for cp in pads:
            cp.wait()

    fill = [tb_ref[i * ne + e] for e in range(ne)]
    nxt = [jnp.where(i + 1 < ntiles, tb_ref[jnp.minimum(i + 1, ntiles - 1) * ne + e], cap) for e in range(ne)]
    al = [(f // CARRY) * CARRY for f in fill]
    need_max = functools.reduce(jnp.maximum, [nx - a for nx, a in zip(nxt, al)])
    slow = need_max > DISP_FAST

    def copies(s_rows, sl, als):
        return [pltpu.make_async_copy(stage_ref.at[sl, pl.ds(e * s_rows, s_rows)],
                                      xe_ref.at[e, pl.ds(pl.multiple_of(als[e], CARRY), s_rows)],
                                      sem.at[sl]) for e in range(ne)]

    def wait_all(sl):
        zeros = [0] * ne

        def w(s_rows):
            for cp in copies(s_rows, sl, zeros):
                cp.wait()
        lax.cond(flag_ref[sl] == 1, lambda: w(DISP_SLOW), lambda: w(DISP_FAST))

    def body(s_rows, group):
        xt = x_ref[...]
        cb = jnp.concatenate(_unpack_halves(carry_ref[...]), axis=1).astype(BF16)
        slot_of_tok = lax.broadcasted_iota(jnp.int32, (s_rows, DISP_TILE), 0).astype(F32)
        sub = lax.broadcasted_iota(jnp.int32, (s_rows, ne * CARRY), 0)
        ln = lax.broadcasted_iota(jnp.int32, (s_rows, ne * CARRY), 1)
        for e0 in range(0, ne, group):
            a_tok, a_car = [], []
            for e in range(e0, e0 + group):
                rel = post_ref[e:e + 1, :] - al[e].astype(F32)
                a_tok.append(jnp.where(slot_of_tok == rel, 1.0, 0.0))
                own = (ln >= e * CARRY) & (ln < e * CARRY + (fill[e] - al[e])) & (ln - e * CARRY == sub)
                a_car.append(jnp.where(own, 1.0, 0.0))
            staged = (jnp.dot(jnp.concatenate(a_tok, axis=0).astype(BF16), xt, preferred_element_type=F32)
                      + jnp.dot(jnp.concatenate(a_car, axis=0).astype(BF16), cb, preferred_element_type=F32))
            stage_ref[slot, e0 * s_rows:(e0 + group) * s_rows, :] = _pack_halves(staged)
        for e in range(ne):
            off = jnp.minimum((nxt[e] // CARRY) * CARRY - al[e], s_rows - CARRY)
            carry_ref[e * CARRY:(e + 1) * CARRY, :] = stage_ref[slot, pl.ds(pl.multiple_of(e * s_rows + off, CARRY), CARRY), :]

        @pl.when(i > 0)
        def _():
            wait_all(1 - slot)
        for cp in copies(s_rows, slot, al):
            cp.start()

    @pl.when(slow)
    def _():
        body(DISP_SLOW, 2)
        flag_ref[slot] = 1

    @pl.when(jnp.logical_not(slow))
    def _():
        body(DISP_FAST, ne)
        flag_ref[slot] = 0

    @pl.when(i == ntiles - 1)
    def _():
        wait_all(slot)


def _dispatch(x2aug, post, tb, cap):
    n, wd = x2aug.shape
    ntiles = n // DISP_TILE
    tb_flat = tb[::DISP_TILE // TOK_TILE, :N_EXPERTS].reshape(-1)
    return pl.pallas_call(
        functools.partial(_dispatch_kernel, cap=cap, ntiles=ntiles),
        grid_spec=pltpu.PrefetchScalarGridSpec(
            num_scalar_prefetch=1,
            grid=(ntiles,),
            in_specs=[pl.BlockSpec((DISP_TILE, wd), lambda i, tb: (i, 0)),
                      pl.BlockSpec((N_EXPERTS, DISP_TILE), lambda i, tb: (0, i))],
            out_specs=pl.BlockSpec(memory_space=pl.ANY),
            scratch_shapes=[pltpu.VMEM((2, N_EXPERTS * DISP_SLOW, wd // 2), U32),
                            pltpu.VMEM((N_EXPERTS * CARRY, wd // 2), U32),
                            pltpu.SemaphoreType.DMA((2,)),
                            pltpu.SMEM((2,), jnp.int32)]),
        out_shape=jax.ShapeDtypeStruct((N_EXPERTS, cap + XE_PAD, wd // 2), U32),
        compiler_params=_cparams("arbitrary"),
        name="dispatch",
    )(tb_flat, x2aug, post)


EXP_ROWS = 512


def _experts_kernel(xe_ref, wg_ref, wu_ref, wd_ref, ye_ref, wgb_ref, wub_ref, wdb_ref):
    e = pl.program_id(0)
    d = wg_ref.shape[1]

    @pl.when(pl.program_id(1) == 0)
    def _():
        wgb_ref[...] = wg_ref[0].astype(BF16)
        wub_ref[...] = wu_ref[0].astype(BF16)
        wdb_ref[...] = wd_ref[0].astype(BF16)

    lo, hi = _unpack_halves(xe_ref[0])
    w = lo.shape[1]
    x = jnp.concatenate([lo, hi[:, 0:d - w]], axis=1).astype(BF16)
    aug = hi[:, d - w:d - w + LANES]
    lane = lax.broadcasted_iota(jnp.int32, aug.shape, 1)
    mine = (lane == e) | (lane == e + GATE_LO_SHIFT)
    gate = jnp.sum(jnp.where(mine, aug, 0.0), axis=-1, keepdims=True)
    g = jnp.dot(x, wgb_ref[...], preferred_element_type=F32)
    u = jnp.dot(x, wub_ref[...], preferred_element_type=F32)
    hid = (g * jax.nn.sigmoid(g)) * u
    y = jnp.dot(hid.astype(BF16), wdb_ref[...], preferred_element_type=F32)
    ye_ref[0] = (y * gate).astype(ye_ref.dtype)


def _experts(xe, w_gate, w_up, w_down, cap):
    ne, _, wd = xe.shape
    d = w_gate.shape[1]
    f = w_gate.shape[2]
    wspec = lambda s: pl.BlockSpec((1,) + s, lambda e, j: (e, 0, 0))
    rows = min(EXP_ROWS, cap)
    return pl.pallas_call(
        _experts_kernel,
        grid=(ne, cap // rows),
        in_specs=[pl.BlockSpec((1, rows, wd), lambda e, j: (e, j, 0)),
                  wspec((d, f)), wspec((d, f)), wspec((f, d))],
        out_specs=pl.BlockSpec((1, rows, d), lambda e, j: (e, j, 0)),
        out_shape=jax.ShapeDtypeStruct((ne, cap, d), BF16),
        scratch_shapes=[pltpu.VMEM((d, f), BF16), pltpu.VMEM((d, f), BF16), pltpu.VMEM((f, d), BF16)],
        compiler_params=_cparams("arbitrary", "arbitrary"),
        name="experts",
    )(xe, w_gate, w_up, w_down)


COMB_TILE = 2 * TOK_TILE
COMB_FAST = 128
COMB_SLOW = 384
COMB_ALIGN = BF16_ROWS


def _combine_kernel(tb_ref, pos_ref, x1_ref, mod_ref, g_ref, ye_ref, y_ref, slab_ref, sem, *, cap, ntiles):
    i = pl.program_id(0)
    slot = lax.rem(i, 2)
    ne = N_EXPERTS

    def info(t):
        t = jnp.minimum(t, ntiles - 1)
        fill = [tb_ref[t * ne + e] for e in range(ne)]
        nxt = [jnp.where(t + 1 < ntiles, tb_ref[jnp.minimum(t + 1, ntiles - 1) * ne + e], cap) for e in range(ne)]
        st_f = [jnp.minimum((f // COMB_ALIGN) * COMB_ALIGN, cap - COMB_FAST) for f in fill]
        st_s = [jnp.minimum((f // COMB_ALIGN) * COMB_ALIGN, cap - COMB_SLOW) for f in fill]
        slow = functools.reduce(jnp.maximum, [nx - s for nx, s in zip(nxt, st_f)]) > COMB_FAST
        return st_f, st_s, slow

    def copies(s_rows, sl, starts):
        return [pltpu.make_async_copy(ye_ref.at[e, pl.ds(pl.multiple_of(starts[e], COMB_ALIGN), s_rows)],
                                      slab_ref.at[sl, pl.ds(e * s_rows, s_rows)],
                                      sem.at[sl]) for e in range(ne)]

    def fetch(t, sl):
        st_f, st_s, slow = info(t)

        @pl.when(slow)
        def _():
            for cp in copies(COMB_SLOW, sl, st_s):
                cp.start()

        @pl.when(jnp.logical_not(slow))
        def _():
            for cp in copies(COMB_FAST, sl, st_f):
                cp.start()

    @pl.when(i == 0)
    def _():
        fetch(i, slot)

    @pl.when(i + 1 < ntiles)
    def _():
        fetch(i + 1, 1 - slot)

    st_f, st_s, slow = info(i)
    lane = lax.broadcasted_iota(jnp.int32, (COMB_TILE, LANES), 1).astype(F32)
    pos = pos_ref[...]

    def finish(moe):
        gate2 = mod_ref[0, 5:6, :]
        y_ref[...] = x1_ref[...] + gate2 * (_rms(moe) * g_ref[...])

    @pl.when(jnp.logical_not(slow))
    def _():
        for cp in copies(COMB_FAST, slot, st_f):
            cp.wait()
        per = LANES // COMB_FAST
        groups = []
        for gi in range(ne // per):
            val = None
            for k in range(per):
                e = gi * per + k
                rel = pos[:, e:e + 1] - st_f[e].astype(F32) + float(k * COMB_FAST)
                inband = (lane >= k * COMB_FAST) & (lane < (k + 1) * COMB_FAST)
                hit = inband & (lane == rel)
                val = hit if val is None else (val | hit)
            groups.append(jnp.where(val, 1.0, 0.0).astype(BF16))
        onehot = jnp.concatenate(groups, axis=1)
        finish(jnp.dot(onehot, slab_ref[slot, 0:ne * COMB_FAST, :], preferred_element_type=F32))

    @pl.when(slow)
    def _():
        for cp in copies(COMB_SLOW, slot, st_s):
            cp.wait()
        per = COMB_SLOW // LANES
        groups = []
        for e in range(ne):
            rel = pos[:, e:e + 1] - st_s[e].astype(F32)
            for k in range(per):
                groups.append(jnp.where(lane + float(k * LANES) == rel, 1.0, 0.0).astype(BF16))
        onehot = jnp.concatenate(groups, axis=1)
        finish(jnp.dot(onehot, slab_ref[slot], preferred_element_type=F32))


def _combine(tb, pos, x1, mod, g_post_ffn, ye, cap, seq):
    n, d = x1.shape
    assert cap >= COMB_SLOW
    ntiles = n // COMB_TILE
    tiles_per_b = seq // COMB_TILE
    tb_flat = tb[::COMB_TILE // TOK_TILE, :N_EXPERTS].reshape(-1)
    return pl.pallas_call(
        functools.partial(_combine_kernel, cap=cap, ntiles=ntiles),
        grid_spec=pltpu.PrefetchScalarGridSpec(
            num_scalar_prefetch=1,
            grid=(ntiles,),
            in_specs=[pl.BlockSpec((COMB_TILE, LANES), lambda i, tb: (i, 0)),
                      pl.BlockSpec((COMB_TILE, d), lambda i, tb: (i, 0)),
                      pl.BlockSpec((1, N_MOD, d), lambda i, tb: (i // tiles_per_b, 0, 0)),
                      pl.BlockSpec((1, d), lambda i, tb: (0, 0)),
                      pl.BlockSpec(memory_space=pl.ANY)],
            out_specs=pl.BlockSpec((COMB_TILE, d), lambda i, tb: (i, 0)),
            scratch_shapes=[pltpu.VMEM((2, N_EXPERTS * COMB_SLOW, d), BF16),
                            pltpu.SemaphoreType.DMA((2,))]),
        out_shape=jax.ShapeDtypeStruct((n, d), F32),
        compiler_params=_cparams("arbitrary"),
        name="combine",
    )(tb_flat, pos, x1, mod, g_post_ffn, ye)


def _prep_weights(w_in, b_gates, g_mlstm, na_rel_bias, w_out, w_router):
    w = MLSTM_WIDTH
    ng = 4 * MLSTM_HEADS
    w_qt = w_in[:, 0:w].T.astype(BF16)
    w_k = w_in[:, w:2 * w].astype(BF16)
    w_vt = w_in[:, 2 * w:3 * w].T.astype(BF16)
    w_o = w_in[:, 3 * w:4 * w].astype(BF16)
    perm = [d * 2 * MLSTM_HEADS + g * MLSTM_HEADS + h
            for g in range(2) for d in range(2) for h in range(MLSTM_HEADS)]
    perm = jnp.asarray(perm)
    w_g = jnp.pad(w_in[:, 4 * w:4 * w + ng][:, perm], ((0, 0), (0, LANES - ng))).astype(BF16)
    b_g = jnp.pad(b_gates.reshape(-1)[perm].astype(F32), (0, LANES - ng)).reshape(1, LANES)
    w_n = w_in[:, 4 * w + ng:].astype(BF16)
    w_om = w_out[0:w].astype(BF16)
    w_on = w_out[w:].astype(BF16)
    w_r = jnp.pad(w_router, ((0, 0), (0, LANES - N_EXPERTS))).astype(BF16)
    return dict(w_qt=w_qt, w_k=w_k, w_vt=w_vt, w_o=w_o, w_g=w_g, b_g=b_g, w_n=w_n, w_om=w_om, w_on=w_on, w_r=w_r,
                g_mlstm=g_mlstm.reshape(1, w).astype(F32), bias_table=_natten_bias_table(na_rel_bias))


def _layer(x, mod, pw, g_pre_mix, g_post_mix, g_pre_ffn, g_post_ffn, w_eg, w_eu, w_ed):
    bsz, seq, d = x.shape
    n = bsz * seq
    x2d = x.reshape(n, d)
    row = lambda g: g.reshape(1, d).astype(F32)
    qt, k, vt, o, gates, na = _inproj(x2d, mod, row(g_pre_mix), pw["w_qt"], pw["w_k"], pw["w_vt"], pw["w_o"],
                                      pw["w_g"], pw["w_n"], seq)
    grows, gcols = _gateprep(gates, pw["b_g"])
    hf, hb = _mlstm(k.reshape(bsz, seq, -1), qt, vt, grows, gcols.reshape(bsz, seq, -1))
    hn = _natten(na.reshape(bsz, seq, -1), pw["bias_table"])
    x1, x2aug, aff = _postmix(hf.reshape(n, -1), hb.reshape(n, -1), o, hn.reshape(n, -1), x2d, mod,
                              pw["g_mlstm"], pw["w_om"], pw["w_on"], row(g_post_mix), row(g_pre_ffn),
                              pw["w_r"], seq)
    cap = EC_CAPACITY_FACTOR * n // N_EXPERTS
    pos, post, tb = _select(aff, cap)
    xe = _dispatch(x2aug, post, tb, cap)
    ye = _experts(xe, w_eg, w_eu, w_ed, cap)
    y = _combine(tb, pos, x1, mod, row(g_post_ffn), ye, cap, seq)
    return y.reshape(bsz, seq, d)


def kernel(x_prompt, x_sample, c_prompt, c_sample, w_ada, b_ada, g_pre_mix, g_post_mix, w_in, b_gates, g_mlstm, na_rel_bias, w_out, g_pre_ffn, g_post_ffn, w_router, w_expert_gate, w_expert_up, w_expert_down):
    depth = w_ada.shape[0]
    nb = x_prompt.shape[0]
    xs = [x_prompt, x_sample]
    cs = jnp.concatenate([c_prompt, c_sample], axis=0)
    for l in range(depth):
        mod = _mod(cs, w_ada[l], b_ada[l])
        pw = _prep_weights(w_in[l], b_gates[l], g_mlstm[l], na_rel_bias[l], w_out[l], w_router[l])
        mods = [mod[:nb], mod[nb:]]
        xs = [_layer(x, m, pw, g_pre_mix[l], g_post_mix[l], g_pre_ffn[l], g_post_ffn[l],
                     w_expert_gate[l], w_expert_up[l], w_expert_down[l]) for x, m in zip(xs, mods)]
    return (xs[0], xs[1])
```

```python
import functools
import math

import jax
import jax.numpy as jnp
from jax import lax
from jax.experimental import pallas as pl
from jax.experimental.pallas import tpu as pltpu

F32 = jnp.float32
BF16 = jnp.bfloat16

LANES = 128
SUBLANES = 8
BF16_ROWS = 16
VMEM_LIMIT = 56 * 1024 * 1024

RMS_EPS = 1e-6
N_MOD = 6
MLSTM_HEADS = 4
MLSTM_HEAD_DIM = 128
MLSTM_CHUNK = 128
MLSTM_WIDTH = MLSTM_HEADS * MLSTM_HEAD_DIM
NA_HEAD_DIM = 64
NA_HEADS = 8
NA_WIDTH = NA_HEADS * NA_HEAD_DIM
NA_MAX_KH = 8
NA_KW = 16
GRID_W = 64
N_EXPERTS = 16
EC_CAPACITY_FACTOR = 2
NEG_BIG = -1e30


def _cparams(*sem):
    return pltpu.CompilerParams(dimension_semantics=sem, vmem_limit_bytes=VMEM_LIMIT)


def _rms(x):
    return x * lax.rsqrt(jnp.mean(x * x, axis=-1, keepdims=True) + RMS_EPS)


def _mod_kernel(c_ref, w_ref, b_ref, o_ref):
    c = c_ref[...]
    a = c * jax.nn.sigmoid(c)
    o_ref[...] = jnp.dot(a, w_ref[...], preferred_element_type=F32,
                         precision=lax.Precision.HIGHEST) + b_ref[...]


def _mod(c, w_ada, b_ada):
    bsz, d = c.shape
    n = w_ada.shape[1]
    bn = 1024
    out = pl.pallas_call(
        _mod_kernel,
        grid=(n // bn,),
        in_specs=[pl.BlockSpec((bsz, d), lambda j: (0, 0)),
                  pl.BlockSpec((d, bn), lambda j: (0, j)),
                  pl.BlockSpec((1, bn), lambda j: (0, j))],
        out_specs=pl.BlockSpec((bsz, bn), lambda j: (0, j)),
        out_shape=jax.ShapeDtypeStruct((bsz, n), F32),
        compiler_params=_cparams("arbitrary"),
        name="mod",
    )(c, w_ada, b_ada.reshape(1, n))
    return out.reshape(bsz, N_MOD, d)


def _inproj_kernel(x_ref, mod_ref, g_ref, wqt_ref, wk_ref, wvt_ref, wo_ref, wg_ref, wn_ref,
                   qt_ref, k_ref, vt_ref, o_ref, gates_ref, na_ref):
    x = x_ref[...]
    shift = mod_ref[0, 0:1, :]
    scale = mod_ref[0, 1:2, :]
    h = (_rms(x) * g_ref[...]) * (1.0 + scale) + shift
    hb = h.astype(BF16)
    nt = (((1,), (1,)), ((), ()))
    qt_ref[...] = lax.dot_general(wqt_ref[...], hb, nt, preferred_element_type=F32).astype(BF16)
    vt_ref[...] = lax.dot_general(wvt_ref[...], hb, nt, preferred_element_type=F32).astype(BF16)
    k_ref[...] = jnp.dot(hb, wk_ref[...], preferred_element_type=F32).astype(BF16)
    o_ref[...] = jnp.dot(hb, wo_ref[...], preferred_element_type=F32).astype(BF16)
    gates_ref[...] = jnp.dot(hb, wg_ref[...], preferred_element_type=F32)
    na_ref[...] = jnp.dot(hb, wn_ref[...], preferred_element_type=F32).astype(BF16)


def _inproj(x2d, mod, g_pre, w_qt, w_k, w_vt, w_o, w_g, w_n, seq, tm=512):
    n, d = x2d.shape
    tiles_per_b = seq // tm
    full = lambda a: pl.BlockSpec(a.shape, lambda i: (0, 0))
    tok = lambda w: pl.BlockSpec((tm, w.shape[1]), lambda i: (i, 0))
    feat = lambda w: pl.BlockSpec((w.shape[0], tm), lambda i: (0, i))
    return pl.pallas_call(
        _inproj_kernel,
        grid=(n // tm,),
        in_specs=[pl.BlockSpec((tm, d), lambda i: (i, 0)),
                  pl.BlockSpec((1, N_MOD, d), lambda i: (i // tiles_per_b, 0, 0)),
                  full(g_pre), full(w_qt), full(w_k), full(w_vt), full(w_o), full(w_g), full(w_n)],
        out_specs=[feat(w_qt), tok(w_k), feat(w_vt), tok(w_o), tok(w_g), tok(w_n)],
        out_shape=[jax.ShapeDtypeStruct((w_qt.shape[0], n), BF16),
                   jax.ShapeDtypeStruct((n, w_k.shape[1]), BF16),
                   jax.ShapeDtypeStruct((w_vt.shape[0], n), BF16),
                   jax.ShapeDtypeStruct((n, w_o.shape[1]), BF16),
                   jax.ShapeDtypeStruct((n, w_g.shape[1]), F32),
                   jax.ShapeDtypeStruct((n, w_n.shape[1]), BF16)],
        compiler_params=_cparams("arbitrary"),
        name="inproj",
    )(x2d, mod, g_pre, w_qt, w_k, w_vt, w_o, w_g, w_n)


N_CHAINS = 2 * MLSTM_HEADS
GROW_R, GROW_CM, GROW_B, GROW_G, GROW_RMAX = (i * N_CHAINS for i in range(5))
GROWS = 5 * N_CHAINS
GATEPREP_CHUNKS = 8
MLSTM_LOOKAHEAD = 4


def _lane_scan_max(x, lane, reverse):
    sh = 1
    while sh < LANES:
        if reverse:
            x = jnp.maximum(x, jnp.where(lane < LANES - sh, pltpu.roll(x, LANES - sh, 1), NEG_BIG))
        else:
            x = jnp.maximum(x, jnp.where(lane >= sh, pltpu.roll(x, sh, 1), NEG_BIG))
        sh *= 2
    return x


def _gateprep_kernel(g_ref, bias_ref, rows_ref, cols_ref):
    L = MLSTM_CHUNK
    row = lax.broadcasted_iota(jnp.int32, (L, L), 0)
    col = lax.broadcasted_iota(jnp.int32, (L, L), 1)
    tri = jnp.concatenate([jnp.where(row <= col, 1.0, 0.0), jnp.where(row >= col, 1.0, 0.0)], axis=1)
    crow = lax.broadcasted_iota(jnp.int32, (N_CHAINS, L), 0)
    lane = lax.broadcasted_iota(jnp.int32, (N_CHAINS, L), 1)
    is_fwd = crow < MLSTM_HEADS
    for t in range(GATEPREP_CHUNKS):
        gt = (g_ref[t * L:(t + 1) * L, :] + bias_ref[...]).T
        bb = jnp.dot(jax.nn.log_sigmoid(gt[N_CHAINS:2 * N_CHAINS]), tri, preferred_element_type=F32,
                     precision=lax.Precision.HIGHEST)
        b_f, b_b = bb[:, 0:L], bb[:, L:2 * L]
        b = jnp.where(is_fwd, b_f, b_b)
        r = gt[0:N_CHAINS] - b
        g = jnp.where(is_fwd[:, 0:1], b_f[:, L - 1:L], b_b[:, 0:1])
        cm = jnp.where(is_fwd, _lane_scan_max(r, lane, False), _lane_scan_max(r, lane, True))
        rmax = jnp.max(r, axis=1, keepdims=True)
        rows_ref[t] = jnp.concatenate([r, cm, b, jnp.broadcast_to(g, (N_CHAINS, L)),
                                       jnp.broadcast_to(rmax, (N_CHAINS, L))], axis=0)
        cols_ref[t * L:(t + 1) * L, :] = jnp.concatenate([r, jnp.zeros((L - N_CHAINS, L), F32)], axis=0).T


def _gateprep(gates, gate_bias):
    n = gates.shape[0]
    L = MLSTM_CHUNK
    tn = GATEPREP_CHUNKS * L
    return pl.pallas_call(
        _gateprep_kernel,
        grid=(n // tn,),
        in_specs=[pl.BlockSpec((tn, LANES), lambda i: (i, 0)),
                  pl.BlockSpec((1, LANES), lambda i: (0, 0))],
        out_specs=[pl.BlockSpec((GATEPREP_CHUNKS, GROWS, LANES), lambda i: (i, 0, 0)),
                   pl.BlockSpec((tn, LANES), lambda i: (i, 0))],
        out_shape=[jax.ShapeDtypeStruct((n // L, GROWS, LANES), F32),
                   jax.ShapeDtypeStruct((n, LANES), F32)],
        compiler_params=_cparams("arbitrary"),
        name="gateprep",
    )(gates, gate_bias)


def _mlstm_scores(k, qt, n_ref, ci):
    n_st = n_ref[ci]
    n_hi = n_st.astype(BF16).astype(F32)
    n_lo = (n_st - n_hi).astype(BF16).astype(F32)
    nn = jnp.concatenate([n_hi, n_lo, jnp.zeros((BF16_ROWS - 2, MLSTM_HEAD_DIM), F32)], axis=0).astype(BF16)
    return jnp.dot(k, qt, preferred_element_type=F32), jnp.dot(nn, qt, preferred_element_type=F32)


def _mlstm_chain(kq, qn2, k, qt, vt, rows, cols, mask_sj, c_ref, n_ref, m_ref, ci):
    L = MLSTM_CHUNK
    scale = MLSTM_HEAD_DIM ** -0.5
    r_row = rows[GROW_R + ci:GROW_R + ci + 1]
    cm_row = rows[GROW_CM + ci:GROW_CM + ci + 1]
    b_row = rows[GROW_B + ci:GROW_B + ci + 1]
    g = rows[GROW_G + ci:GROW_G + ci + 1, 0:1]
    rmax = rows[GROW_RMAX + ci:GROW_RMAX + ci + 1, 0:1]
    r_colb = jnp.broadcast_to(cols[:, ci:ci + 1], (L, L))
    c_st = c_ref[ci]
    n_st = n_ref[ci]
    m_st = m_ref[ci][:, 0:1]
    mm = jnp.maximum(cm_row, m_st)
    pt = jnp.where(mask_sj, jnp.exp(r_colb - mm), 0.0)
    st = kq * scale * pt
    s_int = jnp.exp(m_st - mm)
    den = jnp.sum(st, axis=0, keepdims=True) + s_int * (qn2[0:1] + qn2[1:2])
    inv = 1.0 / jnp.maximum(jnp.abs(den), jnp.exp(-(b_row + mm)))
    lhs = jnp.concatenate([vt, c_st.astype(BF16)], axis=1)
    rhs = jnp.concatenate([st.astype(BF16), (qt.astype(F32) * s_int).astype(BF16)], axis=0)
    h = (jnp.dot(lhs, rhs, preferred_element_type=F32) * inv).astype(BF16).T

    w_row = jnp.exp(r_row - rmax)
    wl = jnp.concatenate([(vt.astype(F32) * w_row).astype(BF16),
                          jnp.broadcast_to(w_row, (BF16_ROWS, L)).astype(BF16)], axis=0)
    loc = jnp.dot(wl, k, preferred_element_type=F32) * scale
    m_loc = g + rmax
    m_new = jnp.maximum(g + m_st, m_loc)
    s_old = jnp.exp(g + m_st - m_new)
    s_loc = jnp.exp(m_loc - m_new)
    c_ref[ci] = s_old * c_st + s_loc * loc[0:MLSTM_HEAD_DIM]
    n_ref[ci] = s_old * n_st + s_loc * loc[MLSTM_HEAD_DIM:MLSTM_HEAD_DIM + 1]
    m_ref[ci] = jnp.broadcast_to(m_new, (1, LANES))
    return h


def _mlstm_kernel(kf_ref, kb_ref, qtf_ref, qtb_ref, vtf_ref, vtb_ref, rf_ref, rb_ref, cf_ref, cb_ref,
                  hf_ref, hb_ref, c_ref, n_ref, m_ref):
    L = MLSTM_CHUNK

    @pl.when(pl.program_id(1) == 0)
    def _():
        c_ref[...] = jnp.zeros_like(c_ref)
        n_ref[...] = jnp.zeros_like(n_ref)
        m_ref[...] = jnp.zeros_like(m_ref)

    row = lax.broadcasted_iota(jnp.int32, (L, L), 0)
    col = lax.broadcasted_iota(jnp.int32, (L, L), 1)
    dirs = ((kf_ref, qtf_ref, vtf_ref, rf_ref, cf_ref, hf_ref, row <= col),
            (kb_ref, qtb_ref, vtb_ref, rb_ref, cb_ref, hb_ref, row >= col))

    def operands(ci):
        d, hh = divmod(ci, MLSTM_HEADS)
        sl = slice(hh * MLSTM_HEAD_DIM, (hh + 1) * MLSTM_HEAD_DIM)
        return dirs[d], sl

    def scores(ci):
        (k_ref, qt_ref, _, _, _, _, _), sl = operands(ci)
        return _mlstm_scores(k_ref[0, :, sl], qt_ref[sl, :], n_ref, ci)

    pre = [scores(ci) for ci in range(MLSTM_LOOKAHEAD)]
    for ci in range(N_CHAINS):
        if ci + MLSTM_LOOKAHEAD < N_CHAINS:
            pre.append(scores(ci + MLSTM_LOOKAHEAD))
        kq, qn2 = pre.pop(0)
        (k_ref, qt_ref, vt_ref, r_ref, cl_ref, o_ref, mask), sl = operands(ci)
        o_ref[0, :, sl] = _mlstm_chain(kq, qn2, k_ref[0, :, sl], qt_ref[sl, :], vt_ref[sl, :],
                                       r_ref[0], cl_ref[0], mask, c_ref, n_ref, m_ref, ci)


def _mlstm(k, qt, vt, grows, gcols):
    bsz, seq, _ = k.shape
    L = MLSTM_CHUNK
    nc = seq // L
    fwd = lambda b, c: (b, c, 0)
    bwd = lambda b, c: (b, nc - 1 - c, 0)
    fwd_t = lambda b, c: (0, b * nc + c)
    bwd_t = lambda b, c: (0, b * nc + nc - 1 - c)
    fwd_r = lambda b, c: (b * nc + c, 0, 0)
    bwd_r = lambda b, c: (b * nc + nc - 1 - c, 0, 0)
    return pl.pallas_call(
        _mlstm_kernel,
        grid=(bsz, nc),
        in_specs=[pl.BlockSpec((1, L, MLSTM_WIDTH), fwd), pl.BlockSpec((1, L, MLSTM_WIDTH), bwd),
                  pl.BlockSpec((MLSTM_WIDTH, L), fwd_t), pl.BlockSpec((MLSTM_WIDTH, L), bwd_t),
                  pl.BlockSpec((MLSTM_WIDTH, L), fwd_t), pl.BlockSpec((MLSTM_WIDTH, L), bwd_t),
                  pl.BlockSpec((1, GROWS, LANES), fwd_r), pl.BlockSpec((1, GROWS, LANES), bwd_r),
                  pl.BlockSpec((1, L, LANES), fwd), pl.BlockSpec((1, L, LANES), bwd)],
        out_specs=[pl.BlockSpec((1, L, MLSTM_WIDTH), fwd),
                   pl.BlockSpec((1, L, MLSTM_WIDTH), bwd)],
        out_shape=[jax.ShapeDtypeStruct((bsz, seq, MLSTM_WIDTH), BF16),
                   jax.ShapeDtypeStruct((bsz, seq, MLSTM_WIDTH), BF16)],
        scratch_shapes=[pltpu.VMEM((N_CHAINS, MLSTM_HEAD_DIM, MLSTM_HEAD_DIM), F32),
                        pltpu.VMEM((N_CHAINS, 1, MLSTM_HEAD_DIM), F32),
                        pltpu.VMEM((N_CHAINS, 1, LANES), F32)],
        compiler_params=_cparams("arbitrary", "arbitrary"),
        name="mlstm",
    )(k, k, qt, qt, vt, vt, grows, grows, gcols, gcols)


NA_QROWS = 8
NA_KROWS = 2 * NA_QROWS
NA_KBLK = 4
NA_NKBLK = NA_KROWS // NA_KBLK
NA_CODE_SECOND = 2 * NA_MAX_KH - 2
NA_CODE_FIRST = NA_CODE_SECOND + NA_MAX_KH
NA_CODE_NONE = NA_CODE_FIRST + NA_MAX_KH
NA_NCODES = NA_CODE_NONE + 1


NA_LIVE_PAIRS = NA_MAX_KH // 2 + 1


def _natten_kernel(q_ref, k0_ref, k1_ref, k2_ref, k3_ref, v0_ref, v1_ref, v2_ref, v3_ref, bias_ref, o_ref,
                   st0_ref, st1_ref, pt0_ref, pt1_ref, *, rows):
    i = pl.program_id(1)
    half = NA_MAX_KH // 2
    rq0 = i * NA_QROWS
    w0 = jnp.clip(rq0 - half, 0, rows - NA_KROWS)
    nqp = NA_QROWS // 2
    nkp = NA_KROWS // 2
    tq = NA_QROWS * GRID_W

    def window_start(r):
        return jnp.clip(r - half, 0, rows - NA_MAX_KH)

    def code(rq, kp):
        r = rq0 + rq
        r0 = window_start(r)
        kr = w0 + 2 * kp
        d0 = kr - r + (NA_MAX_KH - 1)
        va = (kr >= r0) & (kr < r0 + NA_MAX_KH)
        vb = (kr + 1 >= r0) & (kr + 1 < r0 + NA_MAX_KH)
        return jnp.where(va & vb, d0,
                         jnp.where(vb, NA_CODE_SECOND + d0 + 1,
                                   jnp.where(va, NA_CODE_FIRST + d0 - (NA_MAX_KH - 1), NA_CODE_NONE)))

    first = [jnp.minimum((window_start(rq0 + 2 * qi) - w0) // 2, nkp - NA_LIVE_PAIRS) for qi in range(nqp)]
    codes = [[(code(2 * qi, first[qi] + t), code(2 * qi + 1, first[qi] + t)) for t in range(NA_LIVE_PAIRS)]
             for qi in range(nqp)]
    lane_t = lax.broadcasted_iota(jnp.int32, (LANES, LANES), 1)
    lane_q = lax.broadcasted_iota(jnp.int32, (tq, LANES), 1)
    sub_o = lax.broadcasted_iota(jnp.int32, (LANES, tq), 0)
    scale = jnp.asarray(NA_HEAD_DIM ** -0.5, BF16)
    k_refs = (k0_ref, k1_ref, k2_ref, k3_ref)
    v_refs = (v0_ref, v1_ref, v2_ref, v3_ref)
    st_refs = (st0_ref, st1_ref)
    pt_refs = (pt0_ref, pt1_ref)
    pt0_ref[...] = jnp.zeros_like(pt0_ref)
    pt1_ref[...] = jnp.zeros_like(pt1_ref)

    def lanes_of(hp):
        return slice(hp * LANES, (hp + 1) * LANES)

    def scores(h):
        hp, sb = divmod(h, 2)
        lsl = lanes_of(hp)
        qp = q_ref[0, :, lsl] * scale
        own = (lane_q >= sb * NA_HEAD_DIM) & (lane_q < (sb + 1) * NA_HEAD_DIM)
        qm = jnp.where(own, qp, jnp.zeros_like(qp))
        k_all = jnp.concatenate([r[0, :, lsl] for r in k_refs], axis=0)
        st_refs[sb][...] = lax.dot_general(k_all, qm, (((1,), (1,)), ((), ())), preferred_element_type=F32)

    def attend(h):
        hp, sb = divmod(h, 2)
        st_ref, pt_ref = st_refs[sb], pt_refs[sb]
        invs = []
        for qi in range(nqp):
            qsl = slice(qi * LANES, (qi + 1) * LANES)
            offs = [pl.multiple_of((first[qi] + t) * LANES, LANES) for t in range(NA_LIVE_PAIRS)]
            tiles = []
            for t in range(NA_LIVE_PAIRS):
                ca, cb = codes[qi][t]
                bias = jnp.where(lane_t < GRID_W, bias_ref[h, ca], bias_ref[h, cb])
                tiles.append(st_ref[pl.ds(offs[t], LANES), qsl] + bias)
            m = functools.reduce(jnp.maximum, tiles)
            m = jnp.max(m, axis=0, keepdims=True)
            es = [jnp.exp(t - m) for t in tiles]
            l = jnp.sum(functools.reduce(jnp.add, es), axis=0, keepdims=True)
            invs.append(1.0 / l)
            for t in range(NA_LIVE_PAIRS):
                pt_ref[pl.ds(offs[t], LANES), qsl] = es[t].astype(BF16)
        vt = jnp.concatenate([r[0, :, lanes_of(hp)] for r in v_refs], axis=0).T
        return jnp.dot(vt, pt_ref[...], preferred_element_type=F32) * jnp.concatenate(invs, axis=1)

    scores(0)
    outs = []
    for h in range(NA_HEADS):
        if h + 1 < NA_HEADS:
            scores(h + 1)
        outs.append(attend(h))
        if h % 2 == 1:
            ot = jnp.where(sub_o < NA_HEAD_DIM, outs[0], outs[1])
            o_ref[0, :, lanes_of(h // 2)] = ot.T.astype(o_ref.dtype)
            outs = []


def _natten_bias_table(rel_bias):
    cols = jnp.arange(GRID_W)
    col_start = jnp.clip(cols - NA_KW // 2, 0, GRID_W - NA_KW)
    kcol = jnp.arange(GRID_W)
    dc = kcol[:, None] - cols[None, :] + (NA_KW - 1)
    inwin = (kcol[:, None] >= col_start[None, :]) & (kcol[:, None] < col_start[None, :] + NA_KW)
    tab = rel_bias.astype(F32)[:, :, jnp.clip(dc, 0, 2 * NA_KW - 2)]
    tab = jnp.where(inwin[None, None], tab, NEG_BIG)
    neg = jnp.full_like(tab[:, :NA_MAX_KH], NEG_BIG)
    both = jnp.concatenate([tab[:, :-1], tab[:, 1:]], axis=2)
    second = jnp.concatenate([neg, tab[:, :NA_MAX_KH]], axis=2)
    first = jnp.concatenate([tab[:, NA_MAX_KH - 1:], neg], axis=2)
    none = jnp.concatenate([neg[:, :1], neg[:, :1]], axis=2)
    t = jnp.concatenate([both, second, first, none], axis=1)
    return jnp.concatenate([t, t], axis=-1)


def _natten(na, bias_table):
    bsz, seq, _ = na.shape
    rows = seq // GRID_W
    tq = NA_QROWS * GRID_W
    tk = NA_KBLK * GRID_W
    nkb = seq // tk

    def kv_spec(j, col):
        return pl.BlockSpec((1, tk, NA_WIDTH),
                            lambda b, i: (b, jnp.clip(2 * i - 1, 0, nkb - NA_NKBLK) + j, col))

    return pl.pallas_call(
        functools.partial(_natten_kernel, rows=rows),
        grid=(bsz, rows // NA_QROWS),
        in_specs=[pl.BlockSpec((1, tq, NA_WIDTH), lambda b, i: (b, i, 0))]
                 + [kv_spec(j, 1) for j in range(NA_NKBLK)]
                 + [kv_spec(j, 2) for j in range(NA_NKBLK)]
                 + [pl.BlockSpec(bias_table.shape, lambda b, i: (0, 0, 0, 0), pipeline_mode=pl.Buffered(1))],
        out_specs=pl.BlockSpec((1, tq, NA_WIDTH), lambda b, i: (b, i, 0)),
        out_shape=jax.ShapeDtypeStruct((bsz, seq, NA_WIDTH), BF16),
        scratch_shapes=[pltpu.VMEM((NA_KROWS * GRID_W, tq), F32), pltpu.VMEM((NA_KROWS * GRID_W, tq), F32),
                        pltpu.VMEM((NA_KROWS * GRID_W, tq), BF16), pltpu.VMEM((NA_KROWS * GRID_W, tq), BF16)],
        compiler_params=_cparams("arbitrary", "arbitrary"),
        name="natten",
    )(na, *([na] * (2 * NA_NKBLK)), bias_table)


AUG = LANES
GATE_LO_SHIFT = N_EXPERTS


def _postmix_kernel(hf_ref, hb_ref, o_ref, hn_ref, x_ref, mod_ref, gm_ref, wm_ref, wn_ref,
                    gpost_ref, gffn_ref, wr_ref, x1_ref, x2_ref, aff_ref):
    d = x_ref.shape[-1]
    s = hf_ref[...].astype(F32) + hb_ref[...].astype(F32)
    og = jax.nn.sigmoid(o_ref[...].astype(F32))
    heads = []
    for hh in range(MLSTM_HEADS):
        sl = slice(hh * MLSTM_HEAD_DIM, (hh + 1) * MLSTM_HEAD_DIM)
        heads.append(_rms(s[:, sl]) * gm_ref[:, sl] * og[:, sl])
    hm = jnp.concatenate(heads, axis=-1).astype(BF16)
    mix = (jnp.dot(hm, wm_ref[...], preferred_element_type=F32)
           + jnp.dot(hn_ref[...], wn_ref[...], preferred_element_type=F32))
    gate1 = mod_ref[0, 2:3, :]
    shift2 = mod_ref[0, 3:4, :]
    scale2 = mod_ref[0, 4:5, :]
    x1 = x_ref[...] + gate1 * (_rms(mix) * gpost_ref[...])
    x1_ref[...] = x1
    h2 = (_rms(x1) * gffn_ref[...]) * (1.0 + scale2) + shift2
    h2b = h2.astype(BF16)
    x2_ref[:, 0:d] = h2b
    logits = jnp.dot(h2b, wr_ref[...], preferred_element_type=F32)
    lane = lax.broadcasted_iota(jnp.int32, logits.shape, 1)
    logits = jnp.where(lane < N_EXPERTS, logits, NEG_BIG)
    ex = jnp.exp(logits - jnp.max(logits, axis=-1, keepdims=True))
    aff = ex / jnp.sum(ex, axis=-1, keepdims=True)
    aff_ref[...] = aff
    hi = aff.astype(BF16).astype(F32)
    lo = (aff - hi).astype(BF16).astype(F32)
    x2_ref[:, d:d + AUG] = (hi + pltpu.roll(lo, GATE_LO_SHIFT, 1)).astype(BF16)


def _postmix(hf, hb, o, hn, x2d, mod, g_mlstm, w_om, w_on, g_post, g_ffn, w_r, seq, tm=512):
    n, d = x2d.shape
    tiles_per_b = seq // tm
    full = lambda a: pl.BlockSpec(a.shape, lambda i: (0, 0))
    tok = lambda w: pl.BlockSpec((tm, w), lambda i: (i, 0))
    return pl.pallas_call(
        _postmix_kernel,
        grid=(n // tm,),
        in_specs=[tok(MLSTM_WIDTH), tok(MLSTM_WIDTH), tok(MLSTM_WIDTH), tok(NA_WIDTH), tok(d),
                  pl.BlockSpec((1, N_MOD, d), lambda i: (i // tiles_per_b, 0, 0)),
                  full(g_mlstm), full(w_om), full(w_on), full(g_post), full(g_ffn), full(w_r)],
        out_specs=[tok(d), tok(d + AUG), tok(LANES)],
        out_shape=[jax.ShapeDtypeStruct((n, d), F32),
                   jax.ShapeDtypeStruct((n, d + AUG), BF16),
                   jax.ShapeDtypeStruct((n, LANES), F32)],
        compiler_params=_cparams("arbitrary"),
        name="postmix",
    )(hf, hb, o, hn, x2d, mod, g_mlstm, w_om, w_on, g_post, g_ffn, w_r)


TOK_TILE = 128
SEL_STEP = 1024
UNSELECTED = -1e6
SELECT_HI = 2.0
SELECT_ITERS = 160


def _select_kernel(afft_ref, aff_ref, pos_ref, post_ref, tb_ref, thr_ref, need_ref, base_ref, *, cap):
    step = pl.program_id(0)
    ne = afft_ref.shape[0]

    @pl.when(step == 0)
    def _():
        a_t = afft_ref[...]

        def count_ge(v):
            return jnp.sum(jnp.where(a_t >= v, 1.0, 0.0), axis=1, keepdims=True)

        def body(it, lh):
            lo, hi = lh
            mid = 0.5 * (lo + hi)
            ge = count_ge(mid) >= cap
            return jnp.where(ge, mid, lo), jnp.where(ge, hi, mid)

        lo_col, hi_col = lax.fori_loop(0, SELECT_ITERS, body,
                                       (jnp.zeros((ne, 1), F32), jnp.full((ne, 1), SELECT_HI, F32)))
        need_col = cap - count_ge(hi_col)
        sub = lax.broadcasted_iota(jnp.int32, (ne, LANES), 0)
        ln = lax.broadcasted_iota(jnp.int32, (ne, LANES), 1)
        diag = sub == ln
        thr_ref[...] = jnp.sum(jnp.where(diag, lo_col, 0.0), axis=0, keepdims=True)
        need_ref[...] = jnp.sum(jnp.where(diag, need_col, 0.0), axis=0, keepdims=True)
        base_ref[...] = jnp.zeros_like(base_ref)
        base_ref[2:3, :] = jnp.sum(jnp.where(diag, hi_col, 0.0), axis=0, keepdims=True)

    thr_lo = thr_ref[...]
    thr_hi = base_ref[2:3, :]
    need = need_ref[...]
    lane = lax.broadcasted_iota(jnp.int32, (TOK_TILE, LANES), 1)
    valid = lane < ne
    row = lax.broadcasted_iota(jnp.int32, (TOK_TILE, TOK_TILE), 0)
    col = lax.broadcasted_iota(jnp.int32, (TOK_TILE, TOK_TILE), 1)
    strict_lower = (col < row).astype(BF16)
    base_gt = base_ref[0:1, :]
    base_eq = base_ref[1:2, :]
    tb_rows = []
    for t in range(SEL_STEP // TOK_TILE):
        a = aff_ref[t * TOK_TILE:(t + 1) * TOK_TILE, :]
        gt = (a >= thr_hi) & valid
        eq = (a >= thr_lo) & (a < thr_hi) & valid
        gtf = jnp.where(gt, 1.0, 0.0)
        eqf = jnp.where(eq, 1.0, 0.0)
        cg = jnp.dot(strict_lower, gtf.astype(BF16), preferred_element_type=F32)
        ce = jnp.dot(strict_lower, eqf.astype(BF16), preferred_element_type=F32)
        eq_rank = base_eq + ce
        sel = gt | (eq & (eq_rank < need))
        pos = base_gt + cg + jnp.minimum(eq_rank, need)
        posm = jnp.where(sel, pos, UNSELECTED)
        pos_ref[t * TOK_TILE:(t + 1) * TOK_TILE, :] = posm
        post_ref[:, t * TOK_TILE:(t + 1) * TOK_TILE] = posm.T[0:ne, :]
        tb_rows.append(base_gt + jnp.minimum(base_eq, need))
        base_gt = base_gt + jnp.sum(gtf, axis=0, keepdims=True)
        base_eq = base_eq + jnp.sum(eqf, axis=0, keepdims=True)
    tb_ref[...] = jnp.concatenate(tb_rows, axis=0).astype(jnp.int32)
    base_ref[0:1, :] = base_gt
    base_ref[1:2, :] = base_eq


def _select(aff, cap):
    n = aff.shape[0]
    afft = aff[:, :N_EXPERTS].T
    nsteps = n // SEL_STEP
    tiles_per_step = SEL_STEP // TOK_TILE
    return pl.pallas_call(
        functools.partial(_select_kernel, cap=cap),
        grid=(nsteps,),
        in_specs=[pl.BlockSpec((N_EXPERTS, n), lambda i: (0, 0)),
                  pl.BlockSpec((SEL_STEP, LANES), lambda i: (i, 0))],
        out_specs=[pl.BlockSpec((SEL_STEP, LANES), lambda i: (i, 0)),
                   pl.BlockSpec((N_EXPERTS, SEL_STEP), lambda i: (0, i)),
                   pl.BlockSpec((tiles_per_step, LANES), lambda i: (i, 0))],
        out_shape=[jax.ShapeDtypeStruct((n, LANES), F32),
                   jax.ShapeDtypeStruct((N_EXPERTS, n), F32),
                   jax.ShapeDtypeStruct((n // TOK_TILE, LANES), jnp.int32)],
        scratch_shapes=[pltpu.VMEM((1, LANES), F32), pltpu.VMEM((1, LANES), F32),
                        pltpu.VMEM((SUBLANES, LANES), F32)],
        compiler_params=_cparams("arbitrary"),
        name="select",
    )(afft, aff)


DISP_TILE = 2 * TOK_TILE
DISP_FAST = 80
DISP_SLOW = SUBLANES + DISP_TILE + SUBLANES
CARRY = SUBLANES
XE_PAD = 2 * DISP_TILE
DISP_SLOW_GROUP = 4
DISP_STAGE_ROWS = max(N_EXPERTS * DISP_FAST, DISP_SLOW_GROUP * DISP_SLOW)


def _dispatch_kernel(tb_ref, x_ref, post_ref, xe_ref, stage_ref, carry_ref, sem, flag_ref, *, cap, ntiles):
    i = pl.program_id(0)
    slot = lax.rem(i, 2)
    ne = N_EXPERTS

    @pl.when(i == 0)
    def _():
        carry_ref[...] = jnp.zeros_like(carry_ref)
        stage_ref[1, 0:XE_PAD, :] = jnp.zeros((XE_PAD, stage_ref.shape[-1]), F32)
        pads = [pltpu.make_async_copy(stage_ref.at[1, pl.ds(0, XE_PAD)], xe_ref.at[e, pl.ds(cap, XE_PAD)],
                                      sem.at[1]) for e in range(ne)]
        for cp in pads:
            cp.start()
        for cp in pads:
            cp.wait()

    fill = [tb_ref[i * ne + e] for e in range(ne)]
    nxt = [jnp.where(i + 1 < ntiles, tb_ref[jnp.minimum(i + 1, ntiles - 1) * ne + e], cap) for e in range(ne)]
    al = [(f // CARRY) * CARRY for f in fill]
    need_max = functools.reduce(jnp.maximum, [nx - a for nx, a in zip(nxt, al)])
    slow = need_max > DISP_FAST

    def copies(s_rows, sl, experts, als):
        return [pltpu.make_async_copy(stage_ref.at[sl, pl.ds(k * s_rows, s_rows)],
                                      xe_ref.at[e, pl.ds(pl.multiple_of(als[e], CARRY), s_rows)],
                                      sem.at[sl]) for k, e in enumerate(experts)]

    def wait_in_flight(sl):
        @pl.when(flag_ref[sl] == 0)
        def _():
            for cp in copies(DISP_FAST, sl, range(ne), [0] * ne):
                cp.wait()

    def stage(s_rows, experts, xt, cb):
        slot_of_tok = lax.broadcasted_iota(jnp.int32, (s_rows, DISP_TILE), 0).astype(F32)
        sub = lax.broadcasted_iota(jnp.int32, (s_rows, ne * CARRY), 0)
        ln = lax.broadcasted_iota(jnp.int32, (s_rows, ne * CARRY), 1)
        a_tok, a_car = [], []
        for e in experts:
            rel = post_ref[e:e + 1, :] - al[e].astype(F32)
            a_tok.append(jnp.where(slot_of_tok == rel, 1.0, 0.0))
            own = (ln >= e * CARRY) & (ln < e * CARRY + (fill[e] - al[e])) & (ln - e * CARRY == sub)
            a_car.append(jnp.where(own, 1.0, 0.0))
        stage_ref[slot, 0:len(experts) * s_rows, :] = (
            jnp.dot(jnp.concatenate(a_tok, axis=0).astype(BF16), xt, preferred_element_type=F32)
            + jnp.dot(jnp.concatenate(a_car, axis=0).astype(BF16), cb, preferred_element_type=F32))
        for k, e in enumerate(experts):
            off = jnp.minimum((nxt[e] // CARRY) * CARRY - al[e], s_rows - CARRY)
            carry_ref[e * CARRY:(e + 1) * CARRY, :] = stage_ref[slot, pl.ds(pl.multiple_of(k * s_rows + off, CARRY), CARRY), :]

    @pl.when(jnp.logical_not(slow))
    def _():
        stage(DISP_FAST, range(ne), x_ref[...], carry_ref[...].astype(BF16))

        @pl.when(i > 0)
        def _():
            wait_in_flight(1 - slot)
        for cp in copies(DISP_FAST, slot, range(ne), al):
            cp.start()
        flag_ref[slot] = 0

    @pl.when(slow)
    def _():
        xt = x_ref[...]
        cb = carry_ref[...].astype(BF16)

        @pl.when(i > 0)
        def _():
            wait_in_flight(1 - slot)
        for e0 in range(0, ne, DISP_SLOW_GROUP):
            experts = range(e0, e0 + DISP_SLOW_GROUP)
            stage(DISP_SLOW, experts, xt, cb)
            cps = copies(DISP_SLOW, slot, experts, al)
            for cp in cps:
                cp.start()
            for cp in cps:
                cp.wait()
        flag_ref[slot] = 1

    @pl.when(i == ntiles - 1)
    def _():
        wait_in_flight(slot)


def _dispatch(x2aug, post, tb, cap):
    n, wd = x2aug.shape
    ntiles = n // DISP_TILE
    tb_flat = tb[::DISP_TILE // TOK_TILE, :N_EXPERTS].reshape(-1)
    return pl.pallas_call(
        functools.partial(_dispatch_kernel, cap=cap, ntiles=ntiles),
        grid_spec=pltpu.PrefetchScalarGridSpec(
            num_scalar_prefetch=1,
            grid=(ntiles,),
            in_specs=[pl.BlockSpec((DISP_TILE, wd), lambda i, tb: (i, 0)),
                      pl.BlockSpec((N_EXPERTS, DISP_TILE), lambda i, tb: (0, i))],
            out_specs=pl.BlockSpec(memory_space=pl.ANY),
            scratch_shapes=[pltpu.VMEM((2, DISP_STAGE_ROWS, wd), F32),
                            pltpu.VMEM((N_EXPERTS * CARRY, wd), F32),
                            pltpu.SemaphoreType.DMA((2,)),
                            pltpu.SMEM((2,), jnp.int32)]),
        out_shape=jax.ShapeDtypeStruct((N_EXPERTS, cap + XE_PAD, wd), F32),
        compiler_params=_cparams("arbitrary"),
        name="dispatch",
    )(tb_flat, x2aug, post)


EXP_ROWS = 512


def _experts_kernel(xe_ref, wg_ref, wu_ref, wd_ref, ye_ref, wgb_ref, wub_ref, wdb_ref):
    e = pl.program_id(0)
    d = wg_ref.shape[1]

    @pl.when(pl.program_id(1) == 0)
    def _():
        wgb_ref[...] = wg_ref[0].astype(BF16)
        wub_ref[...] = wu_ref[0].astype(BF16)
        wdb_ref[...] = wd_ref[0].astype(BF16)

    xa = xe_ref[0]
    x = xa[:, 0:d].astype(BF16)
    aug = xa[:, d:d + AUG]
    lane = lax.broadcasted_iota(jnp.int32, aug.shape, 1)
    mine = (lane == e) | (lane == e + GATE_LO_SHIFT)
    gate = jnp.sum(jnp.where(mine, aug, 0.0), axis=-1, keepdims=True)
    g = jnp.dot(x, wgb_ref[...], preferred_element_type=F32)
    u = jnp.dot(x, wub_ref[...], preferred_element_type=F32)
    hid = (g * jax.nn.sigmoid(g)) * u
    y = jnp.dot(hid.astype(BF16), wdb_ref[...], preferred_element_type=F32)
    ye_ref[0] = (y * gate).astype(ye_ref.dtype)


def _experts(xe, w_gate, w_up, w_down, cap):
    ne, _, wd = xe.shape
    d = w_gate.shape[1]
    f = w_gate.shape[2]
    wspec = lambda s: pl.BlockSpec((1,) + s, lambda e, j: (e, 0, 0))
    rows = min(EXP_ROWS, cap)
    return pl.pallas_call(
        _experts_kernel,
        grid=(ne, cap // rows),
        in_specs=[pl.BlockSpec((1, rows, wd), lambda e, j: (e, j, 0)),
                  wspec((d, f)), wspec((d, f)), wspec((f, d))],
        out_specs=pl.BlockSpec((1, rows, d), lambda e, j: (e, j, 0)),
        out_shape=jax.ShapeDtypeStruct((ne, cap, d), BF16),
        scratch_shapes=[pltpu.VMEM((d, f), BF16), pltpu.VMEM((d, f), BF16), pltpu.VMEM((f, d), BF16)],
        compiler_params=_cparams("arbitrary", "arbitrary"),
        name="experts",
    )(xe, w_gate, w_up, w_down)


COMB_TILE = 2 * TOK_TILE
COMB_FAST = 128
COMB_SLOW = 384
COMB_ALIGN = BF16_ROWS


def _combine_kernel(tb_ref, pos_ref, x1_ref, mod_ref, g_ref, ye_ref, y_ref, slab_ref, sem, *, cap, ntiles):
    i = pl.program_id(0)
    slot = lax.rem(i, 2)
    ne = N_EXPERTS

    def info(t):
        t = jnp.minimum(t, ntiles - 1)
        fill = [tb_ref[t * ne + e] for e in range(ne)]
        nxt = [jnp.where(t + 1 < ntiles, tb_ref[jnp.minimum(t + 1, ntiles - 1) * ne + e], cap) for e in range(ne)]
        st_f = [jnp.minimum((f // COMB_ALIGN) * COMB_ALIGN, cap - COMB_FAST) for f in fill]
        st_s = [jnp.minimum((f // COMB_ALIGN) * COMB_ALIGN, cap - COMB_SLOW) for f in fill]
        slow = functools.reduce(jnp.maximum, [nx - s for nx, s in zip(nxt, st_f)]) > COMB_FAST
        return st_f, st_s, slow

    def copies(s_rows, sl, starts):
        return [pltpu.make_async_copy(ye_ref.at[e, pl.ds(pl.multiple_of(starts[e], COMB_ALIGN), s_rows)],
                                      slab_ref.at[sl, pl.ds(e * s_rows, s_rows)],
                                      sem.at[sl]) for e in range(ne)]

    def fetch(t, sl):
        st_f, st_s, slow = info(t)

        @pl.when(slow)
        def _():
            for cp in copies(COMB_SLOW, sl, st_s):
                cp.start()

        @pl.when(jnp.logical_not(slow))
        def _():
            for cp in copies(COMB_FAST, sl, st_f):
                cp.start()

    @pl.when(i == 0)
    def _():
        fetch(i, slot)

    @pl.when(i + 1 < ntiles)
    def _():
        fetch(i + 1, 1 - slot)

    st_f, st_s, slow = info(i)
    lane = lax.broadcasted_iota(jnp.int32, (COMB_TILE, LANES), 1).astype(F32)
    pos = pos_ref[...]

    def finish(moe):
        gate2 = mod_ref[0, 5:6, :]
        y_ref[...] = x1_ref[...] + gate2 * (_rms(moe) * g_ref[...])

    @pl.when(jnp.logical_not(slow))
    def _():
        for cp in copies(COMB_FAST, slot, st_f):
            cp.wait()
        per = LANES // COMB_FAST
        groups = []
        for gi in range(ne // per):
            val = None
            for k in range(per):
                e = gi * per + k
                rel = pos[:, e:e + 1] - st_f[e].astype(F32) + float(k * COMB_FAST)
                inband = (lane >= k * COMB_FAST) & (lane < (k + 1) * COMB_FAST)
                hit = inband & (lane == rel)
                val = hit if val is None else (val | hit)
            groups.append(jnp.where(val, 1.0, 0.0).astype(BF16))
        onehot = jnp.concatenate(groups, axis=1)
        finish(jnp.dot(onehot, slab_ref[slot, 0:ne * COMB_FAST, :], preferred_element_type=F32))

    @pl.when(slow)
    def _():
        for cp in copies(COMB_SLOW, slot, st_s):
            cp.wait()
        per = COMB_SLOW // LANES
        groups = []
        for e in range(ne):
            rel = pos[:, e:e + 1] - st_s[e].astype(F32)
            for k in range(per):
                groups.append(jnp.where(lane + float(k * LANES) == rel, 1.0, 0.0).astype(BF16))
        onehot = jnp.concatenate(groups, axis=1)
        finish(jnp.dot(onehot, slab_ref[slot], preferred_element_type=F32))


def _combine(tb, pos, x1, mod, g_post_ffn, ye, cap, seq):
    n, d = x1.shape
    assert cap >= COMB_SLOW
    ntiles = n // COMB_TILE
    tiles_per_b = seq // COMB_TILE
    tb_flat = tb[::COMB_TILE // TOK_TILE, :N_EXPERTS].reshape(-1)
    return pl.pallas_call(
        functools.partial(_combine_kernel, cap=cap, ntiles=ntiles),
        grid_spec=pltpu.PrefetchScalarGridSpec(
            num_scalar_prefetch=1,
            grid=(ntiles,),
            in_specs=[pl.BlockSpec((COMB_TILE, LANES), lambda i, tb: (i, 0)),
                      pl.BlockSpec((COMB_TILE, d), lambda i, tb: (i, 0)),
                      pl.BlockSpec((1, N_MOD, d), lambda i, tb: (i // tiles_per_b, 0, 0)),
                      pl.BlockSpec((1, d), lambda i, tb: (0, 0)),
                      pl.BlockSpec(memory_space=pl.ANY)],
            out_specs=pl.BlockSpec((COMB_TILE, d), lambda i, tb: (i, 0)),
            scratch_shapes=[pltpu.VMEM((2, N_EXPERTS * COMB_SLOW, d), BF16),
                            pltpu.SemaphoreType.DMA((2,))]),
        out_shape=jax.ShapeDtypeStruct((n, d), F32),
        compiler_params=_cparams("arbitrary"),
        name="combine",
    )(tb_flat, pos, x1, mod, g_post_ffn, ye)


def _prep_weights(w_in, b_gates, g_mlstm, na_rel_bias, w_out, w_router):
    w = MLSTM_WIDTH
    ng = 4 * MLSTM_HEADS
    w_qt = w_in[:, 0:w].T.astype(BF16)
    w_k = w_in[:, w:2 * w].astype(BF16)
    w_vt = w_in[:, 2 * w:3 * w].T.astype(BF16)
    w_o = w_in[:, 3 * w:4 * w].astype(BF16)
    perm = [d * 2 * MLSTM_HEADS + g * MLSTM_HEADS + h
            for g in range(2) for d in range(2) for h in range(MLSTM_HEADS)]
    perm = jnp.asarray(perm)
    w_g = jnp.pad(w_in[:, 4 * w:4 * w + ng][:, perm], ((0, 0), (0, LANES - ng))).astype(BF16)
    b_g = jnp.pad(b_gates.reshape(-1)[perm].astype(F32), (0, LANES - ng)).reshape(1, LANES)
    w_n = w_in[:, 4 * w + ng:].astype(BF16)
    w_om = w_out[0:w].astype(BF16)
    w_on = w_out[w:].astype(BF16)
    w_r = jnp.pad(w_router, ((0, 0), (0, LANES - N_EXPERTS))).astype(BF16)
    return dict(w_qt=w_qt, w_k=w_k, w_vt=w_vt, w_o=w_o, w_g=w_g, b_g=b_g, w_n=w_n, w_om=w_om, w_on=w_on, w_r=w_r,
                g_mlstm=g_mlstm.reshape(1, w).astype(F32), bias_table=_natten_bias_table(na_rel_bias))


def _layer(x, mod, pw, g_pre_mix, g_post_mix, g_pre_ffn, g_post_ffn, w_eg, w_eu, w_ed):
    bsz, seq, d = x.shape
    n = bsz * seq
    x2d = x.reshape(n, d)
    row = lambda g: g.reshape(1, d).astype(F32)
    qt, k, vt, o, gates, na = _inproj(x2d, mod, row(g_pre_mix), pw["w_qt"], pw["w_k"], pw["w_vt"], pw["w_o"],
                                      pw["w_g"], pw["w_n"], seq)
    grows, gcols = _gateprep(gates, pw["b_g"])
    hf, hb = _mlstm(k.reshape(bsz, seq, -1), qt, vt, grows, gcols.reshape(bsz, seq, -1))
    hn = _natten(na.reshape(bsz, seq, -1), pw["bias_table"])
    x1, x2aug, aff = _postmix(hf.reshape(n, -1), hb.reshape(n, -1), o, hn.reshape(n, -1), x2d, mod,
                              pw["g_mlstm"], pw["w_om"], pw["w_on"], row(g_post_mix), row(g_pre_ffn),
                              pw["w_r"], seq)
    cap = EC_CAPACITY_FACTOR * n // N_EXPERTS
    pos, post, tb = _select(aff, cap)
    xe = _dispatch(x2aug, post, tb, cap)
    ye = _experts(xe, w_eg, w_eu, w_ed, cap)
    y = _combine(tb, pos, x1, mod, row(g_post_ffn), ye, cap, seq)
    return y.reshape(bsz, seq, d)


def kernel(x_prompt, x_sample, c_prompt, c_sample, w_ada, b_ada, g_pre_mix, g_post_mix, w_in, b_gates, g_mlstm, na_rel_bias, w_out, g_pre_ffn, g_post_ffn, w_router, w_expert_gate, w_expert_up, w_expert_down):
    depth = w_ada.shape[0]
    nb = x_prompt.shape[0]
    xs = [x_prompt, x_sample]
    cs = jnp.concatenate([c_prompt, c_sample], axis=0)
    for l in range(depth):
        mod = _mod(cs, w_ada[l], b_ada[l])
        pw = _prep_weights(w_in[l], b_gates[l], g_mlstm[l], na_rel_bias[l], w_out[l], w_router[l])
        mods = [mod[:nb], mod[nb:]]
        xs = [_layer(x, m, pw, g_pre_mix[l], g_post_mix[l], g_pre_ffn[l], g_post_ffn[l],
                     w_expert_gate[l], w_expert_up[l], w_expert_down[l]) for x, m in zip(xs, mods)]
    return (xs[0], xs[1])
```

```python
import functools
import math

import jax
import jax.numpy as jnp
from jax import lax
from jax.experimental import pallas as pl
from jax.experimental.pallas import tpu as pltpu

F32 = jnp.float32
BF16 = jnp.bfloat16

LANES = 128
SUBLANES = 8
BF16_ROWS = 16
VMEM_LIMIT = 56 * 1024 * 1024

RMS_EPS = 1e-6
N_MOD = 6
MLSTM_HEADS = 4
MLSTM_HEAD_DIM = 128
MLSTM_CHUNK = 128
MLSTM_WIDTH = MLSTM_HEADS * MLSTM_HEAD_DIM
NA_HEAD_DIM = 64
NA_HEADS = 8
NA_WIDTH = NA_HEADS * NA_HEAD_DIM
NA_MAX_KH = 8
NA_KW = 16
GRID_W = 64
N_EXPERTS = 16
EC_CAPACITY_FACTOR = 2
NEG_BIG = -1e30


def _cparams(*sem):
    return pltpu.CompilerParams(dimension_semantics=sem, vmem_limit_bytes=VMEM_LIMIT)


def _rms(x):
    return x * lax.rsqrt(jnp.mean(x * x, axis=-1, keepdims=True) + RMS_EPS)


def _mod_kernel(c_ref, w_ref, b_ref, o_ref):
    c = c_ref[...]
    a = c * jax.nn.sigmoid(c)
    o_ref[...] = jnp.dot(a, w_ref[...], preferred_element_type=F32,
                         precision=lax.Precision.HIGHEST) + b_ref[...]


def _mod(c, w_ada, b_ada):
    bsz, d = c.shape
    n = w_ada.shape[1]
    bn = 1024
    out = pl.pallas_call(
        _mod_kernel,
        grid=(n // bn,),
        in_specs=[pl.BlockSpec((bsz, d), lambda j: (0, 0)),
                  pl.BlockSpec((d, bn), lambda j: (0, j)),
                  pl.BlockSpec((1, bn), lambda j: (0, j))],
        out_specs=pl.BlockSpec((bsz, bn), lambda j: (0, j)),
        out_shape=jax.ShapeDtypeStruct((bsz, n), F32),
        compiler_params=_cparams("arbitrary"),
        name="mod",
    )(c, w_ada, b_ada.reshape(1, n))
    return out.reshape(bsz, N_MOD, d)


def _inproj_kernel(x_ref, mod_ref, g_ref, wqt_ref, wk_ref, wvt_ref, wo_ref, wg_ref, wn_ref,
                   qt_ref, k_ref, vt_ref, o_ref, gates_ref, na_ref):
    x = x_ref[...]
    shift = mod_ref[0, 0:1, :]
    scale = mod_ref[0, 1:2, :]
    h = (_rms(x) * g_ref[...]) * (1.0 + scale) + shift
    hb = h.astype(BF16)
    nt = (((1,), (1,)), ((), ()))
    qt_ref[...] = lax.dot_general(wqt_ref[...], hb, nt, preferred_element_type=F32).astype(BF16)
    vt_ref[...] = lax.dot_general(wvt_ref[...], hb, nt, preferred_element_type=F32).astype(BF16)
    k_ref[...] = jnp.dot(hb, wk_ref[...], preferred_element_type=F32).astype(BF16)
    o_ref[...] = jnp.dot(hb, wo_ref[...], preferred_element_type=F32).astype(BF16)
    gates_ref[...] = jnp.dot(hb, wg_ref[...], preferred_element_type=F32)
    na_ref[...] = jnp.dot(hb, wn_ref[...], preferred_element_type=F32).astype(BF16)


def _inproj(x2d, mod, g_pre, w_qt, w_k, w_vt, w_o, w_g, w_n, seq, tm=512):
    n, d = x2d.shape
    tiles_per_b = seq // tm
    full = lambda a: pl.BlockSpec(a.shape, lambda i: (0, 0))
    tok = lambda w: pl.BlockSpec((tm, w.shape[1]), lambda i: (i, 0))
    feat = lambda w: pl.BlockSpec((w.shape[0], tm), lambda i: (0, i))
    return pl.pallas_call(
        _inproj_kernel,
        grid=(n // tm,),
        in_specs=[pl.BlockSpec((tm, d), lambda i: (i, 0)),
                  pl.BlockSpec((1, N_MOD, d), lambda i: (i // tiles_per_b, 0, 0)),
                  full(g_pre), full(w_qt), full(w_k), full(w_vt), full(w_o), full(w_g), full(w_n)],
        out_specs=[feat(w_qt), tok(w_k), feat(w_vt), tok(w_o), tok(w_g), tok(w_n)],
        out_shape=[jax.ShapeDtypeStruct((w_qt.shape[0], n), BF16),
                   jax.ShapeDtypeStruct((n, w_k.shape[1]), BF16),
                   jax.ShapeDtypeStruct((w_vt.shape[0], n), BF16),
                   jax.ShapeDtypeStruct((n, w_o.shape[1]), BF16),
                   jax.ShapeDtypeStruct((n, w_g.shape[1]), F32),
                   jax.ShapeDtypeStruct((n, w_n.shape[1]), BF16)],
        compiler_params=_cparams("arbitrary"),
        name="inproj",
    )(x2d, mod, g_pre, w_qt, w_k, w_vt, w_o, w_g, w_n)


N_CHAINS = 2 * MLSTM_HEADS
GROW_R, GROW_CM, GROW_B, GROW_G, GROW_RMAX = (i * N_CHAINS for i in range(5))
GROWS = 5 * N_CHAINS
GATEPREP_CHUNKS = 8
MLSTM_LOOKAHEAD = 4


def _lane_scan_max(x, lane, reverse):
    sh = 1
    while sh < LANES:
        if reverse:
            x = jnp.maximum(x, jnp.where(lane < LANES - sh, pltpu.roll(x, LANES - sh, 1), NEG_BIG))
        else:
            x = jnp.maximum(x, jnp.where(lane >= sh, pltpu.roll(x, sh, 1), NEG_BIG))
        sh *= 2
    return x


def _gateprep_kernel(g_ref, bias_ref, rows_ref, cols_ref):
    L = MLSTM_CHUNK
    row = lax.broadcasted_iota(jnp.int32, (L, L), 0)
    col = lax.broadcasted_iota(jnp.int32, (L, L), 1)
    tri = jnp.concatenate([jnp.where(row <= col, 1.0, 0.0), jnp.where(row >= col, 1.0, 0.0)], axis=1)
    crow = lax.broadcasted_iota(jnp.int32, (N_CHAINS, L), 0)
    lane = lax.broadcasted_iota(jnp.int32, (N_CHAINS, L), 1)
    is_fwd = crow < MLSTM_HEADS
    for t in range(GATEPREP_CHUNKS):
        gt = (g_ref[t * L:(t + 1) * L, :] + bias_ref[...]).T
        bb = jnp.dot(jax.nn.log_sigmoid(gt[N_CHAINS:2 * N_CHAINS]), tri, preferred_element_type=F32,
                     precision=lax.Precision.HIGHEST)
        b_f, b_b = bb[:, 0:L], bb[:, L:2 * L]
        b = jnp.where(is_fwd, b_f, b_b)
        r = gt[0:N_CHAINS] - b
        g = jnp.where(is_fwd[:, 0:1], b_f[:, L - 1:L], b_b[:, 0:1])
        cm = jnp.where(is_fwd, _lane_scan_max(r, lane, False), _lane_scan_max(r, lane, True))
        rmax = jnp.max(r, axis=1, keepdims=True)
        rows_ref[t] = jnp.concatenate([r, cm, b, jnp.broadcast_to(g, (N_CHAINS, L)),
                                       jnp.broadcast_to(rmax, (N_CHAINS, L))], axis=0)
        cols_ref[t * L:(t + 1) * L, :] = jnp.concatenate([r, jnp.zeros((L - N_CHAINS, L), F32)], axis=0).T


def _gateprep(gates, gate_bias):
    n = gates.shape[0]
    L = MLSTM_CHUNK
    tn = GATEPREP_CHUNKS * L
    return pl.pallas_call(
        _gateprep_kernel,
        grid=(n // tn,),
        in_specs=[pl.BlockSpec((tn, LANES), lambda i: (i, 0)),
                  pl.BlockSpec((1, LANES), lambda i: (0, 0))],
        out_specs=[pl.BlockSpec((GATEPREP_CHUNKS, GROWS, LANES), lambda i: (i, 0, 0)),
                   pl.BlockSpec((tn, LANES), lambda i: (i, 0))],
        out_shape=[jax.ShapeDtypeStruct((n // L, GROWS, LANES), F32),
                   jax.ShapeDtypeStruct((n, LANES), F32)],
        compiler_params=_cparams("arbitrary"),
        name="gateprep",
    )(gates, gate_bias)


def _mlstm_scores(k, qt, n_ref, ci):
    n_st = n_ref[ci]
    n_hi = n_st.astype(BF16).astype(F32)
    n_lo = (n_st - n_hi).astype(BF16).astype(F32)
    nn = jnp.concatenate([n_hi, n_lo, jnp.zeros((BF16_ROWS - 2, MLSTM_HEAD_DIM), F32)], axis=0).astype(BF16)
    return jnp.dot(k, qt, preferred_element_type=F32), jnp.dot(nn, qt, preferred_element_type=F32)


def _mlstm_chain(kq, qn2, k, qt, vt, rows, cols, mask_sj, c_ref, n_ref, m_ref, ci):
    L = MLSTM_CHUNK
    scale = MLSTM_HEAD_DIM ** -0.5
    r_row = rows[GROW_R + ci:GROW_R + ci + 1]
    cm_row = rows[GROW_CM + ci:GROW_CM + ci + 1]
    b_row = rows[GROW_B + ci:GROW_B + ci + 1]
    g = rows[GROW_G + ci:GROW_G + ci + 1, 0:1]
    rmax = rows[GROW_RMAX + ci:GROW_RMAX + ci + 1, 0:1]
    r_colb = jnp.broadcast_to(cols[:, ci:ci + 1], (L, L))
    c_st = c_ref[ci]
    n_st = n_ref[ci]
    m_st = m_ref[ci][:, 0:1]
    mm = jnp.maximum(cm_row, m_st)
    pt = jnp.where(mask_sj, jnp.exp(r_colb - mm), 0.0)
    st = kq * scale * pt
    s_int = jnp.exp(m_st - mm)
    den = jnp.sum(st, axis=0, keepdims=True) + s_int * (qn2[0:1] + qn2[1:2])
    inv = 1.0 / jnp.maximum(jnp.abs(den), jnp.exp(-(b_row + mm)))
    lhs = jnp.concatenate([vt, c_st.astype(BF16)], axis=1)
    rhs = jnp.concatenate([st.astype(BF16), (qt.astype(F32) * s_int).astype(BF16)], axis=0)
    h = (jnp.dot(lhs, rhs, preferred_element_type=F32) * inv).astype(BF16).T

    w_row = jnp.exp(r_row - rmax)
    wl = jnp.concatenate([(vt.astype(F32) * w_row).astype(BF16),
                          jnp.broadcast_to(w_row, (BF16_ROWS, L)).astype(BF16)], axis=0)
    loc = jnp.dot(wl, k, preferred_element_type=F32) * scale
    m_loc = g + rmax
    m_new = jnp.maximum(g + m_st, m_loc)
    s_old = jnp.exp(g + m_st - m_new)
    s_loc = jnp.exp(m_loc - m_new)
    c_ref[ci] = s_old * c_st + s_loc * loc[0:MLSTM_HEAD_DIM]
    n_ref[ci] = s_old * n_st + s_loc * loc[MLSTM_HEAD_DIM:MLSTM_HEAD_DIM + 1]
    m_ref[ci] = jnp.broadcast_to(m_new, (1, LANES))
    return h


def _mlstm_kernel(kf_ref, kb_ref, qtf_ref, qtb_ref, vtf_ref, vtb_ref, rf_ref, rb_ref, cf_ref, cb_ref,
                  hf_ref, hb_ref, c_ref, n_ref, m_ref):
    L = MLSTM_CHUNK

    @pl.when(pl.program_id(1) == 0)
    def _():
        c_ref[...] = jnp.zeros_like(c_ref)
        n_ref[...] = jnp.zeros_like(n_ref)
        m_ref[...] = jnp.zeros_like(m_ref)

    row = lax.broadcasted_iota(jnp.int32, (L, L), 0)
    col = lax.broadcasted_iota(jnp.int32, (L, L), 1)
    dirs = ((kf_ref, qtf_ref, vtf_ref, rf_ref, cf_ref, hf_ref, row <= col),
            (kb_ref, qtb_ref, vtb_ref, rb_ref, cb_ref, hb_ref, row >= col))

    def operands(ci):
        d, hh = divmod(ci, MLSTM_HEADS)
        sl = slice(hh * MLSTM_HEAD_DIM, (hh + 1) * MLSTM_HEAD_DIM)
        return dirs[d], sl

    def scores(ci):
        (k_ref, qt_ref, _, _, _, _, _), sl = operands(ci)
        return _mlstm_scores(k_ref[0, :, sl], qt_ref[sl, :], n_ref, ci)

    pre = [scores(ci) for ci in range(MLSTM_LOOKAHEAD)]
    for ci in range(N_CHAINS):
        if ci + MLSTM_LOOKAHEAD < N_CHAINS:
            pre.append(scores(ci + MLSTM_LOOKAHEAD))
        kq, qn2 = pre.pop(0)
        (k_ref, qt_ref, vt_ref, r_ref, cl_ref, o_ref, mask), sl = operands(ci)
        o_ref[0, :, sl] = _mlstm_chain(kq, qn2, k_ref[0, :, sl], qt_ref[sl, :], vt_ref[sl, :],
                                       r_ref[0], cl_ref[0], mask, c_ref, n_ref, m_ref, ci)


def _mlstm(k, qt, vt, grows, gcols):
    bsz, seq, _ = k.shape
    L = MLSTM_CHUNK
    nc = seq // L
    fwd = lambda b, c: (b, c, 0)
    bwd = lambda b, c: (b, nc - 1 - c, 0)
    fwd_t = lambda b, c: (0, b * nc + c)
    bwd_t = lambda b, c: (0, b * nc + nc - 1 - c)
    fwd_r = lambda b, c: (b * nc + c, 0, 0)
    bwd_r = lambda b, c: (b * nc + nc - 1 - c, 0, 0)
    return pl.pallas_call(
        _mlstm_kernel,
        grid=(bsz, nc),
        in_specs=[pl.BlockSpec((1, L, MLSTM_WIDTH), fwd), pl.BlockSpec((1, L, MLSTM_WIDTH), bwd),
                  pl.BlockSpec((MLSTM_WIDTH, L), fwd_t), pl.BlockSpec((MLSTM_WIDTH, L), bwd_t),
                  pl.BlockSpec((MLSTM_WIDTH, L), fwd_t), pl.BlockSpec((MLSTM_WIDTH, L), bwd_t),
                  pl.BlockSpec((1, GROWS, LANES), fwd_r), pl.BlockSpec((1, GROWS, LANES), bwd_r),
                  pl.BlockSpec((1, L, LANES), fwd), pl.BlockSpec((1, L, LANES), bwd)],
        out_specs=[pl.BlockSpec((1, L, MLSTM_WIDTH), fwd),
                   pl.BlockSpec((1, L, MLSTM_WIDTH), bwd)],
        out_shape=[jax.ShapeDtypeStruct((bsz, seq, MLSTM_WIDTH), BF16),
                   jax.ShapeDtypeStruct((bsz, seq, MLSTM_WIDTH), BF16)],
        scratch_shapes=[pltpu.VMEM((N_CHAINS, MLSTM_HEAD_DIM, MLSTM_HEAD_DIM), F32),
                        pltpu.VMEM((N_CHAINS, 1, MLSTM_HEAD_DIM), F32),
                        pltpu.VMEM((N_CHAINS, 1, LANES), F32)],
        compiler_params=_cparams("arbitrary", "arbitrary"),
        name="mlstm",
    )(k, k, qt, qt, vt, vt, grows, grows, gcols, gcols)


NA_QROWS = 8
NA_KROWS = 2 * NA_QROWS
NA_KBLK = 4
NA_NKBLK = NA_KROWS // NA_KBLK
NA_CODE_SECOND = 2 * NA_MAX_KH - 2
NA_CODE_FIRST = NA_CODE_SECOND + NA_MAX_KH
NA_CODE_NONE = NA_CODE_FIRST + NA_MAX_KH
NA_NCODES = NA_CODE_NONE + 1


NA_LIVE_PAIRS = NA_MAX_KH // 2 + 1


def _natten_kernel(q_ref, k0_ref, k1_ref, k2_ref, k3_ref, v0_ref, v1_ref, v2_ref, v3_ref, bias_ref, o_ref,
                   st0_ref, st1_ref, pt0_ref, pt1_ref, *, rows):
    i = pl.program_id(1)
    half = NA_MAX_KH // 2
    rq0 = i * NA_QROWS
    w0 = jnp.clip(rq0 - half, 0, rows - NA_KROWS)
    nqp = NA_QROWS // 2
    nkp = NA_KROWS // 2
    tq = NA_QROWS * GRID_W

    def window_start(r):
        return jnp.clip(r - half, 0, rows - NA_MAX_KH)

    def code(rq, kp):
        r = rq0 + rq
        r0 = window_start(r)
        kr = w0 + 2 * kp
        d0 = kr - r + (NA_MAX_KH - 1)
        va = (kr >= r0) & (kr < r0 + NA_MAX_KH)
        vb = (kr + 1 >= r0) & (kr + 1 < r0 + NA_MAX_KH)
        return jnp.where(va & vb, d0,
                         jnp.where(vb, NA_CODE_SECOND + d0 + 1,
                                   jnp.where(va, NA_CODE_FIRST + d0 - (NA_MAX_KH - 1), NA_CODE_NONE)))

    first = [jnp.minimum((window_start(rq0 + 2 * qi) - w0) // 2, nkp - NA_LIVE_PAIRS) for qi in range(nqp)]
    codes = [[(code(2 * qi, first[qi] + t), code(2 * qi + 1, first[qi] + t)) for t in range(NA_LIVE_PAIRS)]
             for qi in range(nqp)]
    lane_t = lax.broadcasted_iota(jnp.int32, (LANES, LANES), 1)
    lane_q = lax.broadcasted_iota(jnp.int32, (tq, LANES), 1)
    sub_o = lax.broadcasted_iota(jnp.int32, (LANES, tq), 0)
    scale = jnp.asarray(NA_HEAD_DIM ** -0.5, BF16)
    k_refs = (k0_ref, k1_ref, k2_ref, k3_ref)
    v_refs = (v0_ref, v1_ref, v2_ref, v3_ref)
    st_refs = (st0_ref, st1_ref)
    pt_refs = (pt0_ref, pt1_ref)
    pt0_ref[...] = jnp.zeros_like(pt0_ref)
    pt1_ref[...] = jnp.zeros_like(pt1_ref)

    def lanes_of(hp):
        return slice(hp * LANES, (hp + 1) * LANES)

    def scores(h):
        hp, sb = divmod(h, 2)
        lsl = lanes_of(hp)
        qp = q_ref[0, :, lsl] * scale
        own = (lane_q >= sb * NA_HEAD_DIM) & (lane_q < (sb + 1) * NA_HEAD_DIM)
        qm = jnp.where(own, qp, jnp.zeros_like(qp))
        k_all = jnp.concatenate([r[0, :, lsl] for r in k_refs], axis=0)
        st_refs[sb][...] = lax.dot_general(k_all, qm, (((1,), (1,)), ((), ())), preferred_element_type=F32)

    def attend(h):
        hp, sb = divmod(h, 2)
        st_ref, pt_ref = st_refs[sb], pt_refs[sb]
        invs = []
        for qi in range(nqp):
            qsl = slice(qi * LANES, (qi + 1) * LANES)
            offs = [pl.multiple_of((first[qi] + t) * LANES, LANES) for t in range(NA_LIVE_PAIRS)]
            tiles = []
            for t in range(NA_LIVE_PAIRS):
                ca, cb = codes[qi][t]
                bias = jnp.where(lane_t < GRID_W, bias_ref[h, ca], bias_ref[h, cb])
                tiles.append(st_ref[pl.ds(offs[t], LANES), qsl] + bias)
            m = functools.reduce(jnp.maximum, tiles)
            m = jnp.max(m, axis=0, keepdims=True)
            es = [jnp.exp(t - m) for t in tiles]
            l = jnp.sum(functools.reduce(jnp.add, es), axis=0, keepdims=True)
            invs.append(1.0 / l)
            for t in range(NA_LIVE_PAIRS):
                pt_ref[pl.ds(offs[t], LANES), qsl] = es[t].astype(BF16)
        vt = jnp.concatenate([r[0, :, lanes_of(hp)] for r in v_refs], axis=0).T
        return jnp.dot(vt, pt_ref[...], preferred_element_type=F32) * jnp.concatenate(invs, axis=1)

    scores(0)
    outs = []
    for h in range(NA_HEADS):
        if h + 1 < NA_HEADS:
            scores(h + 1)
        outs.append(attend(h))
        if h % 2 == 1:
            ot = jnp.where(sub_o < NA_HEAD_DIM, outs[0], outs[1])
            o_ref[0, :, lanes_of(h // 2)] = ot.T.astype(o_ref.dtype)
            outs = []


def _natten_bias_table(rel_bias):
    cols = jnp.arange(GRID_W)
    col_start = jnp.clip(cols - NA_KW // 2, 0, GRID_W - NA_KW)
    kcol = jnp.arange(GRID_W)
    dc = kcol[:, None] - cols[None, :] + (NA_KW - 1)
    inwin = (kcol[:, None] >= col_start[None, :]) & (kcol[:, None] < col_start[None, :] + NA_KW)
    tab = rel_bias.astype(F32)[:, :, jnp.clip(dc, 0, 2 * NA_KW - 2)]
    tab = jnp.where(inwin[None, None], tab, NEG_BIG)
    neg = jnp.full_like(tab[:, :NA_MAX_KH], NEG_BIG)
    both = jnp.concatenate([tab[:, :-1], tab[:, 1:]], axis=2)
    second = jnp.concatenate([neg, tab[:, :NA_MAX_KH]], axis=2)
    first = jnp.concatenate([tab[:, NA_MAX_KH - 1:], neg], axis=2)
    none = jnp.concatenate([neg[:, :1], neg[:, :1]], axis=2)
    t = jnp.concatenate([both, second, first, none], axis=1)
    return jnp.concatenate([t, t], axis=-1)


def _natten(na, bias_table):
    bsz, seq, _ = na.shape
    rows = seq // GRID_W
    tq = NA_QROWS * GRID_W
    tk = NA_KBLK * GRID_W
    nkb = seq // tk

    def kv_spec(j, col):
        return pl.BlockSpec((1, tk, NA_WIDTH),
                            lambda b, i: (b, jnp.clip(2 * i - 1, 0, nkb - NA_NKBLK) + j, col))

    return pl.pallas_call(
        functools.partial(_natten_kernel, rows=rows),
        grid=(bsz, rows // NA_QROWS),
        in_specs=[pl.BlockSpec((1, tq, NA_WIDTH), lambda b, i: (b, i, 0))]
                 + [kv_spec(j, 1) for j in range(NA_NKBLK)]
                 + [kv_spec(j, 2) for j in range(NA_NKBLK)]
                 + [pl.BlockSpec(bias_table.shape, lambda b, i: (0, 0, 0, 0), pipeline_mode=pl.Buffered(1))],
        out_specs=pl.BlockSpec((1, tq, NA_WIDTH), lambda b, i: (b, i, 0)),
        out_shape=jax.ShapeDtypeStruct((bsz, seq, NA_WIDTH), BF16),
        scratch_shapes=[pltpu.VMEM((NA_KROWS * GRID_W, tq), F32), pltpu.VMEM((NA_KROWS * GRID_W, tq), F32),
                        pltpu.VMEM((NA_KROWS * GRID_W, tq), BF16), pltpu.VMEM((NA_KROWS * GRID_W, tq), BF16)],
        compiler_params=_cparams("arbitrary", "arbitrary"),
        name="natten",
    )(na, *([na] * (2 * NA_NKBLK)), bias_table)


AUG = LANES
GATE_LO_SHIFT = N_EXPERTS


def _postmix_kernel(hf_ref, hb_ref, o_ref, hn_ref, x_ref, mod_ref, gm_ref, wm_ref, wn_ref,
                    gpost_ref, gffn_ref, wr_ref, x1_ref, x2_ref, aff_ref):
    d = x_ref.shape[-1]
    s = hf_ref[...].astype(F32) + hb_ref[...].astype(F32)
    og = jax.nn.sigmoid(o_ref[...].astype(F32))
    heads = []
    for hh in range(MLSTM_HEADS):
        sl = slice(hh * MLSTM_HEAD_DIM, (hh + 1) * MLSTM_HEAD_DIM)
        heads.append(_rms(s[:, sl]) * gm_ref[:, sl] * og[:, sl])
    hm = jnp.concatenate(heads, axis=-1).astype(BF16)
    mix = (jnp.dot(hm, wm_ref[...], preferred_element_type=F32)
           + jnp.dot(hn_ref[...], wn_ref[...], preferred_element_type=F32))
    gate1 = mod_ref[0, 2:3, :]
    shift2 = mod_ref[0, 3:4, :]
    scale2 = mod_ref[0, 4:5, :]
    x1 = x_ref[...] + gate1 * (_rms(mix) * gpost_ref[...])
    x1_ref[...] = x1
    h2 = (_rms(x1) * gffn_ref[...]) * (1.0 + scale2) + shift2
    h2b = h2.astype(BF16)
    x2_ref[:, 0:d] = h2b
    logits = jnp.dot(h2b, wr_ref[...], preferred_element_type=F32)
    lane = lax.broadcasted_iota(jnp.int32, logits.shape, 1)
    logits = jnp.where(lane < N_EXPERTS, logits, NEG_BIG)
    ex = jnp.exp(logits - jnp.max(logits, axis=-1, keepdims=True))
    aff = ex / jnp.sum(ex, axis=-1, keepdims=True)
    aff_ref[...] = aff
    hi = aff.astype(BF16).astype(F32)
    lo = (aff - hi).astype(BF16).astype(F32)
    x2_ref[:, d:d + AUG] = (hi + pltpu.roll(lo, GATE_LO_SHIFT, 1)).astype(BF16)


def _postmix(hf, hb, o, hn, x2d, mod, g_mlstm, w_om, w_on, g_post, g_ffn, w_r, seq, tm=512):
    n, d = x2d.shape
    tiles_per_b = seq // tm
    full = lambda a: pl.BlockSpec(a.shape, lambda i: (0, 0))
    tok = lambda w: pl.BlockSpec((tm, w), lambda i: (i, 0))
    return pl.pallas_call(
        _postmix_kernel,
        grid=(n // tm,),
        in_specs=[tok(MLSTM_WIDTH), tok(MLSTM_WIDTH), tok(MLSTM_WIDTH), tok(NA_WIDTH), tok(d),
                  pl.BlockSpec((1, N_MOD, d), lambda i: (i // tiles_per_b, 0, 0)),
                  full(g_mlstm), full(w_om), full(w_on), full(g_post), full(g_ffn), full(w_r)],
        out_specs=[tok(d), tok(d + AUG), tok(LANES)],
        out_shape=[jax.ShapeDtypeStruct((n, d), F32),
                   jax.ShapeDtypeStruct((n, d + AUG), BF16),
                   jax.ShapeDtypeStruct((n, LANES), F32)],
        compiler_params=_cparams("arbitrary"),
        name="postmix",
    )(hf, hb, o, hn, x2d, mod, g_mlstm, w_om, w_on, g_post, g_ffn, w_r)


TOK_TILE = 128
SEL_STEP = 1024
UNSELECTED = -1e6
SELECT_HI = 2.0
SELECT_ITERS = 160


def _select_kernel(afft_ref, aff_ref, pos_ref, post_ref, tb_ref, thr_ref, need_ref, base_ref, *, cap):
    step = pl.program_id(0)
    ne = afft_ref.shape[0]

    @pl.when(step == 0)
    def _():
        a_t = afft_ref[...]

        def count_ge(v):
            return jnp.sum(jnp.where(a_t >= v, 1.0, 0.0), axis=1, keepdims=True)

        def body(it, lh):
            lo, hi = lh
            mid = 0.5 * (lo + hi)
            ge = count_ge(mid) >= cap
            return jnp.where(ge, mid, lo), jnp.where(ge, hi, mid)

        lo_col, hi_col = lax.fori_loop(0, SELECT_ITERS, body,
                                       (jnp.zeros((ne, 1), F32), jnp.full((ne, 1), SELECT_HI, F32)))
        need_col = cap - count_ge(hi_col)
        sub = lax.broadcasted_iota(jnp.int32, (ne, LANES), 0)
        ln = lax.broadcasted_iota(jnp.int32, (ne, LANES), 1)
        diag = sub == ln
        thr_ref[...] = jnp.sum(jnp.where(diag, lo_col, 0.0), axis=0, keepdims=True)
        need_ref[...] = jnp.sum(jnp.where(diag, need_col, 0.0), axis=0, keepdims=True)
        base_ref[...] = jnp.zeros_like(base_ref)
        base_ref[2:3, :] = jnp.sum(jnp.where(diag, hi_col, 0.0), axis=0, keepdims=True)

    thr_lo = thr_ref[...]
    thr_hi = base_ref[2:3, :]
    need = need_ref[...]
    lane = lax.broadcasted_iota(jnp.int32, (TOK_TILE, LANES), 1)
    valid = lane < ne
    row = lax.broadcasted_iota(jnp.int32, (TOK_TILE, TOK_TILE), 0)
    col = lax.broadcasted_iota(jnp.int32, (TOK_TILE, TOK_TILE), 1)
    strict_lower = (col < row).astype(BF16)
    base_gt = base_ref[0:1, :]
    base_eq = base_ref[1:2, :]
    tb_rows = []
    for t in range(SEL_STEP // TOK_TILE):
        a = aff_ref[t * TOK_TILE:(t + 1) * TOK_TILE, :]
        gt = (a >= thr_hi) & valid
        eq = (a >= thr_lo) & (a < thr_hi) & valid
        gtf = jnp.where(gt, 1.0, 0.0)
        eqf = jnp.where(eq, 1.0, 0.0)
        cg = jnp.dot(strict_lower, gtf.astype(BF16), preferred_element_type=F32)
        ce = jnp.dot(strict_lower, eqf.astype(BF16), preferred_element_type=F32)
        eq_rank = base_eq + ce
        sel = gt | (eq & (eq_rank < need))
        pos = base_gt + cg + jnp.minimum(eq_rank, need)
        posm = jnp.where(sel, pos, UNSELECTED)
        pos_ref[t * TOK_TILE:(t + 1) * TOK_TILE, :] = posm
        post_ref[:, t * TOK_TILE:(t + 1) * TOK_TILE] = posm.T[0:ne, :]
        tb_rows.append(base_gt + jnp.minimum(base_eq, need))
        base_gt = base_gt + jnp.sum(gtf, axis=0, keepdims=True)
        base_eq = base_eq + jnp.sum(eqf, axis=0, keepdims=True)
    tb_ref[...] = jnp.concatenate(tb_rows, axis=0).astype(jnp.int32)
    base_ref[0:1, :] = base_gt
    base_ref[1:2, :] = base_eq


def _select(aff, cap):
    n = aff.shape[0]
    afft = aff[:, :N_EXPERTS].T
    nsteps = n // SEL_STEP
    tiles_per_step = SEL_STEP // TOK_TILE
    return pl.pallas_call(
        functools.partial(_select_kernel, cap=cap),
        grid=(nsteps,),
        in_specs=[pl.BlockSpec((N_EXPERTS, n), lambda i: (0, 0)),
                  pl.BlockSpec((SEL_STEP, LANES), lambda i: (i, 0))],
        out_specs=[pl.BlockSpec((SEL_STEP, LANES), lambda i: (i, 0)),
                   pl.BlockSpec((N_EXPERTS, SEL_STEP), lambda i: (0, i)),
                   pl.BlockSpec((tiles_per_step, LANES), lambda i: (i, 0))],
        out_shape=[jax.ShapeDtypeStruct((n, LANES), F32),
                   jax.ShapeDtypeStruct((N_EXPERTS, n), F32),
                   jax.ShapeDtypeStruct((n // TOK_TILE, LANES), jnp.int32)],
        scratch_shapes=[pltpu.VMEM((1, LANES), F32), pltpu.VMEM((1, LANES), F32),
                        pltpu.VMEM((SUBLANES, LANES), F32)],
        compiler_params=_cparams("arbitrary"),
        name="select",
    )(afft, aff)


DISP_TILE = 4 * TOK_TILE
CARRY = BF16_ROWS
DISP_FAST = 128
DISP_SLOW = CARRY + DISP_TILE + CARRY
XE_PAD = 2 * DISP_TILE
DISP_SLOW_GROUP = 4
DISP_STAGE_ROWS = max(N_EXPERTS * DISP_FAST, DISP_SLOW_GROUP * DISP_SLOW)


def _dispatch_kernel(tb_ref, x_ref, post_ref, xe_ref, stage_ref, carry_ref, sem, flag_ref, *, cap, ntiles):
    i = pl.program_id(0)
    slot = lax.rem(i, 2)
    ne = N_EXPERTS

    @pl.when(i == 0)
    def _():
        carry_ref[...] = jnp.zeros_like(carry_ref)
        stage_ref[1, 0:XE_PAD, :] = jnp.zeros((XE_PAD, stage_ref.shape[-1]), stage_ref.dtype)
        pads = [pltpu.make_async_copy(stage_ref.at[1, pl.ds(0, XE_PAD)], xe_ref.at[e, pl.ds(cap, XE_PAD)],
                                      sem.at[1]) for e in range(ne)]
        for cp in pads:
            cp.start()
        for cp in pads:
            cp.wait()

    fill = [tb_ref[i * ne + e] for e in range(ne)]
    nxt = [jnp.where(i + 1 < ntiles, tb_ref[jnp.minimum(i + 1, ntiles - 1) * ne + e], cap) for e in range(ne)]
    al = [(f // CARRY) * CARRY for f in fill]
    need_max = functools.reduce(jnp.maximum, [nx - a for nx, a in zip(nxt, al)])
    slow = need_max > DISP_FAST

    def copies(s_rows, sl, experts, als):
        return [pltpu.make_async_copy(stage_ref.at[sl, pl.ds(k * s_rows, s_rows)],
                                      xe_ref.at[e, pl.ds(pl.multiple_of(als[e], CARRY), s_rows)],
                                      sem.at[sl]) for k, e in enumerate(experts)]

    def wait_in_flight(sl):
        @pl.when(flag_ref[sl] == 0)
        def _():
            for cp in copies(DISP_FAST, sl, range(ne), [0] * ne):
                cp.wait()

    def stage(s_rows, experts, xt, cb):
        slot_of_tok = lax.broadcasted_iota(jnp.int32, (s_rows, DISP_TILE), 0).astype(F32)
        sub = lax.broadcasted_iota(jnp.int32, (s_rows, ne * CARRY), 0)
        ln = lax.broadcasted_iota(jnp.int32, (s_rows, ne * CARRY), 1)
        a_tok, a_car = [], []
        for e in experts:
            rel = post_ref[e:e + 1, :] - al[e].astype(F32)
            a_tok.append(jnp.where(slot_of_tok == rel, 1.0, 0.0))
            own = (ln >= e * CARRY) & (ln < e * CARRY + (fill[e] - al[e])) & (ln - e * CARRY == sub)
            a_car.append(jnp.where(own, 1.0, 0.0))
        stage_ref[slot, 0:len(experts) * s_rows, :] = (
            jnp.dot(jnp.concatenate(a_tok, axis=0).astype(BF16), xt, preferred_element_type=F32)
            + jnp.dot(jnp.concatenate(a_car, axis=0).astype(BF16), cb, preferred_element_type=F32)
        ).astype(stage_ref.dtype)
        for k, e in enumerate(experts):
            off = jnp.minimum((nxt[e] // CARRY) * CARRY - al[e], s_rows - CARRY)
            carry_ref[e * CARRY:(e + 1) * CARRY, :] = stage_ref[slot, pl.ds(pl.multiple_of(k * s_rows + off, CARRY), CARRY), :]

    @pl.when(jnp.logical_not(slow))
    def _():
        stage(DISP_FAST, range(ne), x_ref[...], carry_ref[...].astype(BF16))

        @pl.when(i > 0)
        def _():
            wait_in_flight(1 - slot)
        for cp in copies(DISP_FAST, slot, range(ne), al):
            cp.start()
        flag_ref[slot] = 0

    @pl.when(slow)
    def _():
        xt = x_ref[...]
        cb = carry_ref[...].astype(BF16)

        @pl.when(i > 0)
        def _():
            wait_in_flight(1 - slot)
        for e0 in range(0, ne, DISP_SLOW_GROUP):
            experts = range(e0, e0 + DISP_SLOW_GROUP)
            stage(DISP_SLOW, experts, xt, cb)
            cps = copies(DISP_SLOW, slot, experts, al)
            for cp in cps:
                cp.start()
            for cp in cps:
                cp.wait()
        flag_ref[slot] = 1

    @pl.when(i == ntiles - 1)
    def _():
        wait_in_flight(slot)


def _dispatch(x2aug, post, tb, cap):
    n, wd = x2aug.shape
    ntiles = n // DISP_TILE
    tb_flat = tb[::DISP_TILE // TOK_TILE, :N_EXPERTS].reshape(-1)
    return pl.pallas_call(
        functools.partial(_dispatch_kernel, cap=cap, ntiles=ntiles),
        grid_spec=pltpu.PrefetchScalarGridSpec(
            num_scalar_prefetch=1,
            grid=(ntiles,),
            in_specs=[pl.BlockSpec((DISP_TILE, wd), lambda i, tb: (i, 0)),
                      pl.BlockSpec((N_EXPERTS, DISP_TILE), lambda i, tb: (0, i))],
            out_specs=pl.BlockSpec(memory_space=pl.ANY),
            scratch_shapes=[pltpu.VMEM((2, DISP_STAGE_ROWS, wd), BF16),
                            pltpu.VMEM((N_EXPERTS * CARRY, wd), BF16),
                            pltpu.SemaphoreType.DMA((2,)),
                            pltpu.SMEM((2,), jnp.int32)]),
        out_shape=jax.ShapeDtypeStruct((N_EXPERTS, cap + XE_PAD, wd), BF16),
        compiler_params=_cparams("arbitrary"),
        name="dispatch",
    )(tb_flat, x2aug, post)


EXP_ROWS = 512


def _experts_kernel(xe_ref, wg_ref, wu_ref, wd_ref, ye_ref, wgb_ref, wub_ref, wdb_ref):
    e = pl.program_id(0)
    d = wg_ref.shape[1]

    @pl.when(pl.program_id(1) == 0)
    def _():
        wgb_ref[...] = wg_ref[0].astype(BF16)
        wub_ref[...] = wu_ref[0].astype(BF16)
        wdb_ref[...] = wd_ref[0].astype(BF16)

    x = xe_ref[0, :, 0:d]
    aug = xe_ref[0, :, d:d + AUG].astype(F32)
    lane = lax.broadcasted_iota(jnp.int32, aug.shape, 1)
    mine = (lane == e) | (lane == e + GATE_LO_SHIFT)
    gate = jnp.sum(jnp.where(mine, aug, 0.0), axis=-1, keepdims=True)
    g = jnp.dot(x, wgb_ref[...], preferred_element_type=F32)
    u = jnp.dot(x, wub_ref[...], preferred_element_type=F32)
    hid = (g * jax.nn.sigmoid(g)) * u
    y = jnp.dot(hid.astype(BF16), wdb_ref[...], preferred_element_type=F32)
    ye_ref[0] = (y * gate).astype(ye_ref.dtype)


def _experts(xe, w_gate, w_up, w_down, cap):
    ne, _, wd = xe.shape
    d = w_gate.shape[1]
    f = w_gate.shape[2]
    wspec = lambda s: pl.BlockSpec((1,) + s, lambda e, j: (e, 0, 0))
    rows = min(EXP_ROWS, cap)
    return pl.pallas_call(
        _experts_kernel,
        grid=(ne, cap // rows),
        in_specs=[pl.BlockSpec((1, rows, wd), lambda e, j: (e, j, 0)),
                  wspec((d, f)), wspec((d, f)), wspec((f, d))],
        out_specs=pl.BlockSpec((1, rows, d), lambda e, j: (e, j, 0)),
        out_shape=jax.ShapeDtypeStruct((ne, cap, d), BF16),
        scratch_shapes=[pltpu.VMEM((d, f), BF16), pltpu.VMEM((d, f), BF16), pltpu.VMEM((f, d), BF16)],
        compiler_params=_cparams("arbitrary", "arbitrary"),
        name="experts",
    )(xe, w_gate, w_up, w_down)


COMB_TILE = 2 * TOK_TILE
COMB_FAST = 128
COMB_SLOW = 384
COMB_ALIGN = BF16_ROWS


def _combine_kernel(tb_ref, pos_ref, x1_ref, mod_ref, g_ref, ye_ref, y_ref, slab_ref, sem, *, cap, ntiles):
    i = pl.program_id(0)
    slot = lax.rem(i, 2)
    ne = N_EXPERTS

    def info(t):
        t = jnp.minimum(t, ntiles - 1)
        fill = [tb_ref[t * ne + e] for e in range(ne)]
        nxt = [jnp.where(t + 1 < ntiles, tb_ref[jnp.minimum(t + 1, ntiles - 1) * ne + e], cap) for e in range(ne)]
        st_f = [jnp.minimum((f // COMB_ALIGN) * COMB_ALIGN, cap - COMB_FAST) for f in fill]
        st_s = [jnp.minimum((f // COMB_ALIGN) * COMB_ALIGN, cap - COMB_SLOW) for f in fill]
        slow = functools.reduce(jnp.maximum, [nx - s for nx, s in zip(nxt, st_f)]) > COMB_FAST
        return st_f, st_s, slow

    def copies(s_rows, sl, starts):
        return [pltpu.make_async_copy(ye_ref.at[e, pl.ds(pl.multiple_of(starts[e], COMB_ALIGN), s_rows)],
                                      slab_ref.at[sl, pl.ds(e * s_rows, s_rows)],
                                      sem.at[sl]) for e in range(ne)]

    def fetch(t, sl):
        st_f, st_s, slow = info(t)

        @pl.when(slow)
        def _():
            for cp in copies(COMB_SLOW, sl, st_s):
                cp.start()

        @pl.when(jnp.logical_not(slow))
        def _():
            for cp in copies(COMB_FAST, sl, st_f):
                cp.start()

    @pl.when(i == 0)
    def _():
        fetch(i, slot)

    @pl.when(i + 1 < ntiles)
    def _():
        fetch(i + 1, 1 - slot)

    st_f, st_s, slow = info(i)
    lane = lax.broadcasted_iota(jnp.int32, (COMB_TILE, LANES), 1).astype(F32)
    pos = pos_ref[...]

    def finish(moe):
        gate2 = mod_ref[0, 5:6, :]
        y_ref[...] = x1_ref[...] + gate2 * (_rms(moe) * g_ref[...])

    @pl.when(jnp.logical_not(slow))
    def _():
        for cp in copies(COMB_FAST, slot, st_f):
            cp.wait()
        per = LANES // COMB_FAST
        groups = []
        for gi in range(ne // per):
            val = None
            for k in range(per):
                e = gi * per + k
                rel = pos[:, e:e + 1] - st_f[e].astype(F32) + float(k * COMB_FAST)
                inband = (lane >= k * COMB_FAST) & (lane < (k + 1) * COMB_FAST)
                hit = inband & (lane == rel)
                val = hit if val is None else (val | hit)
            groups.append(jnp.where(val, 1.0, 0.0).astype(BF16))
        onehot = jnp.concatenate(groups, axis=1)
        finish(jnp.dot(onehot, slab_ref[slot, 0:ne * COMB_FAST, :], preferred_element_type=F32))

    @pl.when(slow)
    def _():
        for cp in copies(COMB_SLOW, slot, st_s):
            cp.wait()
        per = COMB_SLOW // LANES
        groups = []
        for e in range(ne):
            rel = pos[:, e:e + 1] - st_s[e].astype(F32)
            for k in range(per):
                groups.append(jnp.where(lane + float(k * LANES) == rel, 1.0, 0.0).astype(BF16))
        onehot = jnp.concatenate(groups, axis=1)
        finish(jnp.dot(onehot, slab_ref[slot], preferred_element_type=F32))


def _combine(tb, pos, x1, mod, g_post_ffn, ye, cap, seq):
    n, d = x1.shape
    assert cap >= COMB_SLOW
    ntiles = n // COMB_TILE
    tiles_per_b = seq // COMB_TILE
    tb_flat = tb[::COMB_TILE // TOK_TILE, :N_EXPERTS].reshape(-1)
    return pl.pallas_call(
        functools.partial(_combine_kernel, cap=cap, ntiles=ntiles),
        grid_spec=pltpu.PrefetchScalarGridSpec(
            num_scalar_prefetch=1,
            grid=(ntiles,),
            in_specs=[pl.BlockSpec((COMB_TILE, LANES), lambda i, tb: (i, 0)),
                      pl.BlockSpec((COMB_TILE, d), lambda i, tb: (i, 0)),
                      pl.BlockSpec((1, N_MOD, d), lambda i, tb: (i // tiles_per_b, 0, 0)),
                      pl.BlockSpec((1, d), lambda i, tb: (0, 0)),
                      pl.BlockSpec(memory_space=pl.ANY)],
            out_specs=pl.BlockSpec((COMB_TILE, d), lambda i, tb: (i, 0)),
            scratch_shapes=[pltpu.VMEM((2, N_EXPERTS * COMB_SLOW, d), BF16),
                            pltpu.SemaphoreType.DMA((2,))]),
        out_shape=jax.ShapeDtypeStruct((n, d), F32),
        compiler_params=_cparams("arbitrary"),
        name="combine",
    )(tb_flat, pos, x1, mod, g_post_ffn, ye)


def _prep_weights(w_in, b_gates, g_mlstm, na_rel_bias, w_out, w_router):
    w = MLSTM_WIDTH
    ng = 4 * MLSTM_HEADS
    w_qt = w_in[:, 0:w].T.astype(BF16)
    w_k = w_in[:, w:2 * w].astype(BF16)
    w_vt = w_in[:, 2 * w:3 * w].T.astype(BF16)
    w_o = w_in[:, 3 * w:4 * w].astype(BF16)
    perm = [d * 2 * MLSTM_HEADS + g * MLSTM_HEADS + h
            for g in range(2) for d in range(2) for h in range(MLSTM_HEADS)]
    perm = jnp.asarray(perm)
    w_g = jnp.pad(w_in[:, 4 * w:4 * w + ng][:, perm], ((0, 0), (0, LANES - ng))).astype(BF16)
    b_g = jnp.pad(b_gates.reshape(-1)[perm].astype(F32), (0, LANES - ng)).reshape(1, LANES)
    w_n = w_in[:, 4 * w + ng:].astype(BF16)
    w_om = w_out[0:w].astype(BF16)
    w_on = w_out[w:].astype(BF16)
    w_r = jnp.pad(w_router, ((0, 0), (0, LANES - N_EXPERTS))).astype(BF16)
    return dict(w_qt=w_qt, w_k=w_k, w_vt=w_vt, w_o=w_o, w_g=w_g, b_g=b_g, w_n=w_n, w_om=w_om, w_on=w_on, w_r=w_r,
                g_mlstm=g_mlstm.reshape(1, w).astype(F32), bias_table=_natten_bias_table(na_rel_bias))


def _layer(x, mod, pw, g_pre_mix, g_post_mix, g_pre_ffn, g_post_ffn, w_eg, w_eu, w_ed):
    bsz, seq, d = x.shape
    n = bsz * seq
    x2d = x.reshape(n, d)
    row = lambda g: g.reshape(1, d).astype(F32)
    qt, k, vt, o, gates, na = _inproj(x2d, mod, row(g_pre_mix), pw["w_qt"], pw["w_k"], pw["w_vt"], pw["w_o"],
                                      pw["w_g"], pw["w_n"], seq)
    grows, gcols = _gateprep(gates, pw["b_g"])
    hf, hb = _mlstm(k.reshape(bsz, seq, -1), qt, vt, grows, gcols.reshape(bsz, seq, -1))
    hn = _natten(na.reshape(bsz, seq, -1), pw["bias_table"])
    x1, x2aug, aff = _postmix(hf.reshape(n, -1), hb.reshape(n, -1), o, hn.reshape(n, -1), x2d, mod,
                              pw["g_mlstm"], pw["w_om"], pw["w_on"], row(g_post_mix), row(g_pre_ffn),
                              pw["w_r"], seq)
    cap = EC_CAPACITY_FACTOR * n // N_EXPERTS
    pos, post, tb = _select(aff, cap)
    xe = _dispatch(x2aug, post, tb, cap)
    ye = _experts(xe, w_eg, w_eu, w_ed, cap)
    y = _combine(tb, pos, x1, mod, row(g_post_ffn), ye, cap, seq)
    return y.reshape(bsz, seq, d)


def kernel(x_prompt, x_sample, c_prompt, c_sample, w_ada, b_ada, g_pre_mix, g_post_mix, w_in, b_gates, g_mlstm, na_rel_bias, w_out, g_pre_ffn, g_post_ffn, w_router, w_expert_gate, w_expert_up, w_expert_down):
    depth = w_ada.shape[0]
    nb = x_prompt.shape[0]
    xs = [x_prompt, x_sample]
    cs = jnp.concatenate([c_prompt, c_sample], axis=0)
    for l in range(depth):
        mod = _mod(cs, w_ada[l], b_ada[l])
        pw = _prep_weights(w_in[l], b_gates[l], g_mlstm[l], na_rel_bias[l], w_out[l], w_router[l])
        mods = [mod[:nb], mod[nb:]]
        xs = [_layer(x, m, pw, g_pre_mix[l], g_post_mix[l], g_pre_ffn[l], g_post_ffn[l],
                     w_expert_gate[l], w_expert_up[l], w_expert_down[l]) for x, m in zip(xs, mods)]
    return (xs[0], xs[1])
```

```python
import functools
import math

import jax
import jax.numpy as jnp
from jax import lax
from jax.experimental import pallas as pl
from jax.experimental.pallas import tpu as pltpu

F32 = jnp.float32
BF16 = jnp.bfloat16

LANES = 128
SUBLANES = 8
BF16_ROWS = 16
VMEM_LIMIT = 56 * 1024 * 1024

RMS_EPS = 1e-6
N_MOD = 6
MLSTM_HEADS = 4
MLSTM_HEAD_DIM = 128
MLSTM_CHUNK = 128
MLSTM_WIDTH = MLSTM_HEADS * MLSTM_HEAD_DIM
NA_HEAD_DIM = 64
NA_HEADS = 8
NA_WIDTH = NA_HEADS * NA_HEAD_DIM
NA_MAX_KH = 8
NA_KW = 16
GRID_W = 64
N_EXPERTS = 16
EC_CAPACITY_FACTOR = 2
NEG_BIG = -1e30


def _cparams(*sem):
    return pltpu.CompilerParams(dimension_semantics=sem, vmem_limit_bytes=VMEM_LIMIT)


def _rms(x):
    return x * lax.rsqrt(jnp.mean(x * x, axis=-1, keepdims=True) + RMS_EPS)


def _mod_kernel(c_ref, w_ref, b_ref, o_ref):
    c = c_ref[...]
    a = c * jax.nn.sigmoid(c)
    o_ref[...] = jnp.dot(a, w_ref[...], preferred_element_type=F32,
                         precision=lax.Precision.HIGHEST) + b_ref[...]


def _mod(c, w_ada, b_ada):
    bsz, d = c.shape
    n = w_ada.shape[1]
    bn = 1024
    out = pl.pallas_call(
        _mod_kernel,
        grid=(n // bn,),
        in_specs=[pl.BlockSpec((bsz, d), lambda j: (0, 0)),
                  pl.BlockSpec((d, bn), lambda j: (0, j)),
                  pl.BlockSpec((1, bn), lambda j: (0, j))],
        out_specs=pl.BlockSpec((bsz, bn), lambda j: (0, j)),
        out_shape=jax.ShapeDtypeStruct((bsz, n), F32),
        compiler_params=_cparams("arbitrary"),
        name="mod",
    )(c, w_ada, b_ada.reshape(1, n))
    return out.reshape(bsz, N_MOD, d)


def _inproj_kernel(x_ref, mod_ref, g_ref, wqt_ref, wk_ref, wvt_ref, wo_ref, wg_ref, wn_ref,
                   qt_ref, k_ref, vt_ref, o_ref, gates_ref, na_ref):
    x = x_ref[...]
    shift = mod_ref[0, 0:1, :]
    scale = mod_ref[0, 1:2, :]
    h = (_rms(x) * g_ref[...]) * (1.0 + scale) + shift
    hb = h.astype(BF16)
    nt = (((1,), (1,)), ((), ()))
    qt_ref[...] = lax.dot_general(wqt_ref[...], hb, nt, preferred_element_type=F32).astype(BF16)
    vt_ref[...] = lax.dot_general(wvt_ref[...], hb, nt, preferred_element_type=F32).astype(BF16)
    k_ref[...] = jnp.dot(hb, wk_ref[...], preferred_element_type=F32).astype(BF16)
    o_ref[...] = jnp.dot(hb, wo_ref[...], preferred_element_type=F32).astype(BF16)
    gates_ref[...] = jnp.dot(hb, wg_ref[...], preferred_element_type=F32)
    na_ref[...] = jnp.dot(hb, wn_ref[...], preferred_element_type=F32).astype(BF16)


def _inproj(x2d, mod, g_pre, w_qt, w_k, w_vt, w_o, w_g, w_n, seq, tm=512):
    n, d = x2d.shape
    tiles_per_b = seq // tm
    full = lambda a: pl.BlockSpec(a.shape, lambda i: (0, 0))
    tok = lambda w: pl.BlockSpec((tm, w.shape[1]), lambda i: (i, 0))
    feat = lambda w: pl.BlockSpec((w.shape[0], tm), lambda i: (0, i))
    return pl.pallas_call(
        _inproj_kernel,
        grid=(n // tm,),
        in_specs=[pl.BlockSpec((tm, d), lambda i: (i, 0)),
                  pl.BlockSpec((1, N_MOD, d), lambda i: (i // tiles_per_b, 0, 0)),
                  full(g_pre), full(w_qt), full(w_k), full(w_vt), full(w_o), full(w_g), full(w_n)],
        out_specs=[feat(w_qt), tok(w_k), feat(w_vt), tok(w_o), tok(w_g), tok(w_n)],
        out_shape=[jax.ShapeDtypeStruct((w_qt.shape[0], n), BF16),
                   jax.ShapeDtypeStruct((n, w_k.shape[1]), BF16),
                   jax.ShapeDtypeStruct((w_vt.shape[0], n), BF16),
                   jax.ShapeDtypeStruct((n, w_o.shape[1]), BF16),
                   jax.ShapeDtypeStruct((n, w_g.shape[1]), F32),
                   jax.ShapeDtypeStruct((n, w_n.shape[1]), BF16)],
        compiler_params=_cparams("arbitrary"),
        name="inproj",
    )(x2d, mod, g_pre, w_qt, w_k, w_vt, w_o, w_g, w_n)


N_CHAINS = 2 * MLSTM_HEADS
GROW_R, GROW_CM, GROW_B, GROW_G, GROW_RMAX = (i * N_CHAINS for i in range(5))
GROWS = 5 * N_CHAINS
GATEPREP_CHUNKS = 8
MLSTM_LOOKAHEAD = 4


def _lane_scan_max(x, lane, reverse):
    sh = 1
    while sh < LANES:
        if reverse:
            x = jnp.maximum(x, jnp.where(lane < LANES - sh, pltpu.roll(x, LANES - sh, 1), NEG_BIG))
        else:
            x = jnp.maximum(x, jnp.where(lane >= sh, pltpu.roll(x, sh, 1), NEG_BIG))
        sh *= 2
    return x


def _gateprep_kernel(g_ref, bias_ref, rows_ref, cols_ref):
    L = MLSTM_CHUNK
    row = lax.broadcasted_iota(jnp.int32, (L, L), 0)
    col = lax.broadcasted_iota(jnp.int32, (L, L), 1)
    tri = jnp.concatenate([jnp.where(row <= col, 1.0, 0.0), jnp.where(row >= col, 1.0, 0.0)], axis=1)
    crow = lax.broadcasted_iota(jnp.int32, (N_CHAINS, L), 0)
    lane = lax.broadcasted_iota(jnp.int32, (N_CHAINS, L), 1)
    is_fwd = crow < MLSTM_HEADS
    for t in range(GATEPREP_CHUNKS):
        gt = (g_ref[t * L:(t + 1) * L, :] + bias_ref[...]).T
        bb = jnp.dot(jax.nn.log_sigmoid(gt[N_CHAINS:2 * N_CHAINS]), tri, preferred_element_type=F32,
                     precision=lax.Precision.HIGHEST)
        b_f, b_b = bb[:, 0:L], bb[:, L:2 * L]
        b = jnp.where(is_fwd, b_f, b_b)
        r = gt[0:N_CHAINS] - b
        g = jnp.where(is_fwd[:, 0:1], b_f[:, L - 1:L], b_b[:, 0:1])
        cm = jnp.where(is_fwd, _lane_scan_max(r, lane, False), _lane_scan_max(r, lane, True))
        rmax = jnp.max(r, axis=1, keepdims=True)
        rows_ref[t] = jnp.concatenate([r, cm, b, jnp.broadcast_to(g, (N_CHAINS, L)),
                                       jnp.broadcast_to(rmax, (N_CHAINS, L))], axis=0)
        cols_ref[t * L:(t + 1) * L, :] = jnp.concatenate([r, jnp.zeros((L - N_CHAINS, L), F32)], axis=0).T


def _gateprep(gates, gate_bias):
    n = gates.shape[0]
    L = MLSTM_CHUNK
    tn = GATEPREP_CHUNKS * L
    return pl.pallas_call(
        _gateprep_kernel,
        grid=(n // tn,),
        in_specs=[pl.BlockSpec((tn, LANES), lambda i: (i, 0)),
                  pl.BlockSpec((1, LANES), lambda i: (0, 0))],
        out_specs=[pl.BlockSpec((GATEPREP_CHUNKS, GROWS, LANES), lambda i: (i, 0, 0)),
                   pl.BlockSpec((tn, LANES), lambda i: (i, 0))],
        out_shape=[jax.ShapeDtypeStruct((n // L, GROWS, LANES), F32),
                   jax.ShapeDtypeStruct((n, LANES), F32)],
        compiler_params=_cparams("arbitrary"),
        name="gateprep",
    )(gates, gate_bias)


def _mlstm_scores(k, qt, n_ref, ci):
    n_st = n_ref[ci]
    n_hi = n_st.astype(BF16).astype(F32)
    n_lo = (n_st - n_hi).astype(BF16).astype(F32)
    nn = jnp.concatenate([n_hi, n_lo, jnp.zeros((BF16_ROWS - 2, MLSTM_HEAD_DIM), F32)], axis=0).astype(BF16)
    return jnp.dot(k, qt, preferred_element_type=F32), jnp.dot(nn, qt, preferred_element_type=F32)


def _mlstm_chain(kq, qn2, k, qt, vt, rows, cols, mask_sj, c_ref, n_ref, m_ref, ci):
    L = MLSTM_CHUNK
    scale = MLSTM_HEAD_DIM ** -0.5
    r_row = rows[GROW_R + ci:GROW_R + ci + 1]
    cm_row = rows[GROW_CM + ci:GROW_CM + ci + 1]
    b_row = rows[GROW_B + ci:GROW_B + ci + 1]
    g = rows[GROW_G + ci:GROW_G + ci + 1, 0:1]
    rmax = rows[GROW_RMAX + ci:GROW_RMAX + ci + 1, 0:1]
    r_colb = jnp.broadcast_to(cols[:, ci:ci + 1], (L, L))
    c_st = c_ref[ci]
    n_st = n_ref[ci]
    m_st = m_ref[ci][:, 0:1]
    mm = jnp.maximum(cm_row, m_st)
    pt = jnp.where(mask_sj, jnp.exp(r_colb - mm), 0.0)
    st = kq * scale * pt
    s_int = jnp.exp(m_st - mm)
    den = jnp.sum(st, axis=0, keepdims=True) + s_int * (qn2[0:1] + qn2[1:2])
    inv = 1.0 / jnp.maximum(jnp.abs(den), jnp.exp(-(b_row + mm)))
    lhs = jnp.concatenate([vt, c_st.astype(BF16)], axis=1)
    rhs = jnp.concatenate([st.astype(BF16), (qt.astype(F32) * s_int).astype(BF16)], axis=0)
    h = (jnp.dot(lhs, rhs, preferred_element_type=F32) * inv).astype(BF16).T

    w_row = jnp.exp(r_row - rmax)
    wl = jnp.concatenate([(vt.astype(F32) * w_row).astype(BF16),
                          jnp.broadcast_to(w_row, (BF16_ROWS, L)).astype(BF16)], axis=0)
    loc = jnp.dot(wl, k, preferred_element_type=F32) * scale
    m_loc = g + rmax
    m_new = jnp.maximum(g + m_st, m_loc)
    s_old = jnp.exp(g + m_st - m_new)
    s_loc = jnp.exp(m_loc - m_new)
    c_ref[ci] = s_old * c_st + s_loc * loc[0:MLSTM_HEAD_DIM]
    n_ref[ci] = s_old * n_st + s_loc * loc[MLSTM_HEAD_DIM:MLSTM_HEAD_DIM + 1]
    m_ref[ci] = jnp.broadcast_to(m_new, (1, LANES))
    return h


def _mlstm_kernel(kf_ref, kb_ref, qtf_ref, qtb_ref, vtf_ref, vtb_ref, rf_ref, rb_ref, cf_ref, cb_ref,
                  hf_ref, hb_ref, c_ref, n_ref, m_ref):
    L = MLSTM_CHUNK

    @pl.when(pl.program_id(1) == 0)
    def _():
        c_ref[...] = jnp.zeros_like(c_ref)
        n_ref[...] = jnp.zeros_like(n_ref)
        m_ref[...] = jnp.zeros_like(m_ref)

    row = lax.broadcasted_iota(jnp.int32, (L, L), 0)
    col = lax.broadcasted_iota(jnp.int32, (L, L), 1)
    dirs = ((kf_ref, qtf_ref, vtf_ref, rf_ref, cf_ref, hf_ref, row <= col),
            (kb_ref, qtb_ref, vtb_ref, rb_ref, cb_ref, hb_ref, row >= col))

    def operands(ci):
        d, hh = divmod(ci, MLSTM_HEADS)
        sl = slice(hh * MLSTM_HEAD_DIM, (hh + 1) * MLSTM_HEAD_DIM)
        return dirs[d], sl

    def scores(ci):
        (k_ref, qt_ref, _, _, _, _, _), sl = operands(ci)
        return _mlstm_scores(k_ref[0, :, sl], qt_ref[sl, :], n_ref, ci)

    pre = [scores(ci) for ci in range(MLSTM_LOOKAHEAD)]
    for ci in range(N_CHAINS):
        if ci + MLSTM_LOOKAHEAD < N_CHAINS:
            pre.append(scores(ci + MLSTM_LOOKAHEAD))
        kq, qn2 = pre.pop(0)
        (k_ref, qt_ref, vt_ref, r_ref, cl_ref, o_ref, mask), sl = operands(ci)
        o_ref[0, :, sl] = _mlstm_chain(kq, qn2, k_ref[0, :, sl], qt_ref[sl, :], vt_ref[sl, :],
                                       r_ref[0], cl_ref[0], mask, c_ref, n_ref, m_ref, ci)


def _mlstm(k, qt, vt, grows, gcols):
    bsz, seq, _ = k.shape
    L = MLSTM_CHUNK
    nc = seq // L
    fwd = lambda b, c: (b, c, 0)
    bwd = lambda b, c: (b, nc - 1 - c, 0)
    fwd_t = lambda b, c: (0, b * nc + c)
    bwd_t = lambda b, c: (0, b * nc + nc - 1 - c)
    fwd_r = lambda b, c: (b * nc + c, 0, 0)
    bwd_r = lambda b, c: (b * nc + nc - 1 - c, 0, 0)
    return pl.pallas_call(
        _mlstm_kernel,
        grid=(bsz, nc),
        in_specs=[pl.BlockSpec((1, L, MLSTM_WIDTH), fwd), pl.BlockSpec((1, L, MLSTM_WIDTH), bwd),
                  pl.BlockSpec((MLSTM_WIDTH, L), fwd_t), pl.BlockSpec((MLSTM_WIDTH, L), bwd_t),
                  pl.BlockSpec((MLSTM_WIDTH, L), fwd_t), pl.BlockSpec((MLSTM_WIDTH, L), bwd_t),
                  pl.BlockSpec((1, GROWS, LANES), fwd_r), pl.BlockSpec((1, GROWS, LANES), bwd_r),
                  pl.BlockSpec((1, L, LANES), fwd), pl.BlockSpec((1, L, LANES), bwd)],
        out_specs=[pl.BlockSpec((1, L, MLSTM_WIDTH), fwd),
                   pl.BlockSpec((1, L, MLSTM_WIDTH), bwd)],
        out_shape=[jax.ShapeDtypeStruct((bsz, seq, MLSTM_WIDTH), BF16),
                   jax.ShapeDtypeStruct((bsz, seq, MLSTM_WIDTH), BF16)],
        scratch_shapes=[pltpu.VMEM((N_CHAINS, MLSTM_HEAD_DIM, MLSTM_HEAD_DIM), F32),
                        pltpu.VMEM((N_CHAINS, 1, MLSTM_HEAD_DIM), F32),
                        pltpu.VMEM((N_CHAINS, 1, LANES), F32)],
        compiler_params=_cparams("arbitrary", "arbitrary"),
        name="mlstm",
    )(k, k, qt, qt, vt, vt, grows, grows, gcols, gcols)


NA_QROWS = 8
NA_KROWS = 2 * NA_QROWS
NA_KBLK = 4
NA_NKBLK = NA_KROWS // NA_KBLK
NA_CODE_SECOND = 2 * NA_MAX_KH - 2
NA_CODE_FIRST = NA_CODE_SECOND + NA_MAX_KH
NA_CODE_NONE = NA_CODE_FIRST + NA_MAX_KH
NA_NCODES = NA_CODE_NONE + 1


NA_LIVE_PAIRS = NA_MAX_KH // 2 + 1


def _natten_kernel(q_ref, k0_ref, k1_ref, k2_ref, k3_ref, v0_ref, v1_ref, v2_ref, v3_ref, bias_ref, o_ref,
                   st0_ref, st1_ref, pt0_ref, pt1_ref, *, rows):
    i = pl.program_id(1)
    half = NA_MAX_KH // 2
    rq0 = i * NA_QROWS
    w0 = jnp.clip(rq0 - half, 0, rows - NA_KROWS)
    nqp = NA_QROWS // 2
    nkp = NA_KROWS // 2
    tq = NA_QROWS * GRID_W

    def window_start(r):
        return jnp.clip(r - half, 0, rows - NA_MAX_KH)

    def code(rq, kp):
        r = rq0 + rq
        r0 = window_start(r)
        kr = w0 + 2 * kp
        d0 = kr - r + (NA_MAX_KH - 1)
        va = (kr >= r0) & (kr < r0 + NA_MAX_KH)
        vb = (kr + 1 >= r0) & (kr + 1 < r0 + NA_MAX_KH)
        return jnp.where(va & vb, d0,
                         jnp.where(vb, NA_CODE_SECOND + d0 + 1,
                                   jnp.where(va, NA_CODE_FIRST + d0 - (NA_MAX_KH - 1), NA_CODE_NONE)))

    first = [jnp.minimum((window_start(rq0 + 2 * qi) - w0) // 2, nkp - NA_LIVE_PAIRS) for qi in range(nqp)]
    codes = [[(code(2 * qi, first[qi] + t), code(2 * qi + 1, first[qi] + t)) for t in range(NA_LIVE_PAIRS)]
             for qi in range(nqp)]
    lane_t = lax.broadcasted_iota(jnp.int32, (LANES, LANES), 1)
    lane_q = lax.broadcasted_iota(jnp.int32, (tq, LANES), 1)
    sub_o = lax.broadcasted_iota(jnp.int32, (LANES, tq), 0)
    scale = jnp.asarray(NA_HEAD_DIM ** -0.5, BF16)
    k_refs = (k0_ref, k1_ref, k2_ref, k3_ref)
    v_refs = (v0_ref, v1_ref, v2_ref, v3_ref)
    st_refs = (st0_ref, st1_ref)
    pt_refs = (pt0_ref, pt1_ref)
    pt0_ref[...] = jnp.zeros_like(pt0_ref)
    pt1_ref[...] = jnp.zeros_like(pt1_ref)

    def lanes_of(hp):
        return slice(hp * LANES, (hp + 1) * LANES)

    def scores(h):
        hp, sb = divmod(h, 2)
        lsl = lanes_of(hp)
        qp = q_ref[0, :, lsl] * scale
        own = (lane_q >= sb * NA_HEAD_DIM) & (lane_q < (sb + 1) * NA_HEAD_DIM)
        qm = jnp.where(own, qp, jnp.zeros_like(qp))
        k_all = jnp.concatenate([r[0, :, lsl] for r in k_refs], axis=0)
        st_refs[sb][...] = lax.dot_general(k_all, qm, (((1,), (1,)), ((), ())), preferred_element_type=F32)

    def attend(h):
        hp, sb = divmod(h, 2)
        st_ref, pt_ref = st_refs[sb], pt_refs[sb]
        invs = []
        for qi in range(nqp):
            qsl = slice(qi * LANES, (qi + 1) * LANES)
            offs = [pl.multiple_of((first[qi] + t) * LANES, LANES) for t in range(NA_LIVE_PAIRS)]
            tiles = []
            for t in range(NA_LIVE_PAIRS):
                ca, cb = codes[qi][t]
                bias = jnp.where(lane_t < GRID_W, bias_ref[h, ca], bias_ref[h, cb])
                tiles.append(st_ref[pl.ds(offs[t], LANES), qsl] + bias)
            m = functools.reduce(jnp.maximum, tiles)
            m = jnp.max(m, axis=0, keepdims=True)
            es = [jnp.exp(t - m) for t in tiles]
            l = jnp.sum(functools.reduce(jnp.add, es), axis=0, keepdims=True)
            invs.append(1.0 / l)
            for t in range(NA_LIVE_PAIRS):
                pt_ref[pl.ds(offs[t], LANES), qsl] = es[t].astype(BF16)
        vt = jnp.concatenate([r[0, :, lanes_of(hp)] for r in v_refs], axis=0).T
        return jnp.dot(vt, pt_ref[...], preferred_element_type=F32) * jnp.concatenate(invs, axis=1)

    scores(0)
    outs = []
    for h in range(NA_HEADS):
        if h + 1 < NA_HEADS:
            scores(h + 1)
        outs.append(attend(h))
        if h % 2 == 1:
            ot = jnp.where(sub_o < NA_HEAD_DIM, outs[0], outs[1])
            o_ref[0, :, lanes_of(h // 2)] = ot.T.astype(o_ref.dtype)
            outs = []


def _natten_bias_table(rel_bias):
    cols = jnp.arange(GRID_W)
    col_start = jnp.clip(cols - NA_KW // 2, 0, GRID_W - NA_KW)
    kcol = jnp.arange(GRID_W)
    dc = kcol[:, None] - cols[None, :] + (NA_KW - 1)
    inwin = (kcol[:, None] >= col_start[None, :]) & (kcol[:, None] < col_start[None, :] + NA_KW)
    tab = rel_bias.astype(F32)[:, :, jnp.clip(dc, 0, 2 * NA_KW - 2)]
    tab = jnp.where(inwin[None, None], tab, NEG_BIG)
    neg = jnp.full_like(tab[:, :NA_MAX_KH], NEG_BIG)
    both = jnp.concatenate([tab[:, :-1], tab[:, 1:]], axis=2)
    second = jnp.concatenate([neg, tab[:, :NA_MAX_KH]], axis=2)
    first = jnp.concatenate([tab[:, NA_MAX_KH - 1:], neg], axis=2)
    none = jnp.concatenate([neg[:, :1], neg[:, :1]], axis=2)
    t = jnp.concatenate([both, second, first, none], axis=1)
    return jnp.concatenate([t, t], axis=-1)


def _natten(na, bias_table):
    bsz, seq, _ = na.shape
    rows = seq // GRID_W
    tq = NA_QROWS * GRID_W
    tk = NA_KBLK * GRID_W
    nkb = seq // tk

    def kv_spec(j, col):
        return pl.BlockSpec((1, tk, NA_WIDTH),
                            lambda b, i: (b, jnp.clip(2 * i - 1, 0, nkb - NA_NKBLK) + j, col))

    return pl.pallas_call(
        functools.partial(_natten_kernel, rows=rows),
        grid=(bsz, rows // NA_QROWS),
        in_specs=[pl.BlockSpec((1, tq, NA_WIDTH), lambda b, i: (b, i, 0))]
                 + [kv_spec(j, 1) for j in range(NA_NKBLK)]
                 + [kv_spec(j, 2) for j in range(NA_NKBLK)]
                 + [pl.BlockSpec(bias_table.shape, lambda b, i: (0, 0, 0, 0), pipeline_mode=pl.Buffered(1))],
        out_specs=pl.BlockSpec((1, tq, NA_WIDTH), lambda b, i: (b, i, 0)),
        out_shape=jax.ShapeDtypeStruct((bsz, seq, NA_WIDTH), BF16),
        scratch_shapes=[pltpu.VMEM((NA_KROWS * GRID_W, tq), F32), pltpu.VMEM((NA_KROWS * GRID_W, tq), F32),
                        pltpu.VMEM((NA_KROWS * GRID_W, tq), BF16), pltpu.VMEM((NA_KROWS * GRID_W, tq), BF16)],
        compiler_params=_cparams("arbitrary", "arbitrary"),
        name="natten",
    )(na, *([na] * (2 * NA_NKBLK)), bias_table)


AUG = LANES
GATE_LO_SHIFT = N_EXPERTS


def _postmix_kernel(hf_ref, hb_ref, o_ref, hn_ref, x_ref, mod_ref, gm_ref, wm_ref, wn_ref,
                    gpost_ref, gffn_ref, wr_ref, x1_ref, x2_ref, aff_ref):
    d = x_ref.shape[-1]
    s = hf_ref[...].astype(F32) + hb_ref[...].astype(F32)
    og = jax.nn.sigmoid(o_ref[...].astype(F32))
    heads = []
    for hh in range(MLSTM_HEADS):
        sl = slice(hh * MLSTM_HEAD_DIM, (hh + 1) * MLSTM_HEAD_DIM)
        heads.append(_rms(s[:, sl]) * gm_ref[:, sl] * og[:, sl])
    hm = jnp.concatenate(heads, axis=-1).astype(BF16)
    mix = (jnp.dot(hm, wm_ref[...], preferred_element_type=F32)
           + jnp.dot(hn_ref[...], wn_ref[...], preferred_element_type=F32))
    gate1 = mod_ref[0, 2:3, :]
    shift2 = mod_ref[0, 3:4, :]
    scale2 = mod_ref[0, 4:5, :]
    x1 = x_ref[...] + gate1 * (_rms(mix) * gpost_ref[...])
    x1_ref[...] = x1
    h2 = (_rms(x1) * gffn_ref[...]) * (1.0 + scale2) + shift2
    h2b = h2.astype(BF16)
    x2_ref[:, 0:d] = h2b
    logits = jnp.dot(h2b, wr_ref[...], preferred_element_type=F32)
    lane = lax.broadcasted_iota(jnp.int32, logits.shape, 1)
    logits = jnp.where(lane < N_EXPERTS, logits, NEG_BIG)
    ex = jnp.exp(logits - jnp.max(logits, axis=-1, keepdims=True))
    aff = ex / jnp.sum(ex, axis=-1, keepdims=True)
    aff_ref[...] = aff
    hi = aff.astype(BF16).astype(F32)
    lo = (aff - hi).astype(BF16).astype(F32)
    x2_ref[:, d:d + AUG] = (hi + pltpu.roll(lo, GATE_LO_SHIFT, 1)).astype(BF16)


def _postmix(hf, hb, o, hn, x2d, mod, g_mlstm, w_om, w_on, g_post, g_ffn, w_r, seq, tm=512):
    n, d = x2d.shape
    tiles_per_b = seq // tm
    full = lambda a: pl.BlockSpec(a.shape, lambda i: (0, 0))
    tok = lambda w: pl.BlockSpec((tm, w), lambda i: (i, 0))
    return pl.pallas_call(
        _postmix_kernel,
        grid=(n // tm,),
        in_specs=[tok(MLSTM_WIDTH), tok(MLSTM_WIDTH), tok(MLSTM_WIDTH), tok(NA_WIDTH), tok(d),
                  pl.BlockSpec((1, N_MOD, d), lambda i: (i // tiles_per_b, 0, 0)),
                  full(g_mlstm), full(w_om), full(w_on), full(g_post), full(g_ffn), full(w_r)],
        out_specs=[tok(d), tok(d + AUG), tok(LANES)],
        out_shape=[jax.ShapeDtypeStruct((n, d), F32),
                   jax.ShapeDtypeStruct((n, d + AUG), BF16),
                   jax.ShapeDtypeStruct((n, LANES), F32)],
        compiler_params=_cparams("arbitrary"),
        name="postmix",
    )(hf, hb, o, hn, x2d, mod, g_mlstm, w_om, w_on, g_post, g_ffn, w_r)


TOK_TILE = 128
SEL_STEP = 1024
UNSELECTED = -1e6
SELECT_HI = 2.0
SELECT_ITERS = 160


def _select_kernel(afft_ref, aff_ref, pos_ref, post_ref, tb_ref, thr_ref, need_ref, base_ref, *, cap):
    step = pl.program_id(0)
    ne = afft_ref.shape[0]

    @pl.when(step == 0)
    def _():
        a_t = afft_ref[...]

        def count_ge(v):
            return jnp.sum(jnp.where(a_t >= v, 1.0, 0.0), axis=1, keepdims=True)

        def body(it, lh):
            lo, hi = lh
            mid = 0.5 * (lo + hi)
            ge = count_ge(mid) >= cap
            return jnp.where(ge, mid, lo), jnp.where(ge, hi, mid)

        lo_col, hi_col = lax.fori_loop(0, SELECT_ITERS, body,
                                       (jnp.zeros((ne, 1), F32), jnp.full((ne, 1), SELECT_HI, F32)))
        need_col = cap - count_ge(hi_col)
        sub = lax.broadcasted_iota(jnp.int32, (ne, LANES), 0)
        ln = lax.broadcasted_iota(jnp.int32, (ne, LANES), 1)
        diag = sub == ln
        thr_ref[...] = jnp.sum(jnp.where(diag, lo_col, 0.0), axis=0, keepdims=True)
        need_ref[...] = jnp.sum(jnp.where(diag, need_col, 0.0), axis=0, keepdims=True)
        base_ref[...] = jnp.zeros_like(base_ref)
        base_ref[2:3, :] = jnp.sum(jnp.where(diag, hi_col, 0.0), axis=0, keepdims=True)

    thr_lo = thr_ref[...]
    thr_hi = base_ref[2:3, :]
    need = need_ref[...]
    lane = lax.broadcasted_iota(jnp.int32, (TOK_TILE, LANES), 1)
    valid = lane < ne
    row = lax.broadcasted_iota(jnp.int32, (TOK_TILE, TOK_TILE), 0)
    col = lax.broadcasted_iota(jnp.int32, (TOK_TILE, TOK_TILE), 1)
    strict_lower = (col < row).astype(BF16)
    base_gt = base_ref[0:1, :]
    base_eq = base_ref[1:2, :]
    tb_rows = []
    for t in range(SEL_STEP // TOK_TILE):
        a = aff_ref[t * TOK_TILE:(t + 1) * TOK_TILE, :]
        gt = (a >= thr_hi) & valid
        eq = (a >= thr_lo) & (a < thr_hi) & valid
        gtf = jnp.where(gt, 1.0, 0.0)
        eqf = jnp.where(eq, 1.0, 0.0)
        cg = jnp.dot(strict_lower, gtf.astype(BF16), preferred_element_type=F32)
        ce = jnp.dot(strict_lower, eqf.astype(BF16), preferred_element_type=F32)
        eq_rank = base_eq + ce
        sel = gt | (eq & (eq_rank < need))
        pos = base_gt + cg + jnp.minimum(eq_rank, need)
        posm = jnp.where(sel, pos, UNSELECTED)
        pos_ref[t * TOK_TILE:(t + 1) * TOK_TILE, :] = posm
        post_ref[:, t * TOK_TILE:(t + 1) * TOK_TILE] = posm.T[0:ne, :]
        tb_rows.append(base_gt + jnp.minimum(base_eq, need))
        base_gt = base_gt + jnp.sum(gtf, axis=0, keepdims=True)
        base_eq = base_eq + jnp.sum(eqf, axis=0, keepdims=True)
    tb_ref[...] = jnp.concatenate(tb_rows, axis=0).astype(jnp.int32)
    base_ref[0:1, :] = base_gt
    base_ref[1:2, :] = base_eq


def _select(aff, cap):
    n = aff.shape[0]
    afft = aff[:, :N_EXPERTS].T
    nsteps = n // SEL_STEP
    tiles_per_step = SEL_STEP // TOK_TILE
    return pl.pallas_call(
        functools.partial(_select_kernel, cap=cap),
        grid=(nsteps,),
        in_specs=[pl.BlockSpec((N_EXPERTS, n), lambda i: (0, 0)),
                  pl.BlockSpec((SEL_STEP, LANES), lambda i: (i, 0))],
        out_specs=[pl.BlockSpec((SEL_STEP, LANES), lambda i: (i, 0)),
                   pl.BlockSpec((N_EXPERTS, SEL_STEP), lambda i: (0, i)),
                   pl.BlockSpec((tiles_per_step, LANES), lambda i: (i, 0))],
        out_shape=[jax.ShapeDtypeStruct((n, LANES), F32),
                   jax.ShapeDtypeStruct((N_EXPERTS, n), F32),
                   jax.ShapeDtypeStruct((n // TOK_TILE, LANES), jnp.int32)],
        scratch_shapes=[pltpu.VMEM((1, LANES), F32), pltpu.VMEM((1, LANES), F32),
                        pltpu.VMEM((SUBLANES, LANES), F32)],
        compiler_params=_cparams("arbitrary"),
        name="select",
    )(afft, aff)


DISP_TILE = 2 * TOK_TILE
CARRY = SUBLANES
DISP_FAST = 64
DISP_SLOW = CARRY + DISP_TILE + CARRY
XE_DTYPE = F32
XE_PAD = 2 * DISP_TILE
DISP_SLOW_GROUP = 4
DISP_STAGE_ROWS = max(N_EXPERTS * DISP_FAST, DISP_SLOW_GROUP * DISP_SLOW)


def _dispatch_kernel(tb_ref, x_ref, post_ref, xe_ref, stage_ref, carry_ref, sem, flag_ref, *, cap, ntiles):
    i = pl.program_id(0)
    slot = lax.rem(i, 2)
    ne = N_EXPERTS

    @pl.when(i == 0)
    def _():
        carry_ref[...] = jnp.zeros_like(carry_ref)
        stage_ref[1, 0:XE_PAD, :] = jnp.zeros((XE_PAD, stage_ref.shape[-1]), stage_ref.dtype)
        pads = [pltpu.make_async_copy(stage_ref.at[1, pl.ds(0, XE_PAD)], xe_ref.at[e, pl.ds(cap, XE_PAD)],
                                      sem.at[1]) for e in range(ne)]
        for cp in pads:
            cp.start()
        for cp in pads:
            cp.wait()

    fill = [tb_ref[i * ne + e] for e in range(ne)]
    nxt = [jnp.where(i + 1 < ntiles, tb_ref[jnp.minimum(i + 1, ntiles - 1) * ne + e], cap) for e in range(ne)]
    al = [(f // CARRY) * CARRY for f in fill]
    need_max = functools.reduce(jnp.maximum, [nx - a for nx, a in zip(nxt, al)])
    slow = need_max > DISP_FAST

    def copies(s_rows, sl, experts, als):
        return [pltpu.make_async_copy(stage_ref.at[sl, pl.ds(k * s_rows, s_rows)],
                                      xe_ref.at[e, pl.ds(pl.multiple_of(als[e], CARRY), s_rows)],
                                      sem.at[sl]) for k, e in enumerate(experts)]

    def wait_in_flight(sl):
        @pl.when(flag_ref[sl] == 0)
        def _():
            for cp in copies(DISP_FAST, sl, range(ne), [0] * ne):
                cp.wait()

    def stage(s_rows, experts, xt, cb):
        slot_of_tok = lax.broadcasted_iota(jnp.int32, (s_rows, DISP_TILE), 0).astype(F32)
        sub = lax.broadcasted_iota(jnp.int32, (s_rows, ne * CARRY), 0)
        ln = lax.broadcasted_iota(jnp.int32, (s_rows, ne * CARRY), 1)
        a_tok, a_car = [], []
        for e in experts:
            rel = post_ref[e:e + 1, :] - al[e].astype(F32)
            a_tok.append(jnp.where(slot_of_tok == rel, 1.0, 0.0))
            own = (ln >= e * CARRY) & (ln < e * CARRY + (fill[e] - al[e])) & (ln - e * CARRY == sub)
            a_car.append(jnp.where(own, 1.0, 0.0))
        stage_ref[slot, 0:len(experts) * s_rows, :] = (
            jnp.dot(jnp.concatenate(a_tok, axis=0).astype(BF16), xt, preferred_element_type=F32)
            + jnp.dot(jnp.concatenate(a_car, axis=0).astype(BF16), cb, preferred_element_type=F32)
        ).astype(stage_ref.dtype)
        for k, e in enumerate(experts):
            off = jnp.minimum((nxt[e] // CARRY) * CARRY - al[e], s_rows - CARRY)
            carry_ref[e * CARRY:(e + 1) * CARRY, :] = stage_ref[slot, pl.ds(pl.multiple_of(k * s_rows + off, CARRY), CARRY), :]

    @pl.when(jnp.logical_not(slow))
    def _():
        stage(DISP_FAST, range(ne), x_ref[...], carry_ref[...].astype(BF16))

        @pl.when(i > 0)
        def _():
            wait_in_flight(1 - slot)
        for cp in copies(DISP_FAST, slot, range(ne), al):
            cp.start()
        flag_ref[slot] = 0

    @pl.when(slow)
    def _():
        xt = x_ref[...]
        cb = carry_ref[...].astype(BF16)

        @pl.when(i > 0)
        def _():
            wait_in_flight(1 - slot)
        for e0 in range(0, ne, DISP_SLOW_GROUP):
            experts = range(e0, e0 + DISP_SLOW_GROUP)
            stage(DISP_SLOW, experts, xt, cb)
            cps = copies(DISP_SLOW, slot, experts, al)
            for cp in cps:
                cp.start()
            for cp in cps:
                cp.wait()
        flag_ref[slot] = 1

    @pl.when(i == ntiles - 1)
    def _():
        wait_in_flight(slot)


def _dispatch(x2aug, post, tb, cap):
    n, wd = x2aug.shape
    ntiles = n // DISP_TILE
    tb_flat = tb[::DISP_TILE // TOK_TILE, :N_EXPERTS].reshape(-1)
    return pl.pallas_call(
        functools.partial(_dispatch_kernel, cap=cap, ntiles=ntiles),
        grid_spec=pltpu.PrefetchScalarGridSpec(
            num_scalar_prefetch=1,
            grid=(ntiles,),
            in_specs=[pl.BlockSpec((DISP_TILE, wd), lambda i, tb: (i, 0)),
                      pl.BlockSpec((N_EXPERTS, DISP_TILE), lambda i, tb: (0, i))],
            out_specs=pl.BlockSpec(memory_space=pl.ANY),
            scratch_shapes=[pltpu.VMEM((2, DISP_STAGE_ROWS, wd), XE_DTYPE),
                            pltpu.VMEM((N_EXPERTS * CARRY, wd), XE_DTYPE),
                            pltpu.SemaphoreType.DMA((2,)),
                            pltpu.SMEM((2,), jnp.int32)]),
        out_shape=jax.ShapeDtypeStruct((N_EXPERTS, cap + XE_PAD, wd), XE_DTYPE),
        compiler_params=_cparams("arbitrary"),
        name="dispatch",
    )(tb_flat, x2aug, post)


EXP_ROWS = 512


def _experts_kernel(xe_ref, wg_ref, wu_ref, wd_ref, ye_ref, wgb_ref, wub_ref, wdb_ref):
    e = pl.program_id(0)
    d = wg_ref.shape[1]

    @pl.when(pl.program_id(1) == 0)
    def _():
        wgb_ref[...] = wg_ref[0].astype(BF16)
        wub_ref[...] = wu_ref[0].astype(BF16)
        wdb_ref[...] = wd_ref[0].astype(BF16)

    x = xe_ref[0, :, 0:d].astype(BF16)
    aug = xe_ref[0, :, d:d + AUG].astype(F32)
    lane = lax.broadcasted_iota(jnp.int32, aug.shape, 1)
    mine = (lane == e) | (lane == e + GATE_LO_SHIFT)
    gate = jnp.sum(jnp.where(mine, aug, 0.0), axis=-1, keepdims=True)
    g = jnp.dot(x, wgb_ref[...], preferred_element_type=F32)
    u = jnp.dot(x, wub_ref[...], preferred_element_type=F32)
    hid = (g * jax.nn.sigmoid(g)) * u
    y = jnp.dot(hid.astype(BF16), wdb_ref[...], preferred_element_type=F32)
    ye_ref[0] = (y * gate).astype(ye_ref.dtype)


def _experts(xe, w_gate, w_up, w_down, cap):
    ne, _, wd = xe.shape
    d = w_gate.shape[1]
    f = w_gate.shape[2]
    wspec = lambda s: pl.BlockSpec((1,) + s, lambda e, j: (e, 0, 0))
    rows = min(EXP_ROWS, cap)
    return pl.pallas_call(
        _experts_kernel,
        grid=(ne, cap // rows),
        in_specs=[pl.BlockSpec((1, rows, wd), lambda e, j: (e, j, 0)),
                  wspec((d, f)), wspec((d, f)), wspec((f, d))],
        out_specs=pl.BlockSpec((1, rows, d), lambda e, j: (e, j, 0)),
        out_shape=jax.ShapeDtypeStruct((ne, cap, d), BF16),
        scratch_shapes=[pltpu.VMEM((d, f), BF16), pltpu.VMEM((d, f), BF16), pltpu.VMEM((f, d), BF16)],
        compiler_params=_cparams("arbitrary", "arbitrary"),
        name="experts",
    )(xe, w_gate, w_up, w_down)


COMB_TILE = 2 * TOK_TILE
COMB_FAST = 128
COMB_SLOW = 384
COMB_ALIGN = BF16_ROWS


def _combine_kernel(tb_ref, pos_ref, x1_ref, mod_ref, g_ref, ye_ref, y_ref, slab_ref, sem, *, cap, ntiles):
    i = pl.program_id(0)
    slot = lax.rem(i, 2)
    ne = N_EXPERTS

    def info(t):
        t = jnp.minimum(t, ntiles - 1)
        fill = [tb_ref[t * ne + e] for e in range(ne)]
        nxt = [jnp.where(t + 1 < ntiles, tb_ref[jnp.minimum(t + 1, ntiles - 1) * ne + e], cap) for e in range(ne)]
        st_f = [jnp.minimum((f // COMB_ALIGN) * COMB_ALIGN, cap - COMB_FAST) for f in fill]
        st_s = [jnp.minimum((f // COMB_ALIGN) * COMB_ALIGN, cap - COMB_SLOW) for f in fill]
        slow = functools.reduce(jnp.maximum, [nx - s for nx, s in zip(nxt, st_f)]) > COMB_FAST
        return st_f, st_s, slow

    def copies(s_rows, sl, starts):
        return [pltpu.make_async_copy(ye_ref.at[e, pl.ds(pl.multiple_of(starts[e], COMB_ALIGN), s_rows)],
                                      slab_ref.at[sl, pl.ds(e * s_rows, s_rows)],
                                      sem.at[sl]) for e in range(ne)]

    def fetch(t, sl):
        st_f, st_s, slow = info(t)

        @pl.when(slow)
        def _():
            for cp in copies(COMB_SLOW, sl, st_s):
                cp.start()

        @pl.when(jnp.logical_not(slow))
        def _():
            for cp in copies(COMB_FAST, sl, st_f):
                cp.start()

    @pl.when(i == 0)
    def _():
        fetch(i, slot)

    @pl.when(i + 1 < ntiles)
    def _():
        fetch(i + 1, 1 - slot)

    st_f, st_s, slow = info(i)
    lane = lax.broadcasted_iota(jnp.int32, (COMB_TILE, LANES), 1).astype(F32)
    pos = pos_ref[...]

    def finish(moe):
        gate2 = mod_ref[0, 5:6, :]
        y_ref[...] = x1_ref[...] + gate2 * (_rms(moe) * g_ref[...])

    @pl.when(jnp.logical_not(slow))
    def _():
        for cp in copies(COMB_FAST, slot, st_f):
            cp.wait()
        per = LANES // COMB_FAST
        groups = []
        for gi in range(ne // per):
            val = None
            for k in range(per):
                e = gi * per + k
                rel = pos[:, e:e + 1] - st_f[e].astype(F32) + float(k * COMB_FAST)
                inband = (lane >= k * COMB_FAST) & (lane < (k + 1) * COMB_FAST)
                hit = inband & (lane == rel)
                val = hit if val is None else (val | hit)
            groups.append(jnp.where(val, 1.0, 0.0).astype(BF16))
        onehot = jnp.concatenate(groups, axis=1)
        finish(jnp.dot(onehot, slab_ref[slot, 0:ne * COMB_FAST, :], preferred_element_type=F32))

    @pl.when(slow)
    def _():
        for cp in copies(COMB_SLOW, slot, st_s):
            cp.wait()
        per = COMB_SLOW // LANES
        groups = []
        for e in range(ne):
            rel = pos[:, e:e + 1] - st_s[e].astype(F32)
            for k in range(per):
                groups.append(jnp.where(lane + float(k * LANES) == rel, 1.0, 0.0).astype(BF16))
        onehot = jnp.concatenate(groups, axis=1)
        finish(jnp.dot(onehot, slab_ref[slot], preferred_element_type=F32))


def _combine(tb, pos, x1, mod, g_post_ffn, ye, cap, seq):
    n, d = x1.shape
    assert cap >= COMB_SLOW
    ntiles = n // COMB_TILE
    tiles_per_b = seq // COMB_TILE
    tb_flat = tb[::COMB_TILE // TOK_TILE, :N_EXPERTS].reshape(-1)
    return pl.pallas_call(
        functools.partial(_combine_kernel, cap=cap, ntiles=ntiles),
        grid_spec=pltpu.PrefetchScalarGridSpec(
            num_scalar_prefetch=1,
            grid=(ntiles,),
            in_specs=[pl.BlockSpec((COMB_TILE, LANES), lambda i, tb: (i, 0)),
                      pl.BlockSpec((COMB_TILE, d), lambda i, tb: (i, 0)),
                      pl.BlockSpec((1, N_MOD, d), lambda i, tb: (i // tiles_per_b, 0, 0)),
                      pl.BlockSpec((1, d), lambda i, tb: (0, 0)),
                      pl.BlockSpec(memory_space=pl.ANY)],
            out_specs=pl.BlockSpec((COMB_TILE, d), lambda i, tb: (i, 0)),
            scratch_shapes=[pltpu.VMEM((2, N_EXPERTS * COMB_SLOW, d), BF16),
                            pltpu.SemaphoreType.DMA((2,))]),
        out_shape=jax.ShapeDtypeStruct((n, d), F32),
        compiler_params=_cparams("arbitrary"),
        name="combine",
    )(tb_flat, pos, x1, mod, g_post_ffn, ye)


def _prep_weights(w_in, b_gates, g_mlstm, na_rel_bias, w_out, w_router):
    w = MLSTM_WIDTH
    ng = 4 * MLSTM_HEADS
    w_qt = w_in[:, 0:w].T.astype(BF16)
    w_k = w_in[:, w:2 * w].astype(BF16)
    w_vt = w_in[:, 2 * w:3 * w].T.astype(BF16)
    w_o = w_in[:, 3 * w:4 * w].astype(BF16)
    perm = [d * 2 * MLSTM_HEADS + g * MLSTM_HEADS + h
            for g in range(2) for d in range(2) for h in range(MLSTM_HEADS)]
    perm = jnp.asarray(perm)
    w_g = jnp.pad(w_in[:, 4 * w:4 * w + ng][:, perm], ((0, 0), (0, LANES - ng))).astype(BF16)
    b_g = jnp.pad(b_gates.reshape(-1)[perm].astype(F32), (0, LANES - ng)).reshape(1, LANES)
    w_n = w_in[:, 4 * w + ng:].astype(BF16)
    w_om = w_out[0:w].astype(BF16)
    w_on = w_out[w:].astype(BF16)
    w_r = jnp.pad(w_router, ((0, 0), (0, LANES - N_EXPERTS))).astype(BF16)
    return dict(w_qt=w_qt, w_k=w_k, w_vt=w_vt, w_o=w_o, w_g=w_g, b_g=b_g, w_n=w_n, w_om=w_om, w_on=w_on, w_r=w_r,
                g_mlstm=g_mlstm.reshape(1, w).astype(F32), bias_table=_natten_bias_table(na_rel_bias))


def _layer(x, mod, pw, g_pre_mix, g_post_mix, g_pre_ffn, g_post_ffn, w_eg, w_eu, w_ed):
    bsz, seq, d = x.shape
    n = bsz * seq
    x2d = x.reshape(n, d)
    row = lambda g: g.reshape(1, d).astype(F32)
    qt, k, vt, o, gates, na = _inproj(x2d, mod, row(g_pre_mix), pw["w_qt"], pw["w_k"], pw["w_vt"], pw["w_o"],
                                      pw["w_g"], pw["w_n"], seq)
    grows, gcols = _gateprep(gates, pw["b_g"])
    hf, hb = _mlstm(k.reshape(bsz, seq, -1), qt, vt, grows, gcols.reshape(bsz, seq, -1))
    hn = _natten(na.reshape(bsz, seq, -1), pw["bias_table"])
    x1, x2aug, aff = _postmix(hf.reshape(n, -1), hb.reshape(n, -1), o, hn.reshape(n, -1), x2d, mod,
                              pw["g_mlstm"], pw["w_om"], pw["w_on"], row(g_post_mix), row(g_pre_ffn),
                              pw["w_r"], seq)
    cap = EC_CAPACITY_FACTOR * n // N_EXPERTS
    pos, post, tb = _select(aff, cap)
    xe = _dispatch(x2aug, post, tb, cap)
    ye = _experts(xe, w_eg, w_eu, w_ed, cap)
    y = _combine(tb, pos, x1, mod, row(g_post_ffn), ye, cap, seq)
    return y.reshape(bsz, seq, d)


def kernel(x_prompt, x_sample, c_prompt, c_sample, w_ada, b_ada, g_pre_mix, g_post_mix, w_in, b_gates, g_mlstm, na_rel_bias, w_out, g_pre_ffn, g_post_ffn, w_router, w_expert_gate, w_expert_up, w_expert_down):
    depth = w_ada.shape[0]
    nb = x_prompt.shape[0]
    xs = [x_prompt, x_sample]
    cs = jnp.concatenate([c_prompt, c_sample], axis=0)
    for l in range(depth):
        mod = _mod(cs, w_ada[l], b_ada[l])
        pw = _prep_weights(w_in[l], b_gates[l], g_mlstm[l], na_rel_bias[l], w_out[l], w_router[l])
        mods = [mod[:nb], mod[nb:]]
        xs = [_layer(x, m, pw, g_pre_mix[l], g_post_mix[l], g_pre_ffn[l], g_post_ffn[l],
                     w_expert_gate[l], w_expert_up[l], w_expert_down[l]) for x, m in zip(xs, mods)]
    return (xs[0], xs[1])
```

```python
import functools
import math

import jax
import jax.numpy as jnp
from jax import lax
from jax.experimental import pallas as pl
from jax.experimental.pallas import tpu as pltpu

F32 = jnp.float32
BF16 = jnp.bfloat16

LANES = 128
SUBLANES = 8
BF16_ROWS = 16
VMEM_LIMIT = 56 * 1024 * 1024

RMS_EPS = 1e-6
N_MOD = 6
MLSTM_HEADS = 4
MLSTM_HEAD_DIM = 128
MLSTM_CHUNK = 128
MLSTM_WIDTH = MLSTM_HEADS * MLSTM_HEAD_DIM
NA_HEAD_DIM = 64
NA_HEADS = 8
NA_WIDTH = NA_HEADS * NA_HEAD_DIM
NA_MAX_KH = 8
NA_KW = 16
GRID_W = 64
N_EXPERTS = 16
EC_CAPACITY_FACTOR = 2
NEG_BIG = -1e30


def _cparams(*sem):
    return pltpu.CompilerParams(dimension_semantics=sem, vmem_limit_bytes=VMEM_LIMIT)


def _rms(x):
    return x * lax.rsqrt(jnp.mean(x * x, axis=-1, keepdims=True) + RMS_EPS)


def _mod_kernel(c_ref, w_ref, b_ref, o_ref):
    c = c_ref[...]
    a = c * jax.nn.sigmoid(c)
    o_ref[...] = jnp.dot(a, w_ref[...], preferred_element_type=F32,
                         precision=lax.Precision.HIGHEST) + b_ref[...]


def _mod(c, w_ada, b_ada):
    bsz, d = c.shape
    n = w_ada.shape[1]
    bn = 1024
    out = pl.pallas_call(
        _mod_kernel,
        grid=(n // bn,),
        in_specs=[pl.BlockSpec((bsz, d), lambda j: (0, 0)),
                  pl.BlockSpec((d, bn), lambda j: (0, j)),
                  pl.BlockSpec((1, bn), lambda j: (0, j))],
        out_specs=pl.BlockSpec((bsz, bn), lambda j: (0, j)),
        out_shape=jax.ShapeDtypeStruct((bsz, n), F32),
        compiler_params=_cparams("arbitrary"),
        name="mod",
    )(c, w_ada, b_ada.reshape(1, n))
    return out.reshape(bsz, N_MOD, d)


def _inproj_kernel(x_ref, mod_ref, g_ref, wqt_ref, wk_ref, wvt_ref, wo_ref, wg_ref, wn_ref,
                   qt_ref, k_ref, vt_ref, o_ref, gates_ref, na_ref):
    x = x_ref[...]
    shift = mod_ref[0, 0:1, :]
    scale = mod_ref[0, 1:2, :]
    h = (_rms(x) * g_ref[...]) * (1.0 + scale) + shift
    hb = h.astype(BF16)
    nt = (((1,), (1,)), ((), ()))
    qt_ref[...] = lax.dot_general(wqt_ref[...], hb, nt, preferred_element_type=F32).astype(BF16)
    vt_ref[...] = lax.dot_general(wvt_ref[...], hb, nt, preferred_element_type=F32).astype(BF16)
    k_ref[...] = jnp.dot(hb, wk_ref[...], preferred_element_type=F32).astype(BF16)
    o_ref[...] = jnp.dot(hb, wo_ref[...], preferred_element_type=F32).astype(BF16)
    gates_ref[...] = jnp.dot(hb, wg_ref[...], preferred_element_type=F32)
    na_ref[...] = jnp.dot(hb, wn_ref[...], preferred_element_type=F32).astype(BF16)


def _inproj(x2d, mod, g_pre, w_qt, w_k, w_vt, w_o, w_g, w_n, seq, tm=512):
    n, d = x2d.shape
    tiles_per_b = seq // tm
    full = lambda a: pl.BlockSpec(a.shape, lambda i: (0, 0))
    tok = lambda w: pl.BlockSpec((tm, w.shape[1]), lambda i: (i, 0))
    feat = lambda w: pl.BlockSpec((w.shape[0], tm), lambda i: (0, i))
    return pl.pallas_call(
        _inproj_kernel,
        grid=(n // tm,),
        in_specs=[pl.BlockSpec((tm, d), lambda i: (i, 0)),
                  pl.BlockSpec((1, N_MOD, d), lambda i: (i // tiles_per_b, 0, 0)),
                  full(g_pre), full(w_qt), full(w_k), full(w_vt), full(w_o), full(w_g), full(w_n)],
        out_specs=[feat(w_qt), tok(w_k), feat(w_vt), tok(w_o), tok(w_g), tok(w_n)],
        out_shape=[jax.ShapeDtypeStruct((w_qt.shape[0], n), BF16),
                   jax.ShapeDtypeStruct((n, w_k.shape[1]), BF16),
                   jax.ShapeDtypeStruct((w_vt.shape[0], n), BF16),
                   jax.ShapeDtypeStruct((n, w_o.shape[1]), BF16),
                   jax.ShapeDtypeStruct((n, w_g.shape[1]), F32),
                   jax.ShapeDtypeStruct((n, w_n.shape[1]), BF16)],
        compiler_params=_cparams("arbitrary"),
        name="inproj",
    )(x2d, mod, g_pre, w_qt, w_k, w_vt, w_o, w_g, w_n)


N_CHAINS = 2 * MLSTM_HEADS
GROW_R, GROW_CM, GROW_B, GROW_G, GROW_RMAX = (i * N_CHAINS for i in range(5))
GROWS = 5 * N_CHAINS
GATEPREP_CHUNKS = 8
MLSTM_LOOKAHEAD = 4


def _lane_scan_max(x, lane, reverse):
    sh = 1
    while sh < LANES:
        if reverse:
            x = jnp.maximum(x, jnp.where(lane < LANES - sh, pltpu.roll(x, LANES - sh, 1), NEG_BIG))
        else:
            x = jnp.maximum(x, jnp.where(lane >= sh, pltpu.roll(x, sh, 1), NEG_BIG))
        sh *= 2
    return x


def _gateprep_kernel(g_ref, bias_ref, rows_ref, cols_ref):
    L = MLSTM_CHUNK
    row = lax.broadcasted_iota(jnp.int32, (L, L), 0)
    col = lax.broadcasted_iota(jnp.int32, (L, L), 1)
    tri = jnp.concatenate([jnp.where(row <= col, 1.0, 0.0), jnp.where(row >= col, 1.0, 0.0)], axis=1)
    crow = lax.broadcasted_iota(jnp.int32, (N_CHAINS, L), 0)
    lane = lax.broadcasted_iota(jnp.int32, (N_CHAINS, L), 1)
    is_fwd = crow < MLSTM_HEADS
    for t in range(GATEPREP_CHUNKS):
        gt = (g_ref[t * L:(t + 1) * L, :] + bias_ref[...]).T
        bb = jnp.dot(jax.nn.log_sigmoid(gt[N_CHAINS:2 * N_CHAINS]), tri, preferred_element_type=F32,
                     precision=lax.Precision.HIGHEST)
        b_f, b_b = bb[:, 0:L], bb[:, L:2 * L]
        b = jnp.where(is_fwd, b_f, b_b)
        r = gt[0:N_CHAINS] - b
        g = jnp.where(is_fwd[:, 0:1], b_f[:, L - 1:L], b_b[:, 0:1])
        cm = jnp.where(is_fwd, _lane_scan_max(r, lane, False), _lane_scan_max(r, lane, True))
        rmax = jnp.max(r, axis=1, keepdims=True)
        rows_ref[t] = jnp.concatenate([r, cm, b, jnp.broadcast_to(g, (N_CHAINS, L)),
                                       jnp.broadcast_to(rmax, (N_CHAINS, L))], axis=0)
        cols_ref[t * L:(t + 1) * L, :] = jnp.concatenate([r, jnp.zeros((L - N_CHAINS, L), F32)], axis=0).T


def _gateprep(gates, gate_bias):
    n = gates.shape[0]
    L = MLSTM_CHUNK
    tn = GATEPREP_CHUNKS * L
    return pl.pallas_call(
        _gateprep_kernel,
        grid=(n // tn,),
        in_specs=[pl.BlockSpec((tn, LANES), lambda i: (i, 0)),
                  pl.BlockSpec((1, LANES), lambda i: (0, 0))],
        out_specs=[pl.BlockSpec((GATEPREP_CHUNKS, GROWS, LANES), lambda i: (i, 0, 0)),
                   pl.BlockSpec((tn, LANES), lambda i: (i, 0))],
        out_shape=[jax.ShapeDtypeStruct((n // L, GROWS, LANES), F32),
                   jax.ShapeDtypeStruct((n, LANES), F32)],
        compiler_params=_cparams("arbitrary"),
        name="gateprep",
    )(gates, gate_bias)


def _mlstm_scores(k, qt, n_ref, ci):
    n_st = n_ref[ci]
    n_hi = n_st.astype(BF16).astype(F32)
    n_lo = (n_st - n_hi).astype(BF16).astype(F32)
    nn = jnp.concatenate([n_hi, n_lo, jnp.zeros((BF16_ROWS - 2, MLSTM_HEAD_DIM), F32)], axis=0).astype(BF16)
    return jnp.dot(k, qt, preferred_element_type=F32), jnp.dot(nn, qt, preferred_element_type=F32)


def _mlstm_chain(kq, qn2, k, qt, vt, rows, cols, mask_sj, c_ref, n_ref, m_ref, ci):
    L = MLSTM_CHUNK
    scale = MLSTM_HEAD_DIM ** -0.5
    r_row = rows[GROW_R + ci:GROW_R + ci + 1]
    cm_row = rows[GROW_CM + ci:GROW_CM + ci + 1]
    b_row = rows[GROW_B + ci:GROW_B + ci + 1]
    g = rows[GROW_G + ci:GROW_G + ci + 1, 0:1]
    rmax = rows[GROW_RMAX + ci:GROW_RMAX + ci + 1, 0:1]
    r_colb = jnp.broadcast_to(cols[:, ci:ci + 1], (L, L))
    c_st = c_ref[ci]
    n_st = n_ref[ci]
    m_st = m_ref[ci][:, 0:1]
    mm = jnp.maximum(cm_row, m_st)
    pt = jnp.where(mask_sj, jnp.exp(r_colb - mm), 0.0)
    st = kq * scale * pt
    s_int = jnp.exp(m_st - mm)
    den = jnp.sum(st, axis=0, keepdims=True) + s_int * (qn2[0:1] + qn2[1:2])
    inv = 1.0 / jnp.maximum(jnp.abs(den), jnp.exp(-(b_row + mm)))
    lhs = jnp.concatenate([vt, c_st.astype(BF16)], axis=1)
    rhs = jnp.concatenate([st.astype(BF16), (qt.astype(F32) * s_int).astype(BF16)], axis=0)
    h = (jnp.dot(lhs, rhs, preferred_element_type=F32) * inv).astype(BF16).T

    w_row = jnp.exp(r_row - rmax)
    wl = jnp.concatenate([(vt.astype(F32) * w_row).astype(BF16),
                          jnp.broadcast_to(w_row, (BF16_ROWS, L)).astype(BF16)], axis=0)
    loc = jnp.dot(wl, k, preferred_element_type=F32) * scale
    m_loc = g + rmax
    m_new = jnp.maximum(g + m_st, m_loc)
    s_old = jnp.exp(g + m_st - m_new)
    s_loc = jnp.exp(m_loc - m_new)
    c_ref[ci] = s_old * c_st + s_loc * loc[0:MLSTM_HEAD_DIM]
    n_ref[ci] = s_old * n_st + s_loc * loc[MLSTM_HEAD_DIM:MLSTM_HEAD_DIM + 1]
    m_ref[ci] = jnp.broadcast_to(m_new, (1, LANES))
    return h


def _mlstm_kernel(kf_ref, kb_ref, qtf_ref, qtb_ref, vtf_ref, vtb_ref, rf_ref, rb_ref, cf_ref, cb_ref,
                  hf_ref, hb_ref, c_ref, n_ref, m_ref):
    L = MLSTM_CHUNK

    @pl.when(pl.program_id(1) == 0)
    def _():
        c_ref[...] = jnp.zeros_like(c_ref)
        n_ref[...] = jnp.zeros_like(n_ref)
        m_ref[...] = jnp.zeros_like(m_ref)

    row = lax.broadcasted_iota(jnp.int32, (L, L), 0)
    col = lax.broadcasted_iota(jnp.int32, (L, L), 1)
    dirs = ((kf_ref, qtf_ref, vtf_ref, rf_ref, cf_ref, hf_ref, row <= col),
            (kb_ref, qtb_ref, vtb_ref, rb_ref, cb_ref, hb_ref, row >= col))

    def operands(ci):
        d, hh = divmod(ci, MLSTM_HEADS)
        sl = slice(hh * MLSTM_HEAD_DIM, (hh + 1) * MLSTM_HEAD_DIM)
        return dirs[d], sl

    def scores(ci):
        (k_ref, qt_ref, _, _, _, _, _), sl = operands(ci)
        return _mlstm_scores(k_ref[0, :, sl], qt_ref[sl, :], n_ref, ci)

    pre = [scores(ci) for ci in range(MLSTM_LOOKAHEAD)]
    for ci in range(N_CHAINS):
        if ci + MLSTM_LOOKAHEAD < N_CHAINS:
            pre.append(scores(ci + MLSTM_LOOKAHEAD))
        kq, qn2 = pre.pop(0)
        (k_ref, qt_ref, vt_ref, r_ref, cl_ref, o_ref, mask), sl = operands(ci)
        o_ref[0, :, sl] = _mlstm_chain(kq, qn2, k_ref[0, :, sl], qt_ref[sl, :], vt_ref[sl, :],
                                       r_ref[0], cl_ref[0], mask, c_ref, n_ref, m_ref, ci)


def _mlstm(k, qt, vt, grows, gcols):
    bsz, seq, _ = k.shape
    L = MLSTM_CHUNK
    nc = seq // L
    fwd = lambda b, c: (b, c, 0)
    bwd = lambda b, c: (b, nc - 1 - c, 0)
    fwd_t = lambda b, c: (0, b * nc + c)
    bwd_t = lambda b, c: (0, b * nc + nc - 1 - c)
    fwd_r = lambda b, c: (b * nc + c, 0, 0)
    bwd_r = lambda b, c: (b * nc + nc - 1 - c, 0, 0)
    return pl.pallas_call(
        _mlstm_kernel,
        grid=(bsz, nc),
        in_specs=[pl.BlockSpec((1, L, MLSTM_WIDTH), fwd), pl.BlockSpec((1, L, MLSTM_WIDTH), bwd),
                  pl.BlockSpec((MLSTM_WIDTH, L), fwd_t), pl.BlockSpec((MLSTM_WIDTH, L), bwd_t),
                  pl.BlockSpec((MLSTM_WIDTH, L), fwd_t), pl.BlockSpec((MLSTM_WIDTH, L), bwd_t),
                  pl.BlockSpec((1, GROWS, LANES), fwd_r), pl.BlockSpec((1, GROWS, LANES), bwd_r),
                  pl.BlockSpec((1, L, LANES), fwd), pl.BlockSpec((1, L, LANES), bwd)],
        out_specs=[pl.BlockSpec((1, L, MLSTM_WIDTH), fwd),
                   pl.BlockSpec((1, L, MLSTM_WIDTH), bwd)],
        out_shape=[jax.ShapeDtypeStruct((bsz, seq, MLSTM_WIDTH), BF16),
                   jax.ShapeDtypeStruct((bsz, seq, MLSTM_WIDTH), BF16)],
        scratch_shapes=[pltpu.VMEM((N_CHAINS, MLSTM_HEAD_DIM, MLSTM_HEAD_DIM), F32),
                        pltpu.VMEM((N_CHAINS, 1, MLSTM_HEAD_DIM), F32),
                        pltpu.VMEM((N_CHAINS, 1, LANES), F32)],
        compiler_params=_cparams("arbitrary", "arbitrary"),
        name="mlstm",
    )(k, k, qt, qt, vt, vt, grows, grows, gcols, gcols)


NA_QROWS = 8
NA_KROWS = 2 * NA_QROWS
NA_KBLK = 4
NA_NKBLK = NA_KROWS // NA_KBLK
NA_CODE_SECOND = 2 * NA_MAX_KH - 2
NA_CODE_FIRST = NA_CODE_SECOND + NA_MAX_KH
NA_CODE_NONE = NA_CODE_FIRST + NA_MAX_KH
NA_NCODES = NA_CODE_NONE + 1


NA_LIVE_PAIRS = NA_MAX_KH // 2 + 1


def _natten_kernel(q_ref, k0_ref, k1_ref, k2_ref, k3_ref, v0_ref, v1_ref, v2_ref, v3_ref, bias_ref, o_ref,
                   st0_ref, st1_ref, pt0_ref, pt1_ref, *, rows):
    i = pl.program_id(1)
    half = NA_MAX_KH // 2
    rq0 = i * NA_QROWS
    w0 = jnp.clip(rq0 - half, 0, rows - NA_KROWS)
    nqp = NA_QROWS // 2
    nkp = NA_KROWS // 2
    tq = NA_QROWS * GRID_W

    def window_start(r):
        return jnp.clip(r - half, 0, rows - NA_MAX_KH)

    def code(rq, kp):
        r = rq0 + rq
        r0 = window_start(r)
        kr = w0 + 2 * kp
        d0 = kr - r + (NA_MAX_KH - 1)
        va = (kr >= r0) & (kr < r0 + NA_MAX_KH)
        vb = (kr + 1 >= r0) & (kr + 1 < r0 + NA_MAX_KH)
        return jnp.where(va & vb, d0,
                         jnp.where(vb, NA_CODE_SECOND + d0 + 1,
                                   jnp.where(va, NA_CODE_FIRST + d0 - (NA_MAX_KH - 1), NA_CODE_NONE)))

    first = [jnp.minimum((window_start(rq0 + 2 * qi) - w0) // 2, nkp - NA_LIVE_PAIRS) for qi in range(nqp)]
    codes = [[(code(2 * qi, first[qi] + t), code(2 * qi + 1, first[qi] + t)) for t in range(NA_LIVE_PAIRS)]
             for qi in range(nqp)]
    lane_t = lax.broadcasted_iota(jnp.int32, (LANES, LANES), 1)
    lane_q = lax.broadcasted_iota(jnp.int32, (tq, LANES), 1)
    sub_o = lax.broadcasted_iota(jnp.int32, (LANES, tq), 0)
    scale = jnp.asarray(NA_HEAD_DIM ** -0.5, BF16)
    k_refs = (k0_ref, k1_ref, k2_ref, k3_ref)
    v_refs = (v0_ref, v1_ref, v2_ref, v3_ref)
    st_refs = (st0_ref, st1_ref)
    pt_refs = (pt0_ref, pt1_ref)
    pt0_ref[...] = jnp.zeros_like(pt0_ref)
    pt1_ref[...] = jnp.zeros_like(pt1_ref)

    def lanes_of(hp):
        return slice(hp * LANES, (hp + 1) * LANES)

    def scores(h):
        hp, sb = divmod(h, 2)
        lsl = lanes_of(hp)
        qp = q_ref[0, :, lsl] * scale
        own = (lane_q >= sb * NA_HEAD_DIM) & (lane_q < (sb + 1) * NA_HEAD_DIM)
        qm = jnp.where(own, qp, jnp.zeros_like(qp))
        k_all = jnp.concatenate([r[0, :, lsl] for r in k_refs], axis=0)
        st_refs[sb][...] = lax.dot_general(k_all, qm, (((1,), (1,)), ((), ())), preferred_element_type=F32)

    def attend(h):
        hp, sb = divmod(h, 2)
        st_ref, pt_ref = st_refs[sb], pt_refs[sb]
        invs = []
        for qi in range(nqp):
            qsl = slice(qi * LANES, (qi + 1) * LANES)
            offs = [pl.multiple_of((first[qi] + t) * LANES, LANES) for t in range(NA_LIVE_PAIRS)]
            tiles = []
            for t in range(NA_LIVE_PAIRS):
                ca, cb = codes[qi][t]
                bias = jnp.where(lane_t < GRID_W, bias_ref[h, ca], bias_ref[h, cb])
                tiles.append(st_ref[pl.ds(offs[t], LANES), qsl] + bias)
            m = functools.reduce(jnp.maximum, tiles)
            m = jnp.max(m, axis=0, keepdims=True)
            es = [jnp.exp(t - m) for t in tiles]
            l = jnp.sum(functools.reduce(jnp.add, es), axis=0, keepdims=True)
            invs.append(1.0 / l)
            for t in range(NA_LIVE_PAIRS):
                pt_ref[pl.ds(offs[t], LANES), qsl] = es[t].astype(BF16)
        vt = jnp.concatenate([r[0, :, lanes_of(hp)] for r in v_refs], axis=0).T
        return jnp.dot(vt, pt_ref[...], preferred_element_type=F32) * jnp.concatenate(invs, axis=1)

    scores(0)
    outs = []
    for h in range(NA_HEADS):
        if h + 1 < NA_HEADS:
            scores(h + 1)
        outs.append(attend(h))
        if h % 2 == 1:
            ot = jnp.where(sub_o < NA_HEAD_DIM, outs[0], outs[1])
            o_ref[0, :, lanes_of(h // 2)] = ot.T.astype(o_ref.dtype)
            outs = []


def _natten_bias_table(rel_bias):
    cols = jnp.arange(GRID_W)
    col_start = jnp.clip(cols - NA_KW // 2, 0, GRID_W - NA_KW)
    kcol = jnp.arange(GRID_W)
    dc = kcol[:, None] - cols[None, :] + (NA_KW - 1)
    inwin = (kcol[:, None] >= col_start[None, :]) & (kcol[:, None] < col_start[None, :] + NA_KW)
    tab = rel_bias.astype(F32)[:, :, jnp.clip(dc, 0, 2 * NA_KW - 2)]
    tab = jnp.where(inwin[None, None], tab, NEG_BIG)
    neg = jnp.full_like(tab[:, :NA_MAX_KH], NEG_BIG)
    both = jnp.concatenate([tab[:, :-1], tab[:, 1:]], axis=2)
    second = jnp.concatenate([neg, tab[:, :NA_MAX_KH]], axis=2)
    first = jnp.concatenate([tab[:, NA_MAX_KH - 1:], neg], axis=2)
    none = jnp.concatenate([neg[:, :1], neg[:, :1]], axis=2)
    t = jnp.concatenate([both, second, first, none], axis=1)
    return jnp.concatenate([t, t], axis=-1)


def _natten(na, bias_table):
    bsz, seq, _ = na.shape
    rows = seq // GRID_W
    tq = NA_QROWS * GRID_W
    tk = NA_KBLK * GRID_W
    nkb = seq // tk

    def kv_spec(j, col):
        return pl.BlockSpec((1, tk, NA_WIDTH),
                            lambda b, i: (b, jnp.clip(2 * i - 1, 0, nkb - NA_NKBLK) + j, col))

    return pl.pallas_call(
        functools.partial(_natten_kernel, rows=rows),
        grid=(bsz, rows // NA_QROWS),
        in_specs=[pl.BlockSpec((1, tq, NA_WIDTH), lambda b, i: (b, i, 0))]
                 + [kv_spec(j, 1) for j in range(NA_NKBLK)]
                 + [kv_spec(j, 2) for j in range(NA_NKBLK)]
                 + [pl.BlockSpec(bias_table.shape, lambda b, i: (0, 0, 0, 0), pipeline_mode=pl.Buffered(1))],
        out_specs=pl.BlockSpec((1, tq, NA_WIDTH), lambda b, i: (b, i, 0)),
        out_shape=jax.ShapeDtypeStruct((bsz, seq, NA_WIDTH), BF16),
        scratch_shapes=[pltpu.VMEM((NA_KROWS * GRID_W, tq), F32), pltpu.VMEM((NA_KROWS * GRID_W, tq), F32),
                        pltpu.VMEM((NA_KROWS * GRID_W, tq), BF16), pltpu.VMEM((NA_KROWS * GRID_W, tq), BF16)],
        compiler_params=_cparams("arbitrary", "arbitrary"),
        name="natten",
    )(na, *([na] * (2 * NA_NKBLK)), bias_table)


AUG = LANES
GATE_LO_SHIFT = N_EXPERTS


def _postmix_kernel(hf_ref, hb_ref, o_ref, hn_ref, x_ref, mod_ref, gm_ref, wm_ref, wn_ref,
                    gpost_ref, gffn_ref, wr_ref, x1_ref, x2_ref, aff_ref):
    d = x_ref.shape[-1]
    s = hf_ref[...].astype(F32) + hb_ref[...].astype(F32)
    og = jax.nn.sigmoid(o_ref[...].astype(F32))
    heads = []
    for hh in range(MLSTM_HEADS):
        sl = slice(hh * MLSTM_HEAD_DIM, (hh + 1) * MLSTM_HEAD_DIM)
        heads.append(_rms(s[:, sl]) * gm_ref[:, sl] * og[:, sl])
    hm = jnp.concatenate(heads, axis=-1).astype(BF16)
    mix = (jnp.dot(hm, wm_ref[...], preferred_element_type=F32)
           + jnp.dot(hn_ref[...], wn_ref[...], preferred_element_type=F32))
    gate1 = mod_ref[0, 2:3, :]
    shift2 = mod_ref[0, 3:4, :]
    scale2 = mod_ref[0, 4:5, :]
    x1 = x_ref[...] + gate1 * (_rms(mix) * gpost_ref[...])
    x1_ref[...] = x1
    h2 = (_rms(x1) * gffn_ref[...]) * (1.0 + scale2) + shift2
    h2b = h2.astype(BF16)
    x2_ref[:, 0:d] = h2b
    logits = jnp.dot(h2b, wr_ref[...], preferred_element_type=F32)
    lane = lax.broadcasted_iota(jnp.int32, logits.shape, 1)
    logits = jnp.where(lane < N_EXPERTS, logits, NEG_BIG)
    ex = jnp.exp(logits - jnp.max(logits, axis=-1, keepdims=True))
    aff = ex / jnp.sum(ex, axis=-1, keepdims=True)
    aff_ref[...] = aff
    hi = aff.astype(BF16).astype(F32)
    lo = (aff - hi).astype(BF16).astype(F32)
    x2_ref[:, d:d + AUG] = (hi + pltpu.roll(lo, GATE_LO_SHIFT, 1)).astype(BF16)


def _postmix(hf, hb, o, hn, x2d, mod, g_mlstm, w_om, w_on, g_post, g_ffn, w_r, seq, tm=512):
    n, d = x2d.shape
    tiles_per_b = seq // tm
    full = lambda a: pl.BlockSpec(a.shape, lambda i: (0, 0))
    tok = lambda w: pl.BlockSpec((tm, w), lambda i: (i, 0))
    return pl.pallas_call(
        _postmix_kernel,
        grid=(n // tm,),
        in_specs=[tok(MLSTM_WIDTH), tok(MLSTM_WIDTH), tok(MLSTM_WIDTH), tok(NA_WIDTH), tok(d),
                  pl.BlockSpec((1, N_MOD, d), lambda i: (i // tiles_per_b, 0, 0)),
                  full(g_mlstm), full(w_om), full(w_on), full(g_post), full(g_ffn), full(w_r)],
        out_specs=[tok(d), tok(d + AUG), tok(LANES)],
        out_shape=[jax.ShapeDtypeStruct((n, d), F32),
                   jax.ShapeDtypeStruct((n, d + AUG), BF16),
                   jax.ShapeDtypeStruct((n, LANES), F32)],
        compiler_params=_cparams("arbitrary"),
        name="postmix",
    )(hf, hb, o, hn, x2d, mod, g_mlstm, w_om, w_on, g_post, g_ffn, w_r)


TOK_TILE = 128
SEL_STEP = 1024
UNSELECTED = -1e6
SELECT_HI = 2.0
SELECT_ITERS = 160


def _select_kernel(afft_ref, aff_ref, pos_ref, post_ref, tb_ref, thr_ref, need_ref, base_ref, *, cap):
    step = pl.program_id(0)
    ne = afft_ref.shape[0]

    @pl.when(step == 0)
    def _():
        a_t = afft_ref[...]

        def count_ge(v):
            return jnp.sum(jnp.where(a_t >= v, 1.0, 0.0), axis=1, keepdims=True)

        def body(it, lh):
            lo, hi = lh
            mid = 0.5 * (lo + hi)
            ge = count_ge(mid) >= cap
            return jnp.where(ge, mid, lo), jnp.where(ge, hi, mid)

        lo_col, hi_col = lax.fori_loop(0, SELECT_ITERS, body,
                                       (jnp.zeros((ne, 1), F32), jnp.full((ne, 1), SELECT_HI, F32)))
        need_col = cap - count_ge(hi_col)
        sub = lax.broadcasted_iota(jnp.int32, (ne, LANES), 0)
        ln = lax.broadcasted_iota(jnp.int32, (ne, LANES), 1)
        diag = sub == ln
        thr_ref[...] = jnp.sum(jnp.where(diag, lo_col, 0.0), axis=0, keepdims=True)
        need_ref[...] = jnp.sum(jnp.where(diag, need_col, 0.0), axis=0, keepdims=True)
        base_ref[...] = jnp.zeros_like(base_ref)
        base_ref[2:3, :] = jnp.sum(jnp.where(diag, hi_col, 0.0), axis=0, keepdims=True)

    thr_lo = thr_ref[...]
    thr_hi = base_ref[2:3, :]
    need = need_ref[...]
    lane = lax.broadcasted_iota(jnp.int32, (TOK_TILE, LANES), 1)
    valid = lane < ne
    row = lax.broadcasted_iota(jnp.int32, (TOK_TILE, TOK_TILE), 0)
    col = lax.broadcasted_iota(jnp.int32, (TOK_TILE, TOK_TILE), 1)
    strict_lower = (col < row).astype(BF16)
    base_gt = base_ref[0:1, :]
    base_eq = base_ref[1:2, :]
    tb_rows = []
    for t in range(SEL_STEP // TOK_TILE):
        a = aff_ref[t * TOK_TILE:(t + 1) * TOK_TILE, :]
        gt = (a >= thr_hi) & valid
        eq = (a >= thr_lo) & (a < thr_hi) & valid
        gtf = jnp.where(gt, 1.0, 0.0)
        eqf = jnp.where(eq, 1.0, 0.0)
        cg = jnp.dot(strict_lower, gtf.astype(BF16), preferred_element_type=F32)
        ce = jnp.dot(strict_lower, eqf.astype(BF16), preferred_element_type=F32)
        eq_rank = base_eq + ce
        sel = gt | (eq & (eq_rank < need))
        pos = base_gt + cg + jnp.minimum(eq_rank, need)
        posm = jnp.where(sel, pos, UNSELECTED)
        pos_ref[t * TOK_TILE:(t + 1) * TOK_TILE, :] = posm
        post_ref[:, t * TOK_TILE:(t + 1) * TOK_TILE] = posm.T[0:ne, :]
        tb_rows.append(base_gt + jnp.minimum(base_eq, need))
        base_gt = base_gt + jnp.sum(gtf, axis=0, keepdims=True)
        base_eq = base_eq + jnp.sum(eqf, axis=0, keepdims=True)
    tb_ref[...] = jnp.concatenate(tb_rows, axis=0).astype(jnp.int32)
    base_ref[0:1, :] = base_gt
    base_ref[1:2, :] = base_eq


def _select(aff, cap):
    n = aff.shape[0]
    afft = aff[:, :N_EXPERTS].T
    nsteps = n // SEL_STEP
    tiles_per_step = SEL_STEP // TOK_TILE
    return pl.pallas_call(
        functools.partial(_select_kernel, cap=cap),
        grid=(nsteps,),
        in_specs=[pl.BlockSpec((N_EXPERTS, n), lambda i: (0, 0)),
                  pl.BlockSpec((SEL_STEP, LANES), lambda i: (i, 0))],
        out_specs=[pl.BlockSpec((SEL_STEP, LANES), lambda i: (i, 0)),
                   pl.BlockSpec((N_EXPERTS, SEL_STEP), lambda i: (0, i)),
                   pl.BlockSpec((tiles_per_step, LANES), lambda i: (i, 0))],
        out_shape=[jax.ShapeDtypeStruct((n, LANES), F32),
                   jax.ShapeDtypeStruct((N_EXPERTS, n), F32),
                   jax.ShapeDtypeStruct((n // TOK_TILE, LANES), jnp.int32)],
        scratch_shapes=[pltpu.VMEM((1, LANES), F32), pltpu.VMEM((1, LANES), F32),
                        pltpu.VMEM((SUBLANES, LANES), F32)],
        compiler_params=_cparams("arbitrary"),
        name="select",
    )(afft, aff)


DISP_TILE = 2 * TOK_TILE
CARRY = SUBLANES
DISP_FAST = 96
DISP_SLOW = CARRY + DISP_TILE + CARRY
XE_DTYPE = F32
XE_PAD = 2 * DISP_TILE
DISP_SLOW_GROUP = 4
DISP_STAGE_ROWS = max(N_EXPERTS * DISP_FAST, DISP_SLOW_GROUP * DISP_SLOW)


def _dispatch_kernel(tb_ref, x_ref, post_ref, xe_ref, stage_ref, carry_ref, sem, flag_ref, *, cap, ntiles):
    i = pl.program_id(0)
    slot = lax.rem(i, 2)
    ne = N_EXPERTS

    @pl.when(i == 0)
    def _():
        carry_ref[...] = jnp.zeros_like(carry_ref)
        stage_ref[1, 0:XE_PAD, :] = jnp.zeros((XE_PAD, stage_ref.shape[-1]), stage_ref.dtype)
        pads = [pltpu.make_async_copy(stage_ref.at[1, pl.ds(0, XE_PAD)], xe_ref.at[e, pl.ds(cap, XE_PAD)],
                                      sem.at[1]) for e in range(ne)]
        for cp in pads:
            cp.start()
        for cp in pads:
            cp.wait()

    fill = [tb_ref[i * ne + e] for e in range(ne)]
    nxt = [jnp.where(i + 1 < ntiles, tb_ref[jnp.minimum(i + 1, ntiles - 1) * ne + e], cap) for e in range(ne)]
    al = [(f // CARRY) * CARRY for f in fill]
    need_max = functools.reduce(jnp.maximum, [nx - a for nx, a in zip(nxt, al)])
    slow = need_max > DISP_FAST

    def copies(s_rows, sl, experts, als):
        return [pltpu.make_async_copy(stage_ref.at[sl, pl.ds(k * s_rows, s_rows)],
                                      xe_ref.at[e, pl.ds(pl.multiple_of(als[e], CARRY), s_rows)],
                                      sem.at[sl]) for k, e in enumerate(experts)]

    def wait_in_flight(sl):
        @pl.when(flag_ref[sl] == 0)
        def _():
            for cp in copies(DISP_FAST, sl, range(ne), [0] * ne):
                cp.wait()

    def stage(s_rows, experts, xt, cb):
        slot_of_tok = lax.broadcasted_iota(jnp.int32, (s_rows, DISP_TILE), 0).astype(F32)
        sub = lax.broadcasted_iota(jnp.int32, (s_rows, ne * CARRY), 0)
        ln = lax.broadcasted_iota(jnp.int32, (s_rows, ne * CARRY), 1)
        a_tok, a_car = [], []
        for e in experts:
            rel = post_ref[e:e + 1, :] - al[e].astype(F32)
            a_tok.append(jnp.where(slot_of_tok == rel, 1.0, 0.0))
            own = (ln >= e * CARRY) & (ln < e * CARRY + (fill[e] - al[e])) & (ln - e * CARRY == sub)
            a_car.append(jnp.where(own, 1.0, 0.0))
        stage_ref[slot, 0:len(experts) * s_rows, :] = (
            jnp.dot(jnp.concatenate(a_tok, axis=0).astype(BF16), xt, preferred_element_type=F32)
            + jnp.dot(jnp.concatenate(a_car, axis=0).astype(BF16), cb, preferred_element_type=F32)
        ).astype(stage_ref.dtype)
        for k, e in enumerate(experts):
            off = jnp.minimum((nxt[e] // CARRY) * CARRY - al[e], s_rows - CARRY)
            carry_ref[e * CARRY:(e + 1) * CARRY, :] = stage_ref[slot, pl.ds(pl.multiple_of(k * s_rows + off, CARRY), CARRY), :]

    @pl.when(jnp.logical_not(slow))
    def _():
        stage(DISP_FAST, range(ne), x_ref[...], carry_ref[...].astype(BF16))

        @pl.when(i > 0)
        def _():
            wait_in_flight(1 - slot)
        for cp in copies(DISP_FAST, slot, range(ne), al):
            cp.start()
        flag_ref[slot] = 0

    @pl.when(slow)
    def _():
        xt = x_ref[...]
        cb = carry_ref[...].astype(BF16)

        @pl.when(i > 0)
        def _():
            wait_in_flight(1 - slot)
        for e0 in range(0, ne, DISP_SLOW_GROUP):
            experts = range(e0, e0 + DISP_SLOW_GROUP)
            stage(DISP_SLOW, experts, xt, cb)
            cps = copies(DISP_SLOW, slot, experts, al)
            for cp in cps:
                cp.start()
            for cp in cps:
                cp.wait()
        flag_ref[slot] = 1

    @pl.when(i == ntiles - 1)
    def _():
        wait_in_flight(slot)


def _dispatch(x2aug, post, tb, cap):
    n, wd = x2aug.shape
    ntiles = n // DISP_TILE
    tb_flat = tb[::DISP_TILE // TOK_TILE, :N_EXPERTS].reshape(-1)
    return pl.pallas_call(
        functools.partial(_dispatch_kernel, cap=cap, ntiles=ntiles),
        grid_spec=pltpu.PrefetchScalarGridSpec(
            num_scalar_prefetch=1,
            grid=(ntiles,),
            in_specs=[pl.BlockSpec((DISP_TILE, wd), lambda i, tb: (i, 0)),
                      pl.BlockSpec((N_EXPERTS, DISP_TILE), lambda i, tb: (0, i))],
            out_specs=pl.BlockSpec(memory_space=pl.ANY),
            scratch_shapes=[pltpu.VMEM((2, DISP_STAGE_ROWS, wd), XE_DTYPE),
                            pltpu.VMEM((N_EXPERTS * CARRY, wd), XE_DTYPE),
                            pltpu.SemaphoreType.DMA((2,)),
                            pltpu.SMEM((2,), jnp.int32)]),
        out_shape=jax.ShapeDtypeStruct((N_EXPERTS, cap + XE_PAD, wd), XE_DTYPE),
        compiler_params=_cparams("arbitrary"),
        name="dispatch",
    )(tb_flat, x2aug, post)


EXP_ROWS = 512


def _experts_kernel(xe_ref, wg_ref, wu_ref, wd_ref, ye_ref, wgb_ref, wub_ref, wdb_ref):
    e = pl.program_id(0)
    d = wg_ref.shape[1]

    @pl.when(pl.program_id(1) == 0)
    def _():
        wgb_ref[...] = wg_ref[0].astype(BF16)
        wub_ref[...] = wu_ref[0].astype(BF16)
        wdb_ref[...] = wd_ref[0].astype(BF16)

    x = xe_ref[0, :, 0:d].astype(BF16)
    aug = xe_ref[0, :, d:d + AUG].astype(F32)
    lane = lax.broadcasted_iota(jnp.int32, aug.shape, 1)
    mine = (lane == e) | (lane == e + GATE_LO_SHIFT)
    gate = jnp.sum(jnp.where(mine, aug, 0.0), axis=-1, keepdims=True)
    g = jnp.dot(x, wgb_ref[...], preferred_element_type=F32)
    u = jnp.dot(x, wub_ref[...], preferred_element_type=F32)
    hid = (g * jax.nn.sigmoid(g)) * u
    y = jnp.dot(hid.astype(BF16), wdb_ref[...], preferred_element_type=F32)
    ye_ref[0] = (y * gate).astype(ye_ref.dtype)


def _experts(xe, w_gate, w_up, w_down, cap):
    ne, _, wd = xe.shape
    d = w_gate.shape[1]
    f = w_gate.shape[2]
    wspec = lambda s: pl.BlockSpec((1,) + s, lambda e, j: (e, 0, 0))
    rows = min(EXP_ROWS, cap)
    return pl.pallas_call(
        _experts_kernel,
        grid=(ne, cap // rows),
        in_specs=[pl.BlockSpec((1, rows, wd), lambda e, j: (e, j, 0)),
                  wspec((d, f)), wspec((d, f)), wspec((f, d))],
        out_specs=pl.BlockSpec((1, rows, d), lambda e, j: (e, j, 0)),
        out_shape=jax.ShapeDtypeStruct((ne, cap, d), BF16),
        scratch_shapes=[pltpu.VMEM((d, f), BF16), pltpu.VMEM((d, f), BF16), pltpu.VMEM((f, d), BF16)],
        compiler_params=_cparams("arbitrary", "arbitrary"),
        name="experts",
    )(xe, w_gate, w_up, w_down)


COMB_TILE = 2 * TOK_TILE
COMB_FAST = 128
COMB_SLOW = 384
COMB_ALIGN = BF16_ROWS


def _combine_kernel(tb_ref, pos_ref, x1_ref, mod_ref, g_ref, ye_ref, y_ref, slab_ref, sem, *, cap, ntiles):
    i = pl.program_id(0)
    slot = lax.rem(i, 2)
    ne = N_EXPERTS

    def info(t):
        t = jnp.minimum(t, ntiles - 1)
        fill = [tb_ref[t * ne + e] for e in range(ne)]
        nxt = [jnp.where(t + 1 < ntiles, tb_ref[jnp.minimum(t + 1, ntiles - 1) * ne + e], cap) for e in range(ne)]
        st_f = [jnp.minimum((f // COMB_ALIGN) * COMB_ALIGN, cap - COMB_FAST) for f in fill]
        st_s = [jnp.minimum((f // COMB_ALIGN) * COMB_ALIGN, cap - COMB_SLOW) for f in fill]
        slow = functools.reduce(jnp.maximum, [nx - s for nx, s in zip(nxt, st_f)]) > COMB_FAST
        return st_f, st_s, slow

    def copies(s_rows, sl, starts):
        return [pltpu.make_async_copy(ye_ref.at[e, pl.ds(pl.multiple_of(starts[e], COMB_ALIGN), s_rows)],
                                      slab_ref.at[sl, pl.ds(e * s_rows, s_rows)],
                                      sem.at[sl]) for e in range(ne)]

    def fetch(t, sl):
        st_f, st_s, slow = info(t)

        @pl.when(slow)
        def _():
            for cp in copies(COMB_SLOW, sl, st_s):
                cp.start()

        @pl.when(jnp.logical_not(slow))
        def _():
            for cp in copies(COMB_FAST, sl, st_f):
                cp.start()

    @pl.when(i == 0)
    def _():
        fetch(i, slot)

    @pl.when(i + 1 < ntiles)
    def _():
        fetch(i + 1, 1 - slot)

    st_f, st_s, slow = info(i)
    lane = lax.broadcasted_iota(jnp.int32, (COMB_TILE, LANES), 1).astype(F32)
    pos = pos_ref[...]

    def finish(moe):
        gate2 = mod_ref[0, 5:6, :]
        y_ref[...] = x1_ref[...] + gate2 * (_rms(moe) * g_ref[...])

    @pl.when(jnp.logical_not(slow))
    def _():
        for cp in copies(COMB_FAST, slot, st_f):
            cp.wait()
        per = LANES // COMB_FAST
        groups = []
        for gi in range(ne // per):
            val = None
            for k in range(per):
                e = gi * per + k
                rel = pos[:, e:e + 1] - st_f[e].astype(F32) + float(k * COMB_FAST)
                inband = (lane >= k * COMB_FAST) & (lane < (k + 1) * COMB_FAST)
                hit = inband & (lane == rel)
                val = hit if val is None else (val | hit)
            groups.append(jnp.where(val, 1.0, 0.0).astype(BF16))
        onehot = jnp.concatenate(groups, axis=1)
        finish(jnp.dot(onehot, slab_ref[slot, 0:ne * COMB_FAST, :], preferred_element_type=F32))

    @pl.when(slow)
    def _():
        for cp in copies(COMB_SLOW, slot, st_s):
            cp.wait()
        per = COMB_SLOW // LANES
        groups = []
        for e in range(ne):
            rel = pos[:, e:e + 1] - st_s[e].astype(F32)
            for k in range(per):
                groups.append(jnp.where(lane + float(k * LANES) == rel, 1.0, 0.0).astype(BF16))
        onehot = jnp.concatenate(groups, axis=1)
        finish(jnp.dot(onehot, slab_ref[slot], preferred_element_type=F32))


def _combine(tb, pos, x1, mod, g_post_ffn, ye, cap, seq):
    n, d = x1.shape
    assert cap >= COMB_SLOW
    ntiles = n // COMB_TILE
    tiles_per_b = seq // COMB_TILE
    tb_flat = tb[::COMB_TILE // TOK_TILE, :N_EXPERTS].reshape(-1)
    return pl.pallas_call(
        functools.partial(_combine_kernel, cap=cap, ntiles=ntiles),
        grid_spec=pltpu.PrefetchScalarGridSpec(
            num_scalar_prefetch=1,
            grid=(ntiles,),
            in_specs=[pl.BlockSpec((COMB_TILE, LANES), lambda i, tb: (i, 0)),
                      pl.BlockSpec((COMB_TILE, d), lambda i, tb: (i, 0)),
                      pl.BlockSpec((1, N_MOD, d), lambda i, tb: (i // tiles_per_b, 0, 0)),
                      pl.BlockSpec((1, d), lambda i, tb: (0, 0)),
                      pl.BlockSpec(memory_space=pl.ANY)],
            out_specs=pl.BlockSpec((COMB_TILE, d), lambda i, tb: (i, 0)),
            scratch_shapes=[pltpu.VMEM((2, N_EXPERTS * COMB_SLOW, d), BF16),
                            pltpu.SemaphoreType.DMA((2,))]),
        out_shape=jax.ShapeDtypeStruct((n, d), F32),
        compiler_params=_cparams("arbitrary"),
        name="combine",
    )(tb_flat, pos, x1, mod, g_post_ffn, ye)


def _prep_weights(w_in, b_gates, g_mlstm, na_rel_bias, w_out, w_router):
    w = MLSTM_WIDTH
    ng = 4 * MLSTM_HEADS
    w_qt = w_in[:, 0:w].T.astype(BF16)
    w_k = w_in[:, w:2 * w].astype(BF16)
    w_vt = w_in[:, 2 * w:3 * w].T.astype(BF16)
    w_o = w_in[:, 3 * w:4 * w].astype(BF16)
    perm = [d * 2 * MLSTM_HEADS + g * MLSTM_HEADS + h
            for g in range(2) for d in range(2) for h in range(MLSTM_HEADS)]
    perm = jnp.asarray(perm)
    w_g = jnp.pad(w_in[:, 4 * w:4 * w + ng][:, perm], ((0, 0), (0, LANES - ng))).astype(BF16)
    b_g = jnp.pad(b_gates.reshape(-1)[perm].astype(F32), (0, LANES - ng)).reshape(1, LANES)
    w_n = w_in[:, 4 * w + ng:].astype(BF16)
    w_om = w_out[0:w].astype(BF16)
    w_on = w_out[w:].astype(BF16)
    w_r = jnp.pad(w_router, ((0, 0), (0, LANES - N_EXPERTS))).astype(BF16)
    return dict(w_qt=w_qt, w_k=w_k, w_vt=w_vt, w_o=w_o, w_g=w_g, b_g=b_g, w_n=w_n, w_om=w_om, w_on=w_on, w_r=w_r,
                g_mlstm=g_mlstm.reshape(1, w).astype(F32), bias_table=_natten_bias_table(na_rel_bias))


def _layer(x, mod, pw, g_pre_mix, g_post_mix, g_pre_ffn, g_post_ffn, w_eg, w_eu, w_ed):
    bsz, seq, d = x.shape
    n = bsz * seq
    x2d = x.reshape(n, d)
    row = lambda g: g.reshape(1, d).astype(F32)
    qt, k, vt, o, gates, na = _inproj(x2d, mod, row(g_pre_mix), pw["w_qt"], pw["w_k"], pw["w_vt"], pw["w_o"],
                                      pw["w_g"], pw["w_n"], seq)
    grows, gcols = _gateprep(gates, pw["b_g"])
    hf, hb = _mlstm(k.reshape(bsz, seq, -1), qt, vt, grows, gcols.reshape(bsz, seq, -1))
    hn = _natten(na.reshape(bsz, seq, -1), pw["bias_table"])
    x1, x2aug, aff = _postmix(hf.reshape(n, -1), hb.reshape(n, -1), o, hn.reshape(n, -1), x2d, mod,
                              pw["g_mlstm"], pw["w_om"], pw["w_on"], row(g_post_mix), row(g_pre_ffn),
                              pw["w_r"], seq)
    cap = EC_CAPACITY_FACTOR * n // N_EXPERTS
    pos, post, tb = _select(aff, cap)
    xe = _dispatch(x2aug, post, tb, cap)
    ye = _experts(xe, w_eg, w_eu, w_ed, cap)
    y = _combine(tb, pos, x1, mod, row(g_post_ffn), ye, cap, seq)
    return y.reshape(bsz, seq, d)


def kernel(x_prompt, x_sample, c_prompt, c_sample, w_ada, b_ada, g_pre_mix, g_post_mix, w_in, b_gates, g_mlstm, na_rel_bias, w_out, g_pre_ffn, g_post_ffn, w_router, w_expert_gate, w_expert_up, w_expert_down):
    depth = w_ada.shape[0]
    nb = x_prompt.shape[0]
    xs = [x_prompt, x_sample]
    cs = jnp.concatenate([c_prompt, c_sample], axis=0)
    for l in range(depth):
        mod = _mod(cs, w_ada[l], b_ada[l])
        pw = _prep_weights(w_in[l], b_gates[l], g_mlstm[l], na_rel_bias[l], w_out[l], w_router[l])
        mods = [mod[:nb], mod[nb:]]
        xs = [_layer(x, m, pw, g_pre_mix[l], g_post_mix[l], g_pre_ffn[l], g_post_ffn[l],
                     w_expert_gate[l], w_expert_up[l], w_expert_down[l]) for x, m in zip(xs, mods)]
    return (xs[0], xs[1])
```
